```python
import math
import jax
import jax.numpy as jnp
from jax import lax
import numpy as np

D_MODEL = 1024
BATCH = 8
SEQ = 4096
DEPTH = 4
DEC_BATCH = 8
DEC_SEQ = 32
PAST_LEN = 2048

F32 = jnp.float32
CHUNK = 64
Q_BLOCK = 128
N_MIXERS = 4
DEEPNORM_ALPHA = (2.0 * DEPTH) ** 0.25
DEEPNORM_BETA = (8.0 * DEPTH) ** -0.25
LN_EPS = 1e-5
NORM_EPS = 1e-6
NEG_INF = -1e30

H_G = 8
DK_G = D_MODEL // H_G
DV_G = D_MODEL // H_G
CONV_W = 4
H_F = 16
DH_F = D_MODEL // H_F
H_D = 8
DH_D = D_MODEL // (2 * H_D)
DIFF_LAYER = 2
LAMBDA_INIT = 0.8 - 0.6 * math.exp(-0.3 * DIFF_LAYER)
N_BUCKETS = 32
MAX_DISTANCE = 128
H_R = 4
DK_R = D_MODEL // H_R
DV_R = 2 * D_MODEL // H_R
ROPE_BASE = 10000.0

kernel_name = 'hybrid_streaming_encoder_step'


def _layer_norm(x, g, b):
    xf = x.astype(F32)
    mu = jnp.mean(xf, axis=-1, keepdims=True)
    var = jnp.mean(jnp.square(xf - mu), axis=-1, keepdims=True)
    return ((xf - mu) * lax.rsqrt(var + LN_EPS) * g.astype(F32) + b.astype(F32)).astype(x.dtype)


def _rms_norm(x, w):
    xf = x.astype(F32)
    return (xf * lax.rsqrt(jnp.mean(xf * xf, axis=-1, keepdims=True) + NORM_EPS) * w.astype(F32)).astype(x.dtype)


def _l2_normalize(x):
    xf = x.astype(F32)
    return (xf * lax.rsqrt(jnp.sum(xf * xf, axis=-1, keepdims=True) + NORM_EPS)).astype(x.dtype)


def _modulate(x, c, w, b):
    shift, scale, gate = jnp.split(jax.nn.silu(c) @ w + b, 3, axis=-1)
    return x * (1.0 + scale[:, None]) + shift[:, None], 1.0 + gate[:, None]


def _post_norm(x, h, gate, g, b):
    return _layer_norm(DEEPNORM_ALPHA * x + gate * h, g, b)


def _to_chunks(a, L):
    B, T = a.shape[:2]
    a = a.astype(F32).reshape((B, T // L, L) + a.shape[2:])
    return jnp.swapaxes(jnp.swapaxes(a, 0, 1), 2, 3)


def _from_chunks(o):
    o = jnp.swapaxes(jnp.swapaxes(o, 2, 3), 0, 1)
    return o.reshape((o.shape[0], o.shape[1] * o.shape[2]) + o.shape[3:])


def _sweep_query_blocks(fn, start, *per_query):
    B, T = per_query[0].shape[:2]
    lq = min(Q_BLOCK, T)
    n = T // lq
    pos = (start + jnp.arange(T)).reshape(n, lq)
    blocks = tuple(jnp.swapaxes(a.reshape((B, n, lq) + a.shape[2:]), 0, 1) for a in per_query)
    out = lax.map(lambda xs: fn(xs[0], *xs[1:]), (pos,) + blocks)
    return jnp.swapaxes(out, 0, 1).reshape((B, T) + out.shape[3:])


def _t5_bucket(rel):
    nb = N_BUCKETS // 2
    max_exact = nb // 2
    ret = jnp.where(rel > 0, nb, 0)
    n = jnp.abs(rel)
    large = max_exact + (jnp.log(jnp.maximum(n, 1).astype(F32) / max_exact)
                         / math.log(MAX_DISTANCE / max_exact) * (nb - max_exact)).astype(jnp.int32)
    large = jnp.minimum(large, nb - 1)
    return ret + jnp.where(n < max_exact, n, large)


def _rotary(x, pos):
    d = x.shape[-1]
    inv = ROPE_BASE ** (-jnp.arange(0, d, 2, dtype=F32) / d)
    ang = pos.astype(F32)[:, None] * inv[None, :]
    cos = jnp.cos(ang)[None, :, None, :]
    sin = jnp.sin(ang)[None, :, None, :]
    xf = x.astype(F32)
    x1, x2 = xf[..., 0::2], xf[..., 1::2]
    return jnp.stack([x1 * cos - x2 * sin, x1 * sin + x2 * cos], axis=-1).reshape(x.shape).astype(x.dtype)


def _gated_delta_chunked(q, k, v, g, beta, s0):
    B, T, H, DK = q.shape
    DV = v.shape[-1]
    L = min(CHUNK, T)
    qc, kc, vc, gc, bc = (_to_chunks(a, L) for a in (q, k, v, g, beta))
    G = jnp.cumsum(gc, axis=-1)
    idx = jnp.arange(L)
    strict = idx[:, None] > idx[None, :]
    incl = idx[:, None] >= idx[None, :]
    dG = G[..., :, None] - G[..., None, :]
    dec_strict = jnp.exp(jnp.where(strict, dG, NEG_INF))
    dec_incl = jnp.exp(jnp.where(incl, dG, NEG_INF))
    kb = kc * bc[..., None]
    a_mat = jnp.einsum('nbhid,nbhjd->nbhij', kb, kc) * dec_strict + jnp.eye(L, dtype=F32)
    rhs = jnp.concatenate([vc * bc[..., None], kb * jnp.exp(G)[..., None]], axis=-1)
    sol = lax.linalg.triangular_solve(a_mat, rhs, left_side=True, lower=True, unit_diagonal=True)
    u_new, w_dec = sol[..., :DV], sol[..., DV:]
    qk = jnp.einsum('nbhid,nbhjd->nbhij', qc, kc) * dec_incl
    q_dec = qc * jnp.exp(G)[..., None]
    k_dec = kc * jnp.exp(G[..., -1:] - G)[..., None]
    g_tot = jnp.exp(G[..., -1])

    def step(s, xs):
        u_c, w_c, qk_c, qd_c, kd_c, gt_c = xs
        v_res = u_c - jnp.einsum('bhld,bhdv->bhlv', w_c, s)
        o = jnp.einsum('bhld,bhdv->bhlv', qd_c, s) + jnp.einsum('bhij,bhjv->bhiv', qk_c, v_res)
        s = s * gt_c[..., None, None] + jnp.einsum('bhld,bhlv->bhdv', kd_c, v_res)
        return s, o

    s_fin, o = lax.scan(step, s0.astype(F32), (u_new, w_dec, qk, q_dec, k_dec, g_tot))
    return _from_chunks(o), s_fin


def _retention_chunked(q, k, v, s0):
    B, T, H, DK = q.shape
    L = min(CHUNK, T)
    log_gamma = jnp.log1p(-jnp.power(2.0, -5.0 - jnp.arange(H, dtype=F32)))
    idx = jnp.arange(L, dtype=F32)
    rel = idx[:, None] - idx[None, :]
    intra = jnp.where(rel >= 0, jnp.exp(log_gamma[:, None, None] * jnp.maximum(rel, 0.0)), 0.0)
    q_dec = jnp.exp(log_gamma[:, None] * (idx + 1.0))
    k_dec = jnp.exp(log_gamma[:, None] * (L - 1.0 - idx))
    c_dec = jnp.exp(log_gamma * L)
    qc, kc, vc = (_to_chunks(a, L) for a in (q, k, v))

    def step(s, xs):
        qi, ki, vi = xs
        att = jnp.einsum('bhid,bhjd->bhij', qi, ki) * intra
        o = jnp.einsum('bhij,bhjv->bhiv', att, vi) + jnp.einsum('bhld,bhdv->bhlv', qi, s) * q_dec[:, :, None]
        s = s * c_dec[:, None, None] + jnp.einsum('bhld,bhlv->bhdv', ki * k_dec[:, :, None], vi)
        return s, o

    s_fin, o = lax.scan(step, s0.astype(F32), (qc, kc, vc))
    return _from_chunks(o), s_fin


def _gdn_mixer(u, conv_buf, s0, w_in, conv_w, a_log, dt_bias, norm_w, w_out):
    B, T, D = u.shape
    proj = u @ w_in
    qkv, b_logit, a_logit, z = jnp.split(proj, [3 * D, 3 * D + H_G, 3 * D + 2 * H_G], axis=-1)
    conv_in = jnp.concatenate([conv_buf.astype(qkv.dtype), qkv], axis=1)
    y = conv_w[0] * conv_in[:, 0:T]
    for j in range(1, CONV_W):
        y = y + conv_w[j] * conv_in[:, j:j + T]
    q, k, v = jnp.split(jax.nn.silu(y), 3, axis=-1)
    q = _l2_normalize(q.reshape(B, T, H_G, DK_G)) * DK_G ** -0.5
    k = _l2_normalize(k.reshape(B, T, H_G, DK_G))
    beta = jax.nn.sigmoid(b_logit.astype(F32))
    g = -jnp.exp(a_log.astype(F32)) * jax.nn.softplus(a_logit.astype(F32) + dt_bias.astype(F32))
    o, s = _gated_delta_chunked(q, k, v.reshape(B, T, H_G, DV_G), g, beta, s0)
    o = _rms_norm(o, norm_w).astype(u.dtype) * jax.nn.silu(z.reshape(B, T, H_G, DV_G))
    return o.reshape(B, T, D) @ w_out, conv_in[:, T:], s.astype(u.dtype)


def _fox_mixer(u, past_k, past_v, past_logf, w_in, b_f, w_out):
    B, T, D = u.shape
    proj = u @ w_in
    q, k, v, z, f_logit = jnp.split(proj, [D, 2 * D, 3 * D, 4 * D], axis=-1)
    q = q.reshape(B, T, H_F, DH_F)
    k = k.reshape(B, T, H_F, DH_F)
    v = v.reshape(B, T, H_F, DH_F)
    logf = jax.nn.log_sigmoid(f_logit.astype(F32) + b_f.astype(F32))
    if past_k is None:
        start, k_all, v_all, logf_all = 0, k, v, logf
    else:
        start = past_k.shape[1]
        k_all = jnp.concatenate([past_k.astype(k.dtype), k], axis=1)
        v_all = jnp.concatenate([past_v.astype(v.dtype), v], axis=1)
        logf_all = jnp.concatenate([past_logf.astype(F32), logf], axis=1)
    cum = jnp.cumsum(logf_all, axis=1)
    cum_k = jnp.swapaxes(cum, 1, 2)
    pos_k = jnp.arange(k_all.shape[1])

    def block(pos_q, qb, cqb):
        logits = jnp.einsum('bqhd,bshd->bhqs', qb, k_all, preferred_element_type=F32) * DH_F ** -0.5
        logits = logits + jnp.swapaxes(cqb, 1, 2)[..., None] - cum_k[:, :, None, :]
        logits = jnp.where(pos_k[None, :] <= pos_q[:, None], logits, NEG_INF)
        p = jax.nn.softmax(logits, axis=-1).astype(v_all.dtype)
        return jnp.einsum('bhqs,bshd->bqhd', p, v_all)

    o = _sweep_query_blocks(block, start, q, cum[:, start:])
    o = o * jax.nn.silu(z.reshape(B, T, H_F, DH_F))
    return o.reshape(B, T, D) @ w_out, k, v, logf


def _diff_mixer(u, past_k, past_v, w_in, lam_q1, lam_k1, lam_q2, lam_k2, subln_w, rel_table, w_out):
    B, T, D = u.shape
    proj = u @ w_in
    q, k, v, z = jnp.split(proj, 4, axis=-1)
    q = q.reshape(B, T, H_D, 2, DH_D)
    k = k.reshape(B, T, H_D, 2, DH_D)
    v = v.reshape(B, T, H_D, 2 * DH_D)
    if past_k is None:
        start, k_all, v_all = 0, k, v
    else:
        start = past_k.shape[1]
        k_all = jnp.concatenate([past_k.astype(k.dtype), k], axis=1)
        v_all = jnp.concatenate([past_v.astype(v.dtype), v], axis=1)
    pos_k = jnp.arange(k_all.shape[1])
    lam = (jnp.exp(jnp.sum(lam_q1.astype(F32) * lam_k1.astype(F32)))
           - jnp.exp(jnp.sum(lam_q2.astype(F32) * lam_k2.astype(F32))) + LAMBDA_INIT)

    def block(pos_q, qb):
        bias = jnp.moveaxis(rel_table[_t5_bucket(pos_k[None, :] - pos_q[:, None])], -1, 0).astype(F32)
        logits = jnp.einsum('bqhmd,bshmd->bhmqs', qb, k_all, preferred_element_type=F32) * DH_D ** -0.5
        logits = logits + bias[None, :, None]
        mask = (pos_k[None, :] // CHUNK) <= (pos_q[:, None] // CHUNK)
        p = jax.nn.softmax(jnp.where(mask, logits, NEG_INF), axis=-1)
        w = (p[:, :, 0] - lam * p[:, :, 1]).astype(v_all.dtype)
        return jnp.einsum('bhqs,bshe->bqhe', w, v_all)

    o = _sweep_query_blocks(block, start, q)
    o = _rms_norm(o, subln_w) * (1.0 - LAMBDA_INIT)
    o = o * jax.nn.silu(z.reshape(B, T, H_D, 2 * DH_D))
    return o.reshape(B, T, D) @ w_out, k, v


def _retention_mixer(u, s0, start, w_in, gn_w, w_out):
    B, T, D = u.shape
    proj = u @ w_in
    q, k, v, z = jnp.split(proj, [D, 2 * D, 4 * D], axis=-1)
    pos = start + jnp.arange(T)
    q = _rotary(q.reshape(B, T, H_R, DK_R), pos) * DK_R ** -0.5
    k = _rotary(k.reshape(B, T, H_R, DK_R), pos)
    o, s = _retention_chunked(q, k, v.reshape(B, T, H_R, DV_R), s0)
    mu = jnp.mean(o, axis=-1, keepdims=True)
    var = jnp.mean(jnp.square(o - mu), axis=-1, keepdims=True)
    o = (o - mu) * lax.rsqrt(var + LN_EPS) * gn_w.astype(F32).reshape(H_R, DV_R)
    o = o.astype(u.dtype) * jax.nn.silu(z.reshape(B, T, H_R, DV_R))
    return o.reshape(B, T, H_R * DV_R) @ w_out, s.astype(u.dtype)


def setup_inputs(seed: int = 0) -> dict:
    key = jax.random.key(seed)
    ks = iter(jax.random.split(key, 64))
    D = D_MODEL
    fan = D ** -0.5

    def nrm(shape, scale):
        return jax.random.normal(next(ks), shape, F32) * scale

    def unif(shape, lo, hi):
        return jax.random.uniform(next(ks), shape, F32, lo, hi)

    dt = jnp.exp(unif((H_G,), math.log(1e-3), math.log(1e-1)))
    return {
        'x_prompt': nrm((BATCH, SEQ, D), 1.0),
        'x_sample': nrm((DEC_BATCH, DEC_SEQ, D), 1.0),
        'c_prompt': nrm((BATCH, D), 1.0),
        'c_sample': nrm((DEC_BATCH, D), 1.0),
        'state_gdn': nrm((DEC_BATCH, H_G, DK_G, DV_G), 0.1),
        'state_gdn_conv': nrm((DEC_BATCH, CONV_W - 1, 3 * D), 1.0),
        'cache_fox_k': nrm((DEC_BATCH, PAST_LEN, H_F, DH_F), 1.0),
        'cache_fox_v': nrm((DEC_BATCH, PAST_LEN, H_F, DH_F), 1.0),
        'cache_fox_logf': jax.nn.log_sigmoid(nrm((DEC_BATCH, PAST_LEN, H_F), 1.0) + 3.0),
        'cache_diff_k': nrm((DEC_BATCH, PAST_LEN, H_D, 2, DH_D), 1.0),
        'cache_diff_v': nrm((DEC_BATCH, PAST_LEN, H_D, 2 * DH_D), 1.0),
        'state_ret': nrm((DEC_BATCH, H_R, DK_R, DV_R), 1.0),
        'ada_w': nrm((DEPTH, D, 3 * D), 0.3 * fan),
        'ada_b': nrm((DEPTH, 3 * D), 0.01),
        'ln_g': 1.0 + nrm((DEPTH, D), 0.02),
        'ln_b': nrm((DEPTH, D), 0.02),
        'gdn_w_in': nrm((D, 4 * D + 2 * H_G), fan),
        'gdn_conv_w': nrm((CONV_W, 3 * D), CONV_W ** -0.5),
        'gdn_a_log': jnp.log(unif((H_G,), 1.0, 16.0)),
        'gdn_dt_bias': dt + jnp.log(-jnp.expm1(-dt)),
        'gdn_norm_w': 1.0 + nrm((DV_G,), 0.02),
        'gdn_w_out': nrm((D, D), fan * DEEPNORM_BETA),
        'fox_w_in': nrm((D, 4 * D + H_F), fan),
        'fox_b_f': unif((H_F,), 1.0, 4.0),
        'fox_w_out': nrm((D, D), fan * DEEPNORM_BETA),
        'rel_bias_table': nrm((N_BUCKETS, H_D), 0.5),
        'diff_w_in': nrm((D, 4 * D), fan),
        'diff_lam_q1': nrm((DH_D,), 0.1),
        'diff_lam_k1': nrm((DH_D,), 0.1),
        'diff_lam_q2': nrm((DH_D,), 0.1),
        'diff_lam_k2': nrm((DH_D,), 0.1),
        'diff_subln_w': 1.0 + nrm((2 * DH_D,), 0.02),
        'diff_w_out': nrm((D, D), fan * DEEPNORM_BETA),
        'ret_w_in': nrm((D, 6 * D), fan),
        'ret_gn_w': 1.0 + nrm((2 * D,), 0.02),
        'ret_w_out': nrm((2 * D, D), (2 * D) ** -0.5 * DEEPNORM_BETA),
    }


def reference(x_prompt, x_sample, c_prompt, c_sample, state_gdn, state_gdn_conv, cache_fox_k, cache_fox_v,
              cache_fox_logf, cache_diff_k, cache_diff_v, state_ret, ada_w, ada_b, ln_g, ln_b,
              gdn_w_in, gdn_conv_w, gdn_a_log, gdn_dt_bias, gdn_norm_w, gdn_w_out,
              fox_w_in, fox_b_f, fox_w_out, rel_bias_table,
              diff_w_in, diff_lam_q1, diff_lam_k1, diff_lam_q2, diff_lam_k2, diff_subln_w, diff_w_out,
              ret_w_in, ret_gn_w, ret_w_out):
    xp, xs = x_prompt, x_sample
    bp = x_prompt.shape[0]
    for i in range(DEPTH):
        kind = i % N_MIXERS
        up, gate_p = _modulate(xp, c_prompt, ada_w[i], ada_b[i])
        us, gate_s = _modulate(xs, c_sample, ada_w[i], ada_b[i])
        if kind == 0:
            zero_buf = jnp.zeros((bp, CONV_W - 1, 3 * D_MODEL), up.dtype)
            zero_s = jnp.zeros((bp, H_G, DK_G, DV_G), F32)
            hp, gdn_conv_p, gdn_state_p = _gdn_mixer(up, zero_buf, zero_s, gdn_w_in, gdn_conv_w, gdn_a_log,
                                                     gdn_dt_bias, gdn_norm_w, gdn_w_out)
            hs, gdn_conv_s, gdn_state_s = _gdn_mixer(us, state_gdn_conv, state_gdn, gdn_w_in, gdn_conv_w, gdn_a_log,
                                                     gdn_dt_bias, gdn_norm_w, gdn_w_out)
        elif kind == 1:
            hp, fox_k_p, fox_v_p, fox_logf_p = _fox_mixer(up, None, None, None, fox_w_in, fox_b_f, fox_w_out)
            hs, fox_k_s, fox_v_s, fox_logf_s = _fox_mixer(us, cache_fox_k, cache_fox_v, cache_fox_logf,
                                                          fox_w_in, fox_b_f, fox_w_out)
        elif kind == 2:
            hp, diff_k_p, diff_v_p = _diff_mixer(up, None, None, diff_w_in, diff_lam_q1, diff_lam_k1, diff_lam_q2,
                                                 diff_lam_k2, diff_subln_w, rel_bias_table, diff_w_out)
            hs, diff_k_s, diff_v_s = _diff_mixer(us, cache_diff_k, cache_diff_v, diff_w_in, diff_lam_q1, diff_lam_k1,
                                                 diff_lam_q2, diff_lam_k2, diff_subln_w, rel_bias_table, diff_w_out)
        else:
            zero_r = jnp.zeros((bp, H_R, DK_R, DV_R), F32)
            hp, ret_state_p = _retention_mixer(up, zero_r, 0, ret_w_in, ret_gn_w, ret_w_out)
            hs, ret_state_s = _retention_mixer(us, state_ret, PAST_LEN, ret_w_in, ret_gn_w, ret_w_out)
        xp = _post_norm(xp, hp, gate_p, ln_g[i], ln_b[i])
        xs = _post_norm(xs, hs, gate_s, ln_g[i], ln_b[i])
    return (xp, xs, gdn_state_p, gdn_conv_p, fox_k_p, fox_v_p, fox_logf_p, diff_k_p, diff_v_p, ret_state_p,
            gdn_state_s, gdn_conv_s, fox_k_s, fox_v_s, fox_logf_s, diff_k_s, diff_v_s, ret_state_s)
```

```python
import functools
import math

import numpy as np
import jax
import jax.numpy as jnp
from jax import lax
from jax.experimental import pallas as pl
from jax.experimental.pallas import tpu as pltpu

F32 = jnp.float32
BF16 = jnp.bfloat16

DEPTH = 4
GDN_CHUNK = 64
ATTN_CHUNK = 64
DEEPNORM_ALPHA = (2.0 * DEPTH) ** 0.25
LN_EPS = 1e-5
NORM_EPS = 1e-6
NEG_INF = -1e30
H_G, H_F, H_D, H_R = 8, 16, 8, 4
CONV_W = 4
DIFF_LAYER = 2
LAMBDA_INIT = 0.8 - 0.6 * math.exp(-0.3 * DIFF_LAYER)
N_BUCKETS = 32
MAX_DISTANCE = 128
ROPE_BASE = 10000.0

LANES = 128
SUBLANES = 8
VMEM_LIMIT = 56 * 1024 * 1024


def _cparams(*sem):
    return pltpu.CompilerParams(dimension_semantics=sem, vmem_limit_bytes=VMEM_LIMIT)


def _sigmoid(x):
    return 1.0 / (1.0 + jnp.exp(-x))


def _silu(x):
    return x * _sigmoid(x)


def _softplus(x):
    return jnp.maximum(x, 0.0) + jnp.log(1.0 + jnp.exp(-jnp.abs(x)))


def _dot(a, b):
    return jnp.dot(a, b, preferred_element_type=F32)


def _dot_nt(a, b):
    return lax.dot_general(a, b, (((1,), (1,)), ((), ())), preferred_element_type=F32)


def _dot_tn(a, b):
    return lax.dot_general(a, b, (((0,), (0,)), ((), ())), preferred_element_type=F32)


def _split3(x):
    x1 = x.astype(BF16)
    r1 = x - x1.astype(F32)
    x2 = r1.astype(BF16)
    x3 = (r1 - x2.astype(F32)).astype(BF16)
    return x1, x2, x3


def _dot_exact_l(m01, x):
    x1, x2, x3 = _split3(x)
    return _dot(m01, x1) + _dot(m01, x2) + _dot(m01, x3)


def _dot_exact_nt(m01, x):
    x1, x2, x3 = _split3(x)
    return _dot_nt(m01, x1) + _dot_nt(m01, x2) + _dot_nt(m01, x3)


def _iota(shape, dim):
    return lax.broadcasted_iota(jnp.int32, shape, dim)


def _div_pow2(x, n):
    assert n & (n - 1) == 0
    return jnp.right_shift(x, n.bit_length() - 1)


def _row_tile(t, pref):
    return pref if t % pref == 0 else t


def _mod_kernel(c_ref, w_ref, b_ref, o_ref):
    s = _silu(c_ref[...])
    w = w_ref[...]
    s1 = s.astype(BF16)
    s2 = (s - s1.astype(F32)).astype(BF16)
    w1 = w.astype(BF16)
    w2 = (w - w1.astype(F32)).astype(BF16)
    o_ref[...] = _dot(s1, w1) + _dot(s1, w2) + _dot(s2, w1) + b_ref[...]


def _modulation(c_all, ada_w, ada_b):
    nb, d = c_all.shape
    depth, _, n = ada_w.shape
    tn = 1024
    return pl.pallas_call(
        _mod_kernel,
        grid=(depth, n // tn),
        in_specs=[pl.BlockSpec((nb, d), lambda l, j: (0, 0)),
                  pl.BlockSpec((None, d, tn), lambda l, j: (l, 0, j)),
                  pl.BlockSpec((None, 1, tn), lambda l, j: (l, 0, j))],
        out_specs=pl.BlockSpec((None, nb, tn), lambda l, j: (l, 0, j)),
        out_shape=jax.ShapeDtypeStruct((depth, nb, n), F32),
        compiler_params=_cparams("arbitrary", "arbitrary"),
        name="adaln_modulation",
    )(c_all, ada_w, ada_b.reshape(depth, 1, n))


def _mod_specs(layer, boff, d, which):
    return [pl.BlockSpec((None, None, 1, d), lambda b, i, w=w: (layer, boff + b, 0, w)) for w in which]


def _modulated(x_ref, shift_ref, scale_ref):
    return (x_ref[...] * (1.0 + scale_ref[...]) + shift_ref[...]).astype(BF16)


def _const_spec(shape):
    return pl.BlockSpec(shape, lambda b, i: (0,) * len(shape))


def _rows_spec(tm, n):
    return pl.BlockSpec((None, tm, n), lambda b, i: (b, i, 0))


def _gdn_proj_kernel(x_ref, shift_ref, scale_ref, wqkv_ref, wba_ref, wz_ref, qkv_ref, ba_ref, z_ref):
    u = _modulated(x_ref, shift_ref, scale_ref)
    d = x_ref.shape[-1]
    for s in range(3):
        qkv_ref[:, s * d:(s + 1) * d] = _dot(u, wqkv_ref[:, s * d:(s + 1) * d])
    ba_ref[...] = _dot(u, wba_ref[...])
    z_ref[...] = _dot(u, wz_ref[...]).astype(BF16)


def _gdn_proj(x, mod4, layer, boff, w_in):
    b, t, d = x.shape
    tm = _row_tile(t, 256)
    wqkv = w_in[:, :3 * d].astype(BF16)
    wba = jnp.pad(w_in[:, 3 * d:3 * d + 2 * H_G], ((0, 0), (0, LANES - 2 * H_G))).astype(BF16)
    wz = w_in[:, 3 * d + 2 * H_G:].astype(BF16)
    return pl.pallas_call(
        _gdn_proj_kernel,
        grid=(b, t // tm),
        in_specs=[_rows_spec(tm, d)] + _mod_specs(layer, boff, d, (0, 1))
                 + [_const_spec((d, 3 * d)), _const_spec((d, LANES)), _const_spec((d, d))],
        out_specs=[_rows_spec(tm, 3 * d), _rows_spec(tm, LANES), _rows_spec(tm, d)],
        out_shape=[jax.ShapeDtypeStruct((b, t, 3 * d), F32), jax.ShapeDtypeStruct((b, t, LANES), F32),
                   jax.ShapeDtypeStruct((b, t, d), BF16)],
        compiler_params=_cparams("arbitrary", "arbitrary"),
        name="gdn_in_proj",
    )(x, mod4, mod4, wqkv, wba, wz)


def _unit_lower_inverse_minus_identity(a):
    n = a.shape[0]
    r = _iota((n, n), 0)
    c = _iota((n, n), 1)

    def mm(x, y):
        return _dot(x.astype(BF16), y.astype(BF16))

    base = 8
    d = jnp.where(_div_pow2(r, base) == _div_pow2(c, base), a, 0.0)
    d2 = mm(d, d)
    d4 = mm(d2, d2)
    nn = -d
    nn = nn + d2 + mm(nn, d2)
    nn = nn + d4 + mm(nn, d4)
    m = base
    while m < n:
        same_pair = _div_pow2(r, 2 * m) == _div_pow2(c, 2 * m)
        off = jnp.where(same_pair, jnp.where(_div_pow2(r, m) != _div_pow2(c, m), a, 0.0), 0.0)
        y = off + mm(nn, off)
        x = y + mm(y, nn)
        nn = nn - x
        m *= 2
    return nn


def _gdn_kernel(qkv_ref, ba_ref, z_ref, cbuf_ref, s0_ref, cw_ref, avec_ref, dtvec_ref, nw_ref,
                o_ref, s_ref, ext_ref, *, chunk):
    ti = pl.program_id(1)
    tb = qkv_ref.shape[0]
    d = z_ref.shape[-1]
    dk = d // H_G
    n_chunks = tb // chunk

    @pl.when(ti == 0)
    def _():
        ext_ref[0:SUBLANES, :] = cbuf_ref[...]
        s_ref[...] = s0_ref[...]

    ext_ref[SUBLANES:SUBLANES + tb, :] = qkv_ref[...]

    ri = _iota((chunk, chunk), 0)
    ci = _iota((chunk, chunk), 1)
    tri = (ri >= ci).astype(BF16)
    eye_l = (_iota((LANES, LANES), 0) == _iota((LANES, LANES), 1)).astype(BF16)
    incl = ri >= ci
    strict = ri > ci
    cw = cw_ref[...]
    neg_exp_a = -jnp.exp(avec_ref[...])
    dtv = dtvec_ref[...]
    nw = nw_ref[...]

    def chunk_body(cidx, carry):
        r0 = pl.multiple_of(cidx * chunk, chunk)
        ba = ba_ref[pl.ds(r0, chunk), :]
        beta_all = _sigmoid(ba)
        g_all = neg_exp_a * _softplus(ba + dtv)
        gcum = _dot_exact_l(tri, g_all)
        gcum_t = _dot_exact_nt(eye_l, gcum)
        for h in range(H_G):
            def conv_slab(c0):
                win = ext_ref[pl.ds(r0, chunk + SUBLANES), c0:c0 + dk]
                acc = None
                for j in range(CONV_W):
                    lo = SUBLANES - CONV_W + 1 + j
                    term = cw[j:j + 1, c0:c0 + dk] * win[lo:lo + chunk, :]
                    acc = term if acc is None else acc + term
                return _silu(acc)

            q = conv_slab(h * dk)
            k = conv_slab(d + h * dk)
            v = conv_slab(2 * d + h * dk)
            q = q * lax.rsqrt(jnp.sum(q * q, axis=-1, keepdims=True) + NORM_EPS) * (dk ** -0.5)
            k = k * lax.rsqrt(jnp.sum(k * k, axis=-1, keepdims=True) + NORM_EPS)
            beta = beta_all[:, h:h + 1]
            gcol = gcum[:, H_G + h:H_G + h + 1]
            grow = gcum_t[H_G + h:H_G + h + 1, :]
            dec_incl = jnp.exp(jnp.where(incl, gcol - grow, NEG_INF))
            dec_strict = jnp.where(strict, dec_incl, 0.0)
            kb = k * beta
            kbf = k.astype(BF16)
            a_mat = _dot_nt(kb.astype(BF16), kbf) * dec_strict
            qk = _dot_nt(q.astype(BF16), kbf) * dec_incl
            exp_g = jnp.exp(gcol)
            rhs = jnp.concatenate([v * beta, kb * exp_g], axis=1)
            nn = _unit_lower_inverse_minus_identity(a_mat)
            sol = rhs + _dot(nn.astype(BF16), rhs.astype(BF16))
            u_new = sol[:, :dk]
            w_dec = sol[:, dk:]
            s = s_ref[h]
            sb = s.astype(BF16)
            v_res = u_new - _dot(w_dec.astype(BF16), sb)
            o = _dot((q * exp_g).astype(BF16), sb) + _dot(qk.astype(BF16), v_res.astype(BF16))
            g_last = gcol[chunk - 1:chunk, :]
            k_dec = k * jnp.exp(g_last - gcol)
            s_ref[h] = s * jnp.exp(g_last) + _dot_tn(k_dec.astype(BF16), v_res.astype(BF16))
            on = o * lax.rsqrt(jnp.mean(o * o, axis=-1, keepdims=True) + NORM_EPS) * nw
            zz = z_ref[pl.ds(r0, chunk), h * dk:(h + 1) * dk].astype(F32)
            o_ref[pl.ds(r0, chunk), h * dk:(h + 1) * dk] = (on * _silu(zz)).astype(BF16)
        return carry

    lax.fori_loop(0, n_chunks, chunk_body, 0)
    ext_ref[0:SUBLANES, :] = ext_ref[tb:tb + SUBLANES, :]


def _gdn_mix(qkv, ba, z, conv_buf, s0, conv_w, a_log, dt_bias, norm_w):
    b, t, d3 = qkv.shape
    d = d3 // 3
    dk = d // H_G
    chunk = min(GDN_CHUNK, t)
    tb = _row_tile(t, 4 * chunk)
    assert t >= CONV_W - 1 and tb >= SUBLANES
    cbuf = jnp.pad(conv_buf.astype(F32), ((0, 0), (SUBLANES - (CONV_W - 1), 0), (0, 0)))
    cw = jnp.pad(conv_w.astype(F32), ((0, SUBLANES - CONV_W), (0, 0)))
    avec = jnp.pad(a_log.astype(F32), (H_G, LANES - 2 * H_G)).reshape(1, LANES)
    dtvec = jnp.pad(dt_bias.astype(F32), (H_G, LANES - 2 * H_G)).reshape(1, LANES)
    nw = norm_w.astype(F32).reshape(1, dk)
    state_spec = pl.BlockSpec((None, H_G, dk, dk), lambda bb, i: (bb, 0, 0, 0))
    return pl.pallas_call(
        functools.partial(_gdn_kernel, chunk=chunk),
        grid=(b, t // tb),
        in_specs=[_rows_spec(tb, d3), _rows_spec(tb, LANES), _rows_spec(tb, d),
                  pl.BlockSpec((None, SUBLANES, d3), lambda bb, i: (bb, 0, 0)), state_spec,
                  _const_spec((SUBLANES, d3)), _const_spec((1, LANES)), _const_spec((1, LANES)),
                  _const_spec((1, dk))],
        out_specs=[_rows_spec(tb, d), state_spec],
        out_shape=[jax.ShapeDtypeStruct((b, t, d), BF16), jax.ShapeDtypeStruct((b, H_G, dk, dk), F32)],
        scratch_shapes=[pltpu.VMEM((tb + SUBLANES, d3), F32)],
        compiler_params=_cparams("arbitrary", "arbitrary"),
        name="gdn_mixer",
    )(qkv, ba, z, cbuf, s0.astype(F32), cw, avec, dtvec, nw)


def _out_proj_kernel(o_ref, x_ref, gate_ref, w_ref, g_ref, b_ref, y_ref):
    h = _dot(o_ref[...], w_ref[...])
    y = DEEPNORM_ALPHA * x_ref[...] + (1.0 + gate_ref[...]) * h
    mu = jnp.mean(y, axis=-1, keepdims=True)
    yc = y - mu
    var = jnp.mean(yc * yc, axis=-1, keepdims=True)
    y_ref[...] = yc * lax.rsqrt(var + LN_EPS) * g_ref[...] + b_ref[...]


def _out_proj(o, x, mod4, layer, boff, w_out, ln_g, ln_b):
    b, t, d = x.shape
    kdim = o.shape[-1]
    tm = _row_tile(t, 512)
    return pl.pallas_call(
        _out_proj_kernel,
        grid=(b, t // tm),
        in_specs=[_rows_spec(tm, kdim), _rows_spec(tm, d)] + _mod_specs(layer, boff, d, (2,))
                 + [_const_spec((kdim, d)), _const_spec((1, d)), _const_spec((1, d))],
        out_specs=_rows_spec(tm, d),
        out_shape=jax.ShapeDtypeStruct((b, t, d), F32),
        compiler_params=_cparams("arbitrary", "arbitrary"),
        name="out_proj_postnorm",
    )(o, x, mod4, w_out.astype(BF16), ln_g.reshape(1, d), ln_b.reshape(1, d))


def _fox_proj_kernel(x_ref, shift_ref, scale_ref, w_ref, wf_ref, bf_ref,
                     q_ref, k32_ref, v32_ref, kb_ref, vb_ref, z_ref, logf_ref, *, q_scale):
    u = _modulated(x_ref, shift_ref, scale_ref)
    d = x_ref.shape[-1]
    q_ref[...] = (_dot(u, w_ref[:, 0:d]) * q_scale).astype(BF16)
    k = _dot(u, w_ref[:, d:2 * d])
    k32_ref[...] = k
    kb_ref[...] = k.astype(BF16)
    v = _dot(u, w_ref[:, 2 * d:3 * d])
    v32_ref[...] = v
    vb_ref[...] = v.astype(BF16)
    z_ref[...] = _dot(u, w_ref[:, 3 * d:4 * d]).astype(BF16)
    f = _dot(u, wf_ref[...])[:, :H_F] + bf_ref[...]
    logf_ref[...] = -_softplus(-f)


def _fox_proj(x, mod4, layer, boff, w_in, b_f):
    b, t, d = x.shape
    tm = _row_tile(t, 256)
    w = w_in[:, :4 * d].astype(BF16)
    wf = jnp.pad(w_in[:, 4 * d:], ((0, 0), (0, LANES - H_F))).astype(BF16)
    f32o = jax.ShapeDtypeStruct((b, t, d), F32)
    bf16o = jax.ShapeDtypeStruct((b, t, d), BF16)
    return pl.pallas_call(
        functools.partial(_fox_proj_kernel, q_scale=(d // H_F) ** -0.5),
        grid=(b, t // tm),
        in_specs=[_rows_spec(tm, d)] + _mod_specs(layer, boff, d, (0, 1))
                 + [_const_spec((d, 4 * d)), _const_spec((d, LANES)), _const_spec((1, H_F))],
        out_specs=[_rows_spec(tm, d)] * 6 + [_rows_spec(tm, H_F)],
        out_shape=[bf16o, f32o, f32o, bf16o, bf16o, bf16o, jax.ShapeDtypeStruct((b, t, H_F), F32)],
        compiler_params=_cparams("arbitrary", "arbitrary"),
        name="fox_in_proj",
    )(x, mod4, mod4, w, wf, b_f.astype(F32).reshape(1, H_F))


def _cumsum_kernel(x_ref, c0_ref, cn_ref, ct_ref, *, blk):
    s, h = x_ref.shape
    tri = (_iota((blk, blk), 0) >= _iota((blk, blk), 1)).astype(BF16)
    eye_h = (_iota((h, h), 0) == _iota((h, h), 1)).astype(BF16)
    carry = c0_ref[...]
    for i in range(s // blk):
        c = _dot_exact_l(tri, x_ref[i * blk:(i + 1) * blk, :]) + carry
        cn_ref[i * blk:(i + 1) * blk, :] = c
        ct_ref[:, i * blk:(i + 1) * blk] = _dot_exact_nt(eye_h, c)
        carry = c[blk - 1:blk, :]


def _cumsum_time(x, c0):
    b, s, h = x.shape
    blk = 256 if s % 256 == 0 else s
    return pl.pallas_call(
        functools.partial(_cumsum_kernel, blk=blk),
        grid=(b,),
        in_specs=[pl.BlockSpec((None, s, h), lambda bb: (bb, 0, 0)),
                  pl.BlockSpec((None, 1, h), lambda bb: (bb, 0, 0))],
        out_specs=[pl.BlockSpec((None, s, h), lambda bb: (bb, 0, 0)),
                   pl.BlockSpec((None, h, s), lambda bb: (bb, 0, 0))],
        out_shape=[jax.ShapeDtypeStruct((b, s, h), F32), jax.ShapeDtypeStruct((b, h, s), F32)],
        compiler_params=_cparams("arbitrary"),
        name="logf_cumsum",
    )(x, c0)


def _online_softmax_step(s, v, m_ref, l_ref, acc_ref):
    m_prev = m_ref[...]
    m_new = jnp.maximum(m_prev, jnp.max(s, axis=-1, keepdims=True))
    alpha = jnp.exp(m_prev - m_new)
    p = jnp.exp(s - m_new)
    l_ref[...] = alpha * l_ref[...] + jnp.sum(p, axis=-1, keepdims=True)
    acc_ref[...] = alpha * acc_ref[...] + _dot(p.astype(BF16), v)
    m_ref[...] = m_new


def _fox_attn_kernel(q_ref, k_ref, v_ref, z_ref, cq_ref, ckt_ref, o_ref, m_ref, l_ref, acc_ref, *, tk):
    hp = pl.program_id(1)
    qi = pl.program_id(2)
    tq = q_ref.shape[0]
    dh = LANES // 2
    lane = _iota((tq, LANES), 1)
    q = q_ref[...]
    cq_all = cq_ref[...]
    hlane = _iota(cq_all.shape, 1)
    n_full = qi * (tq // tk)
    row_pos = qi * tq + _iota((tq, tk), 0)
    col_in = _iota((tq, tk), 1)
    for hh in range(2):
        h = hp * 2 + hh
        qm = jnp.where(_div_pow2(lane, dh) == hh, q, jnp.zeros_like(q))
        cq = jnp.sum(jnp.where(hlane == h, cq_all, 0.0), axis=-1, keepdims=True)
        m_ref[...] = jnp.full(m_ref.shape, NEG_INF, F32)
        l_ref[...] = jnp.zeros(l_ref.shape, F32)
        acc_ref[hh] = jnp.zeros(acc_ref.shape[1:], F32)

        def step(j, masked, hh=hh, h=h, qm=qm, cq=cq):
            k0 = pl.multiple_of(j * tk, tk)
            s = _dot_nt(qm, k_ref[pl.ds(k0, tk), :])
            s = s + cq - ckt_ref[pl.ds(h, 1), pl.ds(k0, tk)]
            if masked:
                s = jnp.where(k0 + col_in <= row_pos, s, NEG_INF)
            _online_softmax_step(s, v_ref[pl.ds(k0, tk), :], m_ref, l_ref, acc_ref.at[hh])

        def body(j, carry):
            step(j, False)
            return carry

        lax.fori_loop(0, n_full, body, 0)
        for dd in range(tq // tk):
            step(n_full + dd, True)
        acc_ref[hh] = acc_ref[hh] / l_ref[...]
    o = jnp.where(lane < dh,acc_ref[0], acc_ref[1])
    o_ref[...] = (o * _silu(z_ref[...].astype(F32))).astype(BF16)


def _fox_attn_prompt(q, kb, vb, z, cum_n, cum_t):
    b, t, d = q.shape
    tq = _row_tile(t, 256)
    tk = tq
    hpairs = d // LANES
    kv_spec = pl.BlockSpec((None, t, LANES), lambda bb, hp, i: (bb, 0, hp))
    q_spec = pl.BlockSpec((None, tq, LANES), lambda bb, hp, i: (bb, i, hp))
    return pl.pallas_call(
        functools.partial(_fox_attn_kernel, tk=tk),
        grid=(b, hpairs, t // tq),
        in_specs=[q_spec, kv_spec, kv_spec, q_spec,
                  pl.BlockSpec((None, tq, H_F), lambda bb, hp, i: (bb, i, 0)),
                  pl.BlockSpec((None, H_F, t), lambda bb, hp, i: (bb, 0, 0))],
        out_specs=q_spec,
        out_shape=jax.ShapeDtypeStruct((b, t, d), BF16),
        scratch_shapes=[pltpu.VMEM((tq, 1), F32), pltpu.VMEM((tq, 1), F32), pltpu.VMEM((2, tq, LANES), F32)],
        compiler_params=_cparams("arbitrary", "arbitrary", "arbitrary"),
        name="fox_attention_prompt",
    )(q, kb, vb, z, cum_n, cum_t)


def _fox_decode_kernel(q_ref, kp_ref, vp_ref, kn_ref, vn_ref, z_ref, cq_ref, ckt_ref, o_ref):
    hp = pl.program_id(1)
    t = q_ref.shape[0]
    p_len = kp_ref.shape[0]
    dh = LANES // 2
    lane = _iota((t, LANES), 1)
    q = q_ref[...]
    kp = kp_ref[...].astype(BF16)
    vp = vp_ref[...].astype(BF16)
    kn = kn_ref[...]
    vn = vn_ref[...]
    cq_all = cq_ref[...]
    hlane = _iota(cq_all.shape, 1)
    causal = _iota((t, t), 1) <= _iota((t, t), 0)
    outs = []
    for hh in range(2):
        h = hp * 2 + hh
        qm = jnp.where(_div_pow2(lane, dh) == hh, q, jnp.zeros_like(q))
        cq = jnp.sum(jnp.where(hlane == h, cq_all, 0.0), axis=-1, keepdims=True)
        ck = ckt_ref[pl.ds(h, 1), :]
        s_p = _dot_nt(qm, kp) + cq - ck[:, :p_len]
        s_n = jnp.where(causal, _dot_nt(qm, kn) + cq - ck[:, p_len:], NEG_INF)
        m = jnp.maximum(jnp.max(s_p, axis=-1, keepdims=True), jnp.max(s_n, axis=-1, keepdims=True))
        e_p = jnp.exp(s_p - m)
        e_n = jnp.exp(s_n - m)
        den = jnp.sum(e_p, axis=-1, keepdims=True) + jnp.sum(e_n, axis=-1, keepdims=True)
        outs.append((_dot(e_p.astype(BF16), vp) + _dot(e_n.astype(BF16), vn)) / den)
    o = jnp.where(lane < dh,outs[0], outs[1])
    o_ref[...] = (o * _silu(z_ref[...].astype(F32))).astype(BF16)


def _fox_attn_sample(q, k_past, v_past, kb, vb, z, cq_new, cum_t):
    b, t, d = q.shape
    p_len = k_past.shape[1]
    hpairs = d // LANES
    new_spec = pl.BlockSpec((None, t, LANES), lambda bb, hp: (bb, 0, hp))
    past_spec = pl.BlockSpec((None, p_len, LANES), lambda bb, hp: (bb, 0, hp))
    return pl.pallas_call(
        _fox_decode_kernel,
        grid=(b, hpairs),
        in_specs=[new_spec, past_spec, past_spec, new_spec, new_spec, new_spec,
                  pl.BlockSpec((None, t, H_F), lambda bb, hp: (bb, 0, 0)),
                  pl.BlockSpec((None, H_F, p_len + t), lambda bb, hp: (bb, 0, 0))],
        out_specs=new_spec,
        out_shape=jax.ShapeDtypeStruct((b, t, d), BF16),
        compiler_params=_cparams("arbitrary", "arbitrary"),
        name="fox_attention_sample",
    )(q, k_past, v_past, kb, vb, z, cq_new, cum_t)


def _diff_proj_kernel(x_ref, shift_ref, scale_ref, w_ref, q_ref, k32_ref, v32_ref, kb_ref, vb_ref, z_ref,
                      *, q_scale):
    u = _modulated(x_ref, shift_ref, scale_ref)
    d = x_ref.shape[-1]
    q_ref[...] = (_dot(u, w_ref[:, 0:d]) * q_scale).astype(BF16)
    k = _dot(u, w_ref[:, d:2 * d])
    k32_ref[...] = k
    kb_ref[...] = k.astype(BF16)
    v = _dot(u, w_ref[:, 2 * d:3 * d])
    v32_ref[...] = v
    vb_ref[...] = v.astype(BF16)
    z_ref[...] = _dot(u, w_ref[:, 3 * d:4 * d]).astype(BF16)


def _diff_proj(x, mod4, layer, boff, w_in):
    b, t, d = x.shape
    tm = _row_tile(t, 256)
    f32o = jax.ShapeDtypeStruct((b, t, d), F32)
    bf16o = jax.ShapeDtypeStruct((b, t, d), BF16)
    return pl.pallas_call(
        functools.partial(_diff_proj_kernel, q_scale=(d // (2 * H_D)) ** -0.5),
        grid=(b, t // tm),
        in_specs=[_rows_spec(tm, d)] + _mod_specs(layer, boff, d, (0, 1)) + [_const_spec((d, 4 * d))],
        out_specs=[_rows_spec(tm, d)] * 6,
        out_shape=[bf16o, f32o, f32o, bf16o, bf16o, bf16o],
        compiler_params=_cparams("arbitrary", "arbitrary"),
        name="diff_in_proj",
    )(x, mod4, mod4, w_in.astype(BF16))


def _t5_thresholds():
    nb = N_BUCKETS // 2
    max_exact = nb // 2
    steps = nb - max_exact
    ratio = MAX_DISTANCE // max_exact
    out = []
    for kk in range(1, nb - max_exact):
        target = max_exact ** steps * ratio ** kk
        n = max_exact
        while n ** steps < target:
            n += 1
        out.append(n)
    return nb, max_exact, out


def _bias_kernel(tbl_ref, o_ref, *, q0, k0):
    h = pl.program_id(0)
    nq, nk = o_ref.shape
    rel = (k0 + _iota((nq, nk), 1)) - (q0 + _iota((nq, nk), 0))
    nb, max_exact, thr = _t5_thresholds()
    n = jnp.abs(rel)
    large = jnp.full((nq, nk), max_exact, jnp.int32)
    for tval in thr:
        large = large + (n >= tval).astype(jnp.int32)
    bucket = jnp.where(rel > 0, nb, 0) + jnp.where(n < max_exact, n, large)
    acc = jnp.zeros((nq, nk), F32)
    for bkt in range(N_BUCKETS):
        acc = jnp.where(bucket == bkt, tbl_ref[bkt * H_D + h], acc)
    o_ref[...] = acc


def _bias_tile(rel_table, q0, nq, k0, nk):
    return pl.pallas_call(
        functools.partial(_bias_kernel, q0=q0, k0=k0),
        grid=(H_D,),
        in_specs=[pl.BlockSpec(memory_space=pltpu.SMEM)],
        out_specs=pl.BlockSpec((None, nq, nk), lambda h: (h, 0, 0)),
        out_shape=jax.ShapeDtypeStruct((H_D, nq, nk), F32),
        compiler_params=_cparams("arbitrary"),
        name="t5_bias_tile",
    )(rel_table.astype(F32).reshape(N_BUCKETS * H_D))


def _diff_lambda(lam_ref):
    lam = lam_ref[...]
    s1 = jnp.sum(lam[0:1, :] * lam[1:2, :], axis=-1, keepdims=True)
    s2 = jnp.sum(lam[2:3, :] * lam[3:4, :], axis=-1, keepdims=True)
    return jnp.exp(s1) - jnp.exp(s2) + LAMBDA_INIT


def _diff_epilogue(o, z_ref, subln_ref, o_ref):
    on = o * lax.rsqrt(jnp.mean(o * o, axis=-1, keepdims=True) + NORM_EPS) * subln_ref[...]
    on = on * (1.0 - LAMBDA_INIT)
    o_ref[...] = (on * _silu(z_ref[...].astype(F32))).astype(BF16)


def _diff_attn_kernel(tbl_ref, q_ref, k_ref, v_ref, z_ref, bias0_ref, bias1_ref, lam_ref, subln_ref,
                      o_ref, m_ref, l_ref, acc_ref):
    h = pl.program_id(1)
    qi = pl.program_id(2)
    tq = q_ref.shape[0]
    tk = tq
    dh = LANES // 2
    lane = _iota((tq, LANES), 1)
    q = q_ref[...]
    nb, max_exact, _ = _t5_thresholds()
    far_bias = tbl_ref[(nb - 1) * H_D + h]
    row_chunk = _div_pow2(_iota((tq, tk), 0), ATTN_CHUNK)
    col_chunk = _div_pow2(_iota((tq, tk), 1), ATTN_CHUNK)
    for br in range(2):
        qm = jnp.where(_div_pow2(lane, dh) == br, q, jnp.zeros_like(q))
        m_ref[...] = jnp.full(m_ref.shape, NEG_INF, F32)
        l_ref[...] = jnp.zeros(l_ref.shape, F32)
        acc_ref[br] = jnp.zeros(acc_ref.shape[1:], F32)

        def step(j, bias, masked, br=br, qm=qm):
            k0 = pl.multiple_of(j * tk, tk)
            s = _dot_nt(qm, k_ref[pl.ds(k0, tk), :]) + bias
            if masked:
                s = jnp.where(col_chunk <= row_chunk, s, NEG_INF)
            _online_softmax_step(s, v_ref[pl.ds(k0, tk), :], m_ref, l_ref, acc_ref.at[br])

        def body(j, carry):
            step(j, far_bias, False)
            return carry

        lax.fori_loop(0, jnp.maximum(qi - 1, 0), body, 0)

        @pl.when(qi >= 1)
        def _():
            step(qi - 1, bias1_ref[...], False)

        step(qi, bias0_ref[...], True)
        acc_ref[br] = acc_ref[br] / l_ref[...]
    o = acc_ref[0] - _diff_lambda(lam_ref) * acc_ref[1]
    _diff_epilogue(o, z_ref, subln_ref, o_ref)


def _lam_pack(lam_q1, lam_k1, lam_q2, lam_k2):
    rows = jnp.stack([lam_q1, lam_k1, lam_q2, lam_k2]).astype(F32)
    return jnp.pad(rows, ((0, SUBLANES - 4), (0, LANES - rows.shape[1])))


def _diff_attn_prompt(q, kb, vb, z, rel_table, lam, subln_w):
    b, t, d = q.shape
    tq = _row_tile(t, 256)
    assert tq % ATTN_CHUNK == 0 and tq >= MAX_DISTANCE
    bias0 = _bias_tile(rel_table, 0, tq, 0, tq)
    bias1 = _bias_tile(rel_table, tq, tq, 0, tq)
    kv_spec = pl.BlockSpec((None, t, LANES), lambda bb, h, i: (bb, 0, h))
    q_spec = pl.BlockSpec((None, tq, LANES), lambda bb, h, i: (bb, i, h))
    bias_spec = pl.BlockSpec((None, tq, tq), lambda bb, h, i: (h, 0, 0))
    return pl.pallas_call(
        _diff_attn_kernel,
        grid=(b, H_D, t // tq),
        in_specs=[pl.BlockSpec(memory_space=pltpu.SMEM), q_spec, kv_spec, kv_spec, q_spec, bias_spec, bias_spec,
                  pl.BlockSpec((SUBLANES, LANES), lambda bb, h, i: (0, 0)),
                  pl.BlockSpec((1, LANES), lambda bb, h, i: (0, 0))],
        out_specs=q_spec,
        out_shape=jax.ShapeDtypeStruct((b, t, d), BF16),
        scratch_shapes=[pltpu.VMEM((tq, 1), F32), pltpu.VMEM((tq, 1), F32), pltpu.VMEM((2, tq, LANES), F32)],
        compiler_params=_cparams("arbitrary", "arbitrary", "arbitrary"),
        name="diff_attention_prompt",
    )(rel_table.astype(F32).reshape(N_BUCKETS * H_D), q, kb, vb, z, bias0, bias1, lam,
      subln_w.astype(F32).reshape(1, LANES))


def _diff_decode_kernel(q_ref, kp_ref, vp_ref, kn_ref, vn_ref, z_ref, bias_ref, lam_ref, subln_ref, o_ref,
                        *, p_len):
    t = q_ref.shape[0]
    dh = LANES // 2
    lane = _iota((t, LANES), 1)
    q = q_ref[...]
    kp = kp_ref[...].astype(BF16)
    vp = vp_ref[...].astype(BF16)
    kn = kn_ref[...]
    vn = vn_ref[...]
    bias = bias_ref[...]
    q_chunk = _div_pow2(p_len + _iota((t, t), 0), ATTN_CHUNK)
    kn_chunk = _div_pow2(p_len + _iota((t, t), 1), ATTN_CHUNK)
    kp_chunk = _div_pow2(_iota((t, p_len), 1), ATTN_CHUNK)
    qp_chunk = _div_pow2(p_len + _iota((t, p_len), 0), ATTN_CHUNK)
    outs = []
    for br in range(2):
        qm = jnp.where(_div_pow2(lane, dh) == br, q, jnp.zeros_like(q))
        s_p = jnp.where(kp_chunk <= qp_chunk, _dot_nt(qm, kp) + bias[:, :p_len], NEG_INF)
        s_n = jnp.where(kn_chunk <= q_chunk, _dot_nt(qm, kn) + bias[:, p_len:], NEG_INF)
        m = jnp.maximum(jnp.max(s_p, axis=-1, keepdims=True), jnp.max(s_n, axis=-1, keepdims=True))
        e_p = jnp.exp(s_p - m)
        e_n = jnp.exp(s_n - m)
        den = jnp.sum(e_p, axis=-1, keepdims=True) + jnp.sum(e_n, axis=-1, keepdims=True)
        outs.append((_dot(e_p.astype(BF16), vp) + _dot(e_n.astype(BF16), vn)) / den)
    o = outs[0] - _diff_lambda(lam_ref) * outs[1]
    _diff_epilogue(o, z_ref, subln_ref, o_ref)


def _diff_attn_sample(q, k_past, v_past, kb, vb, z, rel_table, lam, subln_w):
    b, t, d = q.shape
    p_len = k_past.shape[1]
    bias = _bias_tile(rel_table, p_len, t, 0, p_len + t)
    new_spec = pl.BlockSpec((None, t, LANES), lambda bb, h: (bb, 0, h))
    past_spec = pl.BlockSpec((None, p_len, LANES), lambda bb, h: (bb, 0, h))
    return pl.pallas_call(
        functools.partial(_diff_decode_kernel, p_len=p_len),
        grid=(b, H_D),
        in_specs=[new_spec, past_spec, past_spec, new_spec, new_spec, new_spec,
                  pl.BlockSpec((None, t, p_len + t), lambda bb, h: (h, 0, 0)),
                  pl.BlockSpec((SUBLANES, LANES), lambda bb, h: (0, 0)),
                  pl.BlockSpec((1, LANES), lambda bb, h: (0, 0))],
        out_specs=new_spec,
        out_shape=jax.ShapeDtypeStruct((b, t, d), BF16),
        compiler_params=_cparams("arbitrary", "arbitrary"),
        name="diff_attention_sample",
    )(q, k_past, v_past, kb, vb, z, bias, lam, subln_w.astype(F32).reshape(1, LANES))


def _rope_kernel(inv_ref, cos_ref, sin_ref, *, start):
    t, w = cos_ref.shape
    pos = (start + pl.program_id(0) * t + _iota((t, w), 0)).astype(F32)
    ang = pos * inv_ref[...]
    even = (_iota((t, w), 1) & 1) == 0
    cos_ref[...] = jnp.cos(ang)
    sn = jnp.sin(ang)
    sin_ref[...] = jnp.where(even, -sn, sn)


def _rope_tables(t, start, dk):
    inv_half = np.power(np.float32(ROPE_BASE), -np.arange(0, dk, 2, dtype=np.float32) / np.float32(dk))
    inv = jnp.asarray(np.repeat(inv_half.astype(np.float32), 2).reshape(1, dk))
    tt = _row_tile(t, 512)
    return pl.pallas_call(
        functools.partial(_rope_kernel, start=start),
        grid=(t // tt,),
        in_specs=[pl.BlockSpec((1, dk), lambda i: (0, 0))],
        out_specs=[pl.BlockSpec((tt, dk), lambda i: (i, 0))] * 2,
        out_shape=[jax.ShapeDtypeStruct((t, dk), F32)] * 2,
        compiler_params=_cparams("arbitrary"),
        name="rope_tables",
    )(inv)


def _rotate_pairs(x, cos, sin_signed):
    slabs = []
    for c0 in range(0, x.shape[-1], LANES):
        xs = x[:, c0:c0 + LANES]
        even = (_iota(xs.shape, 1) & 1) == 0
        slabs.append(jnp.where(even, pltpu.roll(xs, LANES - 1, 1), pltpu.roll(xs, 1, 1)))
    return x * cos + jnp.concatenate(slabs, axis=1) * sin_signed


def _ret_proj_kernel(x_ref, shift_ref, scale_ref, w_ref, cos_ref, sin_ref, q_ref, k_ref, v_ref, z_ref,
                     *, q_scale):
    u = _modulated(x_ref, shift_ref, scale_ref)
    d = x_ref.shape[-1]
    dk = cos_ref.shape[-1]
    cos = cos_ref[...]
    sn = sin_ref[...]
    for h in range(d // dk):
        qh = _dot(u, w_ref[:, h * dk:(h + 1) * dk])
        q_ref[:, h * dk:(h + 1) * dk] = (_rotate_pairs(qh, cos, sn) * q_scale).astype(BF16)
        kh = _dot(u, w_ref[:, d + h * dk:d + (h + 1) * dk])
        k_ref[:, h * dk:(h + 1) * dk] = _rotate_pairs(kh, cos, sn).astype(BF16)
    for s in range(2):
        v_ref[:, s * d:(s + 1) * d] = _dot(u, w_ref[:, (2 + s) * d:(3 + s) * d]).astype(BF16)
        z_ref[:, s * d:(s + 1) * d] = _dot(u, w_ref[:, (4 + s) * d:(5 + s) * d]).astype(BF16)


def _ret_proj(x, mod4, layer, boff, w_in, cos, sin_signed):
    b, t, d = x.shape
    dk = d // H_R
    tm = _row_tile(t, 256)
    tab_spec = pl.BlockSpec((tm, dk), lambda bb, i: (i, 0))
    return pl.pallas_call(
        functools.partial(_ret_proj_kernel, q_scale=dk ** -0.5),
        grid=(b, t // tm),
        in_specs=[_rows_spec(tm, d)] + _mod_specs(layer, boff, d, (0, 1)) + [_const_spec((d, 6 * d)), tab_spec, tab_spec],
        out_specs=[_rows_spec(tm, d), _rows_spec(tm, d), _rows_spec(tm, 2 * d), _rows_spec(tm, 2 * d)],
        out_shape=[jax.ShapeDtypeStruct((b, t, d), BF16), jax.ShapeDtypeStruct((b, t, d), BF16),
                   jax.ShapeDtypeStruct((b, t, 2 * d), BF16), jax.ShapeDtypeStruct((b, t, 2 * d), BF16)],
        compiler_params=_cparams("arbitrary", "arbitrary"),
        name="ret_in_proj",
    )(x, mod4, mod4, w_in.astype(BF16), cos, sin_signed)


def _ret_kernel(q_ref, k_ref, v_ref, z_ref, s0_ref, gn_ref, o_ref, s_ref):
    ti = pl.program_id(1)
    lr = q_ref.shape[0]
    dk = q_ref.shape[-1] // H_R
    dv = v_ref.shape[-1] // H_R

    @pl.when(ti == 0)
    def _():
        s_ref[...] = s0_ref[...]

    rel = (_iota((lr, lr), 0) - _iota((lr, lr), 1)).astype(F32)
    idx = _iota((lr, 1), 0).astype(F32)
    for h in range(H_R):
        log_gamma = math.log1p(-(2.0 ** (-5.0 - h)))
        intra = jnp.where(rel >= 0, jnp.exp(log_gamma * jnp.maximum(rel, 0.0)), 0.0)
        q_dec = jnp.exp(log_gamma * (idx + 1.0))
        k_dec = jnp.exp(log_gamma * (lr - 1.0 - idx))
        c_dec = math.exp(log_gamma * lr)
        qh = q_ref[:, h * dk:(h + 1) * dk]
        kh = k_ref[:, h * dk:(h + 1) * dk]
        vh = v_ref[:, h * dv:(h + 1) * dv]
        s = s_ref[h]
        att = _dot_nt(qh, kh) * intra
        o = _dot(att.astype(BF16), vh) + _dot(qh, s.astype(BF16)) * q_dec
        s_ref[h] = s * c_dec + _dot_tn((kh.astype(F32) * k_dec).astype(BF16), vh)
        mu = jnp.mean(o, axis=-1, keepdims=True)
        oc = o - mu
        var = jnp.mean(oc * oc, axis=-1, keepdims=True)
        on = oc * lax.rsqrt(var + LN_EPS) * gn_ref[:, h * dv:(h + 1) * dv]
        zz = z_ref[:, h * dv:(h + 1) * dv].astype(F32)
        o_ref[:, h * dv:(h + 1) * dv] = (on * _silu(zz)).astype(BF16)


def _ret_mix(q, k, v, z, s0, gn_w):
    b, t, d = q.shape
    dk = d // H_R
    dv = v.shape[-1] // H_R
    lr = _row_tile(t, 256)
    state_spec = pl.BlockSpec((None, H_R, dk, dv), lambda bb, i: (bb, 0, 0, 0))
    return pl.pallas_call(
        _ret_kernel,
        grid=(b, t // lr),
        in_specs=[_rows_spec(lr, d), _rows_spec(lr, d), _rows_spec(lr, 2 * d), _rows_spec(lr, 2 * d), state_spec,
                  _const_spec((1, 2 * d))],
        out_specs=[_rows_spec(lr, 2 * d), state_spec],
        out_shape=[jax.ShapeDtypeStruct((b, t, 2 * d), BF16), jax.ShapeDtypeStruct((b, H_R, dk, dv), F32)],
        compiler_params=_cparams("arbitrary", "arbitrary"),
        name="retention_mixer",
    )(q, k, v, z, s0.astype(F32), gn_w.astype(F32).reshape(1, 2 * d))


def _run_group(x, mod4, boff, state_gdn, state_gdn_conv, cache_fox_k, cache_fox_v, cache_fox_logf,
               cache_diff_k, cache_diff_v, state_ret, start, p):
    b, t, d = x.shape
    dk_g = d // H_G

    qkv, ba, z = _gdn_proj(x, mod4, 0, boff, p["gdn_w_in"])
    if state_gdn is None:
        state_gdn = jnp.zeros((b, H_G, dk_g, dk_g), F32)
        state_gdn_conv = jnp.zeros((b, CONV_W - 1, 3 * d), F32)
    o, gdn_state = _gdn_mix(qkv, ba, z, state_gdn_conv, state_gdn, p["gdn_conv_w"], p["gdn_a_log"],
                            p["gdn_dt_bias"], p["gdn_norm_w"])
    gdn_conv = qkv[:, t - (CONV_W - 1):, :]
    x = _out_proj(o, x, mod4, 0, boff, p["gdn_w_out"], p["ln_g"][0], p["ln_b"][0])

    q, k32, v32, kb, vb, z, logf = _fox_proj(x, mod4, 1, boff, p["fox_w_in"], p["fox_b_f"])
    zero_c = jnp.zeros((b, 1, H_F), F32)
    if cache_fox_k is None:
        cum_n, cum_t = _cumsum_time(logf, zero_c)
        o = _fox_attn_prompt(q, kb, vb, z, cum_n, cum_t)
    else:
        p_len = cache_fox_k.shape[1]
        cum_pn, cum_pt = _cumsum_time(cache_fox_logf.astype(F32), zero_c)
        cum_n, cum_nt = _cumsum_time(logf, cum_pn[:, p_len - 1:, :])
        o = _fox_attn_sample(q, cache_fox_k.reshape(b, p_len, d), cache_fox_v.reshape(b, p_len, d), kb, vb, z,
                             cum_n, jnp.concatenate([cum_pt, cum_nt], axis=2))
    fox_k = k32.reshape(b, t, H_F, d // H_F)
    fox_v = v32.reshape(b, t, H_F, d // H_F)
    x = _out_proj(o, x, mod4, 1, boff, p["fox_w_out"], p["ln_g"][1], p["ln_b"][1])

    q, k32, v32, kb, vb, z = _diff_proj(x, mod4, 2, boff, p["diff_w_in"])
    lam = _lam_pack(p["diff_lam_q1"], p["diff_lam_k1"], p["diff_lam_q2"], p["diff_lam_k2"])
    if cache_diff_k is None:
        o = _diff_attn_prompt(q, kb, vb, z, p["rel_bias_table"], lam, p["diff_subln_w"])
    else:
        p_len = cache_diff_k.shape[1]
        o = _diff_attn_sample(q, cache_diff_k.reshape(b, p_len, d), cache_diff_v.reshape(b, p_len, d), kb, vb, z,
                              p["rel_bias_table"], lam, p["diff_subln_w"])
    diff_k = k32.reshape(b, t, H_D, 2, d // (2 * H_D))
    diff_v = v32.reshape(b, t, H_D, d // H_D)
    x = _out_proj(o, x, mod4, 2, boff, p["diff_w_out"], p["ln_g"][2], p["ln_b"][2])

    dk_r = d // H_R
    cos, sin_signed = _rope_tables(t, start, dk_r)
    q, k, v, z = _ret_proj(x, mod4, 3, boff, p["ret_w_in"], cos, sin_signed)
    if state_ret is None:
        state_ret = jnp.zeros((b, H_R, dk_r, 2 * d // H_R), F32)
    o, ret_state = _ret_mix(q, k, v, z, state_ret, p["ret_gn_w"])
    x = _out_proj(o, x, mod4, 3, boff, p["ret_w_out"], p["ln_g"][3], p["ln_b"][3])

    return x, gdn_state, gdn_conv, fox_k, fox_v, logf, diff_k, diff_v, ret_state


def kernel(x_prompt, x_sample, c_prompt, c_sample, state_gdn, state_gdn_conv, cache_fox_k, cache_fox_v, cache_fox_logf, cache_diff_k, cache_diff_v, state_ret, ada_w, ada_b, ln_g, ln_b, gdn_w_in, gdn_conv_w, gdn_a_log, gdn_dt_bias, gdn_norm_w, gdn_w_out, fox_w_in, fox_b_f, fox_w_out, rel_bias_table, diff_w_in, diff_lam_q1, diff_lam_k1, diff_lam_q2, diff_lam_k2, diff_subln_w, diff_w_out, ret_w_in, ret_gn_w, ret_w_out):
    p = dict(ln_g=ln_g, ln_b=ln_b, gdn_w_in=gdn_w_in, gdn_conv_w=gdn_conv_w, gdn_a_log=gdn_a_log,
             gdn_dt_bias=gdn_dt_bias, gdn_norm_w=gdn_norm_w, gdn_w_out=gdn_w_out, fox_w_in=fox_w_in,
             fox_b_f=fox_b_f, fox_w_out=fox_w_out, rel_bias_table=rel_bias_table, diff_w_in=diff_w_in,
             diff_lam_q1=diff_lam_q1, diff_lam_k1=diff_lam_k1, diff_lam_q2=diff_lam_q2, diff_lam_k2=diff_lam_k2,
             diff_subln_w=diff_subln_w, diff_w_out=diff_w_out, ret_w_in=ret_w_in, ret_gn_w=ret_gn_w,
             ret_w_out=ret_w_out)
    bp = x_prompt.shape[0]
    d = x_prompt.shape[-1]
    mod = _modulation(jnp.concatenate([c_prompt, c_sample], axis=0), ada_w, ada_b)
    mod4 = mod.reshape(mod.shape[0], mod.shape[1], 1, 3 * d)
    outs_p = _run_group(x_prompt, mod4, 0, None, None, None, None, None, None, None, None, 0, p)
    outs_s = _run_group(x_sample, mod4, bp, state_gdn, state_gdn_conv, cache_fox_k, cache_fox_v, cache_fox_logf,
                        cache_diff_k, cache_diff_v, state_ret, cache_fox_k.shape[1], p)
    return (outs_p[0], outs_s[0]) + tuple(outs_p[1:]) + tuple(outs_s[1:])
```

```python
import functools
import math

import numpy as np
import jax
import jax.numpy as jnp
from jax import lax
from jax.experimental import pallas as pl
from jax.experimental.pallas import tpu as pltpu

F32 = jnp.float32
BF16 = jnp.bfloat16

DEPTH = 4
ATTN_TQ = 512
ATTN_TK = 256
GDN_CHUNK = 64
ATTN_CHUNK = 64
DEEPNORM_ALPHA = (2.0 * DEPTH) ** 0.25
LN_EPS = 1e-5
NORM_EPS = 1e-6
NEG_INF = -1e30
H_G, H_F, H_D, H_R = 8, 16, 8, 4
CONV_W = 4
DIFF_LAYER = 2
LAMBDA_INIT = 0.8 - 0.6 * math.exp(-0.3 * DIFF_LAYER)
N_BUCKETS = 32
MAX_DISTANCE = 128
ROPE_BASE = 10000.0

LANES = 128
SUBLANES = 8
VMEM_LIMIT = 56 * 1024 * 1024


def _cparams(*sem):
    return pltpu.CompilerParams(dimension_semantics=sem, vmem_limit_bytes=VMEM_LIMIT)


def _sigmoid(x):
    return 1.0 / (1.0 + jnp.exp(-x))


def _silu(x):
    return x * _sigmoid(x)


def _softplus(x):
    return jnp.maximum(x, 0.0) + jnp.log(1.0 + jnp.exp(-jnp.abs(x)))


def _dot(a, b):
    return jnp.dot(a, b, preferred_element_type=F32)


def _dot_nt(a, b):
    return lax.dot_general(a, b, (((1,), (1,)), ((), ())), preferred_element_type=F32)


def _dot_tn(a, b):
    return lax.dot_general(a, b, (((0,), (0,)), ((), ())), preferred_element_type=F32)


def _split3(x):
    x1 = x.astype(BF16)
    r1 = x - x1.astype(F32)
    x2 = r1.astype(BF16)
    x3 = (r1 - x2.astype(F32)).astype(BF16)
    return x1, x2, x3


def _dot_exact_l(m01, x):
    x1, x2, x3 = _split3(x)
    return _dot(m01, x1) + _dot(m01, x2) + _dot(m01, x3)


def _dot_exact_nt(m01, x):
    x1, x2, x3 = _split3(x)
    return _dot_nt(m01, x1) + _dot_nt(m01, x2) + _dot_nt(m01, x3)


def _iota(shape, dim):
    return lax.broadcasted_iota(jnp.int32, shape, dim)


def _div_pow2(x, n):
    assert n & (n - 1) == 0
    return jnp.right_shift(x, n.bit_length() - 1)


def _row_tile(t, pref):
    return pref if t % pref == 0 else t


def _mod_kernel(c_ref, w_ref, b_ref, o_ref):
    s = _silu(c_ref[...])
    w = w_ref[...]
    s1 = s.astype(BF16)
    s2 = (s - s1.astype(F32)).astype(BF16)
    w1 = w.astype(BF16)
    w2 = (w - w1.astype(F32)).astype(BF16)
    o_ref[...] = _dot(s1, w1) + _dot(s1, w2) + _dot(s2, w1) + b_ref[...]


def _modulation(c_all, ada_w, ada_b):
    nb, d = c_all.shape
    depth, _, n = ada_w.shape
    tn = 1024
    return pl.pallas_call(
        _mod_kernel,
        grid=(depth, n // tn),
        in_specs=[pl.BlockSpec((nb, d), lambda l, j: (0, 0)),
                  pl.BlockSpec((None, d, tn), lambda l, j: (l, 0, j)),
                  pl.BlockSpec((None, 1, tn), lambda l, j: (l, 0, j))],
        out_specs=pl.BlockSpec((None, nb, tn), lambda l, j: (l, 0, j)),
        out_shape=jax.ShapeDtypeStruct((depth, nb, n), F32),
        compiler_params=_cparams("arbitrary", "arbitrary"),
        name="adaln_modulation",
    )(c_all, ada_w, ada_b.reshape(depth, 1, n))


def _mod_specs(layer, boff, d, which):
    return [pl.BlockSpec((None, None, 1, d), lambda b, i, w=w: (layer, boff + b, 0, w)) for w in which]


def _modulated(x_ref, shift_ref, scale_ref):
    return (x_ref[...] * (1.0 + scale_ref[...]) + shift_ref[...]).astype(BF16)


def _const_spec(shape):
    return pl.BlockSpec(shape, lambda b, i: (0,) * len(shape))


def _rows_spec(tm, n):
    return pl.BlockSpec((None, tm, n), lambda b, i: (b, i, 0))


def _gdn_proj_kernel(x_ref, shift_ref, scale_ref, wqkv_ref, wba_ref, wz_ref, qkv_ref, ba_ref, z_ref):
    u = _modulated(x_ref, shift_ref, scale_ref)
    d = x_ref.shape[-1]
    for s in range(3):
        qkv_ref[:, s * d:(s + 1) * d] = _dot(u, wqkv_ref[:, s * d:(s + 1) * d])
    ba_ref[...] = _dot(u, wba_ref[...])
    z_ref[...] = _dot(u, wz_ref[...]).astype(BF16)


def _gdn_proj(x, mod4, layer, boff, w_in):
    b, t, d = x.shape
    tm = _row_tile(t, 256)
    wqkv = w_in[:, :3 * d].astype(BF16)
    wba = jnp.pad(w_in[:, 3 * d:3 * d + 2 * H_G], ((0, 0), (0, LANES - 2 * H_G))).astype(BF16)
    wz = w_in[:, 3 * d + 2 * H_G:].astype(BF16)
    return pl.pallas_call(
        _gdn_proj_kernel,
        grid=(b, t // tm),
        in_specs=[_rows_spec(tm, d)] + _mod_specs(layer, boff, d, (0, 1))
                 + [_const_spec((d, 3 * d)), _const_spec((d, LANES)), _const_spec((d, d))],
        out_specs=[_rows_spec(tm, 3 * d), _rows_spec(tm, LANES), _rows_spec(tm, d)],
        out_shape=[jax.ShapeDtypeStruct((b, t, 3 * d), F32), jax.ShapeDtypeStruct((b, t, LANES), F32),
                   jax.ShapeDtypeStruct((b, t, d), BF16)],
        compiler_params=_cparams("arbitrary", "arbitrary"),
        name="gdn_in_proj",
    )(x, mod4, mod4, wqkv, wba, wz)


def _unit_lower_inverse_minus_identity(a):
    n = a.shape[0]
    r = _iota((n, n), 0)
    c = _iota((n, n), 1)

    def mm(x, y):
        return _dot(x.astype(BF16), y.astype(BF16))

    base = 8
    d = jnp.where(_div_pow2(r, base) == _div_pow2(c, base), a, 0.0)
    d2 = mm(d, d)
    d4 = mm(d2, d2)
    nn = -d
    nn = nn + d2 + mm(nn, d2)
    nn = nn + d4 + mm(nn, d4)
    m = base
    while m < n:
        same_pair = _div_pow2(r, 2 * m) == _div_pow2(c, 2 * m)
        off = jnp.where(same_pair, jnp.where(_div_pow2(r, m) != _div_pow2(c, m), a, 0.0), 0.0)
        y = off + mm(nn, off)
        x = y + mm(y, nn)
        nn = nn - x
        m *= 2
    return nn


def _gdn_kernel(qkv_ref, ba_ref, z_ref, cbuf_ref, s0_ref, cw_ref, avec_ref, dtvec_ref, nw_ref,
                o_ref, s_ref, ext_ref, *, chunk):
    ti = pl.program_id(1)
    tb = qkv_ref.shape[0]
    d = z_ref.shape[-1]
    dk = d // H_G
    n_chunks = tb // chunk

    @pl.when(ti == 0)
    def _():
        ext_ref[0:SUBLANES, :] = cbuf_ref[...]
        s_ref[...] = s0_ref[...]

    ext_ref[SUBLANES:SUBLANES + tb, :] = qkv_ref[...]

    ri = _iota((chunk, chunk), 0)
    ci = _iota((chunk, chunk), 1)
    tri = (ri >= ci).astype(BF16)
    eye_l = (_iota((LANES, LANES), 0) == _iota((LANES, LANES), 1)).astype(BF16)
    incl = ri >= ci
    strict = ri > ci
    cw = cw_ref[...]
    neg_exp_a = -jnp.exp(avec_ref[...])
    dtv = dtvec_ref[...]
    nw = nw_ref[...]

    def chunk_body(cidx, carry):
        r0 = pl.multiple_of(cidx * chunk, chunk)
        ba = ba_ref[pl.ds(r0, chunk), :]
        beta_all = _sigmoid(ba)
        g_all = neg_exp_a * _softplus(ba + dtv)
        gcum = _dot_exact_l(tri, g_all)
        gcum_t = _dot_exact_nt(eye_l, gcum)
        for h in range(H_G):
            def conv_slab(c0):
                win = ext_ref[pl.ds(r0, chunk + SUBLANES), c0:c0 + dk]
                acc = None
                for j in range(CONV_W):
                    lo = SUBLANES - CONV_W + 1 + j
                    term = cw[j:j + 1, c0:c0 + dk] * win[lo:lo + chunk, :]
                    acc = term if acc is None else acc + term
                return _silu(acc)

            q = conv_slab(h * dk)
            k = conv_slab(d + h * dk)
            v = conv_slab(2 * d + h * dk)
            q = q * lax.rsqrt(jnp.sum(q * q, axis=-1, keepdims=True) + NORM_EPS) * (dk ** -0.5)
            k = k * lax.rsqrt(jnp.sum(k * k, axis=-1, keepdims=True) + NORM_EPS)
            beta = beta_all[:, h:h + 1]
            gcol = gcum[:, H_G + h:H_G + h + 1]
            grow = gcum_t[H_G + h:H_G + h + 1, :]
            dec_incl = jnp.exp(jnp.where(incl, gcol - grow, NEG_INF))
            dec_strict = jnp.where(strict, dec_incl, 0.0)
            kb = k * beta
            kbf = k.astype(BF16)
            a_mat = _dot_nt(kb.astype(BF16), kbf) * dec_strict
            qk = _dot_nt(q.astype(BF16), kbf) * dec_incl
            exp_g = jnp.exp(gcol)
            rhs = jnp.concatenate([v * beta, kb * exp_g], axis=1)
            nn = _unit_lower_inverse_minus_identity(a_mat)
            sol = rhs + _dot(nn.astype(BF16), rhs.astype(BF16))
            u_new = sol[:, :dk]
            w_dec = sol[:, dk:]
            s = s_ref[h]
            sb = s.astype(BF16)
            v_res = u_new - _dot(w_dec.astype(BF16), sb)
            o = _dot((q * exp_g).astype(BF16), sb) + _dot(qk.astype(BF16), v_res.astype(BF16))
            g_last = gcol[chunk - 1:chunk, :]
            k_dec = k * jnp.exp(g_last - gcol)
            s_ref[h] = s * jnp.exp(g_last) + _dot_tn(k_dec.astype(BF16), v_res.astype(BF16))
            on = o * lax.rsqrt(jnp.mean(o * o, axis=-1, keepdims=True) + NORM_EPS) * nw
            zz = z_ref[pl.ds(r0, chunk), h * dk:(h + 1) * dk].astype(F32)
            o_ref[pl.ds(r0, chunk), h * dk:(h + 1) * dk] = (on * _silu(zz)).astype(BF16)
        return carry

    lax.fori_loop(0, n_chunks, chunk_body, 0)
    ext_ref[0:SUBLANES, :] = ext_ref[tb:tb + SUBLANES, :]


def _gdn_mix(qkv, ba, z, conv_buf, s0, conv_w, a_log, dt_bias, norm_w):
    b, t, d3 = qkv.shape
    d = d3 // 3
    dk = d // H_G
    chunk = min(GDN_CHUNK, t)
    tb = _row_tile(t, 4 * chunk)
    assert t >= CONV_W - 1 and tb >= SUBLANES
    cbuf = jnp.pad(conv_buf.astype(F32), ((0, 0), (SUBLANES - (CONV_W - 1), 0), (0, 0)))
    cw = jnp.pad(conv_w.astype(F32), ((0, SUBLANES - CONV_W), (0, 0)))
    avec = jnp.pad(a_log.astype(F32), (H_G, LANES - 2 * H_G)).reshape(1, LANES)
    dtvec = jnp.pad(dt_bias.astype(F32), (H_G, LANES - 2 * H_G)).reshape(1, LANES)
    nw = norm_w.astype(F32).reshape(1, dk)
    state_spec = pl.BlockSpec((None, H_G, dk, dk), lambda bb, i: (bb, 0, 0, 0))
    return pl.pallas_call(
        functools.partial(_gdn_kernel, chunk=chunk),
        grid=(b, t // tb),
        in_specs=[_rows_spec(tb, d3), _rows_spec(tb, LANES), _rows_spec(tb, d),
                  pl.BlockSpec((None, SUBLANES, d3), lambda bb, i: (bb, 0, 0)), state_spec,
                  _const_spec((SUBLANES, d3)), _const_spec((1, LANES)), _const_spec((1, LANES)),
                  _const_spec((1, dk))],
        out_specs=[_rows_spec(tb, d), state_spec],
        out_shape=[jax.ShapeDtypeStruct((b, t, d), BF16), jax.ShapeDtypeStruct((b, H_G, dk, dk), F32)],
        scratch_shapes=[pltpu.VMEM((tb + SUBLANES, d3), F32)],
        compiler_params=_cparams("arbitrary", "arbitrary"),
        name="gdn_mixer",
    )(qkv, ba, z, cbuf, s0.astype(F32), cw, avec, dtvec, nw)


def _out_proj_kernel(o_ref, x_ref, gate_ref, w_ref, g_ref, b_ref, y_ref):
    h = _dot(o_ref[...], w_ref[...])
    y = DEEPNORM_ALPHA * x_ref[...] + (1.0 + gate_ref[...]) * h
    mu = jnp.mean(y, axis=-1, keepdims=True)
    yc = y - mu
    var = jnp.mean(yc * yc, axis=-1, keepdims=True)
    y_ref[...] = yc * lax.rsqrt(var + LN_EPS) * g_ref[...] + b_ref[...]


def _out_proj(o, x, mod4, layer, boff, w_out, ln_g, ln_b):
    b, t, d = x.shape
    kdim = o.shape[-1]
    tm = _row_tile(t, 512)
    return pl.pallas_call(
        _out_proj_kernel,
        grid=(b, t // tm),
        in_specs=[_rows_spec(tm, kdim), _rows_spec(tm, d)] + _mod_specs(layer, boff, d, (2,))
                 + [_const_spec((kdim, d)), _const_spec((1, d)), _const_spec((1, d))],
        out_specs=_rows_spec(tm, d),
        out_shape=jax.ShapeDtypeStruct((b, t, d), F32),
        compiler_params=_cparams("arbitrary", "arbitrary"),
        name="out_proj_postnorm",
    )(o, x, mod4, w_out.astype(BF16), ln_g.reshape(1, d), ln_b.reshape(1, d))


def _qkvz_outputs(u, w_ref, q_ref, k32_ref, v32_ref, kb_ref, vb_ref, z_ref, q_scale, transposed):
    d = u.shape[-1]
    if transposed:
        q_ref[...] = (_dot_nt(w_ref[:, 0:d], u) * q_scale).astype(BF16)
    else:
        q_ref[...] = (_dot(u, w_ref[:, 0:d]) * q_scale).astype(BF16)
    k = _dot(u, w_ref[:, d:2 * d])
    k32_ref[...] = k
    kb_ref[...] = k.astype(BF16)
    v = _dot(u, w_ref[:, 2 * d:3 * d])
    v32_ref[...] = v
    vb_ref[...] = (v.T if transposed else v).astype(BF16)
    z_ref[...] = _dot(u, w_ref[:, 3 * d:4 * d]).astype(BF16)


def _qkvz_weights(w_in, d, transposed):
    w = w_in[:, :4 * d]
    if transposed:
        w = jnp.concatenate([w[:, :d].T, w[:, d:]], axis=1)
    return w.astype(BF16)


def _qkvz_specs(b, t, d, tm, transposed):
    cols_spec = pl.BlockSpec((None, d, tm), lambda bb, i: (bb, 0, i))
    rows = _rows_spec(tm, d)
    f32o = jax.ShapeDtypeStruct((b, t, d), F32)
    bf16o = jax.ShapeDtypeStruct((b, t, d), BF16)
    bf16t = jax.ShapeDtypeStruct((b, d, t), BF16)
    if transposed:
        return [cols_spec, rows, rows, rows, cols_spec, rows], [bf16t, f32o, f32o, bf16o, bf16t, bf16o]
    return [rows] * 6, [bf16o, f32o, f32o, bf16o, bf16o, bf16o]


def _fox_proj_kernel(x_ref, shift_ref, scale_ref, w_ref, wf_ref, bf_ref,
                     q_ref, k32_ref, v32_ref, kb_ref, vb_ref, z_ref, logf_ref, *, q_scale, transposed):
    u = _modulated(x_ref, shift_ref, scale_ref)
    _qkvz_outputs(u, w_ref, q_ref, k32_ref, v32_ref, kb_ref, vb_ref, z_ref, q_scale, transposed)
    f = _dot(u, wf_ref[...])[:, :H_F] + bf_ref[...]
    logf_ref[...] = -_softplus(-f)


def _fox_proj(x, mod4, layer, boff, w_in, b_f, transposed):
    b, t, d = x.shape
    tm = _row_tile(t, 256)
    w = _qkvz_weights(w_in, d, transposed)
    wf = jnp.pad(w_in[:, 4 * d:], ((0, 0), (0, LANES - H_F))).astype(BF16)
    out_specs, out_shape = _qkvz_specs(b, t, d, tm, transposed)
    return pl.pallas_call(
        functools.partial(_fox_proj_kernel, q_scale=(d // H_F) ** -0.5, transposed=transposed),
        grid=(b, t // tm),
        in_specs=[_rows_spec(tm, d)] + _mod_specs(layer, boff, d, (0, 1))
                 + [_const_spec((d, 4 * d)), _const_spec((d, LANES)), _const_spec((1, H_F))],
        out_specs=out_specs + [_rows_spec(tm, H_F)],
        out_shape=out_shape + [jax.ShapeDtypeStruct((b, t, H_F), F32)],
        compiler_params=_cparams("arbitrary", "arbitrary"),
        name="fox_in_proj",
    )(x, mod4, mod4, w, wf, b_f.astype(F32).reshape(1, H_F))


def _cumsum_kernel(x_ref, c0_ref, cn_ref, ct_ref, *, blk):
    s, h = x_ref.shape
    tri = (_iota((blk, blk), 0) >= _iota((blk, blk), 1)).astype(BF16)
    eye_h = (_iota((h, h), 0) == _iota((h, h), 1)).astype(BF16)
    carry = c0_ref[...]
    for i in range(s // blk):
        c = _dot_exact_l(tri, x_ref[i * blk:(i + 1) * blk, :]) + carry
        cn_ref[i * blk:(i + 1) * blk, :] = c
        ct_ref[:, i * blk:(i + 1) * blk] = _dot_exact_nt(eye_h, c)
        carry = c[blk - 1:blk, :]


def _cumsum_time(x, c0):
    b, s, h = x.shape
    blk = 256 if s % 256 == 0 else s
    return pl.pallas_call(
        functools.partial(_cumsum_kernel, blk=blk),
        grid=(b,),
        in_specs=[pl.BlockSpec((None, s, h), lambda bb: (bb, 0, 0)),
                  pl.BlockSpec((None, 1, h), lambda bb: (bb, 0, 0))],
        out_specs=[pl.BlockSpec((None, s, h), lambda bb: (bb, 0, 0)),
                   pl.BlockSpec((None, h, s), lambda bb: (bb, 0, 0))],
        out_shape=[jax.ShapeDtypeStruct((b, s, h), F32), jax.ShapeDtypeStruct((b, h, s), F32)],
        compiler_params=_cparams("arbitrary"),
        name="logf_cumsum",
    )(x, c0)


AUG = LANES // H_F


def _fox_aug_kernel(x_ref, kaug_ref, qaugt_ref, *, blk):
    s, h = x_ref.shape
    tri = (_iota((blk, blk), 0) >= _iota((blk, blk), 1)).astype(BF16)
    lane_h = _iota((h, LANES), 1)
    row_h = _iota((h, LANES), 0)
    ek = [jnp.where(lane_h == row_h * AUG + part, -1.0, 0.0).astype(BF16) for part in range(3)]
    row_q = _iota((LANES, h), 0)
    col_q = _iota((LANES, h), 1)
    eq = [jnp.where(row_q == col_q * AUG + 3 + part, 1.0, 0.0).astype(BF16) for part in range(3)]
    k_slot = _iota((blk, LANES), 1) & (AUG - 1)
    k_ones = jnp.where(k_slot >= 3, jnp.where(k_slot < 6, 1.0, 0.0), 0.0)
    q_ones = jnp.where((_iota((LANES, blk), 0) & (AUG - 1)) < 3, 1.0, 0.0)
    carry = jnp.zeros((1, h), F32)
    for i in range(s // blk):
        c = _dot_exact_l(tri, x_ref[i * blk:(i + 1) * blk, :]) + carry
        parts = _split3(c)
        kaug = k_ones
        qaugt = q_ones
        for part in range(3):
            kaug = kaug + _dot(parts[part], ek[part])
            qaugt = qaugt + _dot_nt(eq[part], parts[part])
        kaug_ref[i * blk:(i + 1) * blk, :] = kaug.astype(BF16)
        qaugt_ref[:, i * blk:(i + 1) * blk] = qaugt.astype(BF16)
        carry = c[blk - 1:blk, :]


def _fox_aug(logf):
    b, s, h = logf.shape
    assert h * AUG == LANES and AUG >= 6
    blk = 256 if s % 256 == 0 else s
    return pl.pallas_call(
        functools.partial(_fox_aug_kernel, blk=blk),
        grid=(b,),
        in_specs=[pl.BlockSpec((None, s, h), lambda bb: (bb, 0, 0))],
        out_specs=[pl.BlockSpec((None, s, LANES), lambda bb: (bb, 0, 0)),
                   pl.BlockSpec((None, LANES, s), lambda bb: (bb, 0, 0))],
        out_shape=[jax.ShapeDtypeStruct((b, s, LANES), BF16), jax.ShapeDtypeStruct((b, LANES, s), BF16)],
        compiler_params=_cparams("arbitrary"),
        name="fox_bias_operands",
    )(logf)


def _softmax_t_step(s, vt, m_ref, l_ref, acc_ref):
    m_prev = m_ref[...]
    m_new = jnp.maximum(m_prev, jnp.max(s, axis=0, keepdims=True))
    alpha = jnp.exp(m_prev - m_new)
    p = jnp.exp(s - m_new)
    l_ref[...] = alpha * l_ref[...] + jnp.sum(p, axis=0, keepdims=True)
    acc_ref[...] = alpha * acc_ref[...] + _dot(vt, p.astype(BF16))
    m_ref[...] = m_new


def _fox_attn_kernel(qt_ref, k_ref, kaug_ref, vt_ref, qaugt_ref, z_ref, o_ref,
                     qcat_ref, s_ref, m_ref, l_ref, acc_ref, *, tk):
    hp = pl.program_id(1)
    qi = pl.program_id(2)
    tq = qt_ref.shape[1]
    assert tq == 2 * tk
    dh = LANES // 2
    row = _iota((LANES, tq), 0)
    qt = qt_ref[...].astype(F32)
    qa = qaugt_ref[...].astype(F32)
    for hh in range(2):
        qcat_ref[hh, 0:LANES, :] = jnp.where(_div_pow2(row, dh) == hh, qt, 0.0).astype(BF16)
        qcat_ref[hh, LANES:2 * LANES, :] = jnp.where(_div_pow2(row, AUG) == hp * 2 + hh, qa, 0.0).astype(BF16)
    m_ref[...] = jnp.full(m_ref.shape, NEG_INF, F32)
    l_ref[...] = jnp.zeros(l_ref.shape, F32)
    acc_ref[...] = jnp.zeros(acc_ref.shape, F32)
    n_full = qi * 2
    key_in = _iota((tk, tq), 0)
    qry_pos = qi * tq + _iota((tk, tq), 1)

    def scores(j, slot):
        k0 = pl.multiple_of(j * tk, tk)
        kcat = jnp.concatenate([k_ref[pl.ds(k0, tk), :], kaug_ref[pl.ds(k0, tk), :]], axis=1)
        for hh in range(2):
            s_ref[slot, hh] = _dot(kcat, qcat_ref[hh])

    def consume(j, slot, masked):
        k0 = pl.multiple_of(j * tk, tk)
        for hh in range(2):
            s = s_ref[slot, hh]
            if masked:
                s = jnp.where(k0 + key_in <= qry_pos, s, NEG_INF)
            _softmax_t_step(s, vt_ref[hh * dh:(hh + 1) * dh, pl.ds(k0, tk)], m_ref.at[hh], l_ref.at[hh],
                            acc_ref.at[hh])

    scores(0, 0)

    def body(i, carry):
        a = 2 * i
        scores(a + 1, 1)
        consume(a, 0, False)
        scores(a + 2, 0)
        consume(a + 1, 1, False)
        return carry

    lax.fori_loop(0, qi, body, 0)
    scores(n_full + 1, 1)
    consume(n_full, 0, True)
    consume(n_full + 1, 1, True)
    o_t = jnp.concatenate([acc_ref[0] / l_ref[0], acc_ref[1] / l_ref[1]], axis=0)
    o_ref[...] = (o_t.T * _silu(z_ref[...].astype(F32))).astype(BF16)


def _fox_attn_prompt(qt, kb, kaug, vt, qaugt, z):
    b, d, t = qt.shape
    tq, tk = ATTN_TQ, ATTN_TK
    assert t % tq == 0
    hpairs = d // LANES
    dh = LANES // 2
    return pl.pallas_call(
        functools.partial(_fox_attn_kernel, tk=tk),
        grid=(b, hpairs, t // tq),
        in_specs=[pl.BlockSpec((None, LANES, tq), lambda bb, hp, i: (bb, hp, i)),
                  pl.BlockSpec((None, t, LANES), lambda bb, hp, i: (bb, 0, hp)),
                  pl.BlockSpec((None, t, LANES), lambda bb, hp, i: (bb, 0, 0)),
                  pl.BlockSpec((None, LANES, t), lambda bb, hp, i: (bb, hp, 0)),
                  pl.BlockSpec((None, LANES, tq), lambda bb, hp, i: (bb, 0, i)),
                  pl.BlockSpec((None, tq, LANES), lambda bb, hp, i: (bb, i, hp))],
        out_specs=pl.BlockSpec((None, tq, LANES), lambda bb, hp, i: (bb, i, hp)),
        out_shape=jax.ShapeDtypeStruct((b, t, d), BF16),
        scratch_shapes=[pltpu.VMEM((2, 2 * LANES, tq), BF16), pltpu.VMEM((2, 2, tk, tq), F32),
                        pltpu.VMEM((2, 1, tq), F32), pltpu.VMEM((2, 1, tq), F32), pltpu.VMEM((2, dh, tq), F32)],
        compiler_params=_cparams("arbitrary", "arbitrary", "arbitrary"),
        name="fox_attention_prompt",
    )(qt, kb, kaug, vt, qaugt, z)


def _fox_decode_kernel(q_ref, kp_ref, vp_ref, kn_ref, vn_ref, z_ref, cq_ref, ckt_ref, o_ref):
    hp = pl.program_id(1)
    t = q_ref.shape[0]
    p_len = kp_ref.shape[0]
    dh = LANES // 2
    lane = _iota((t, LANES), 1)
    q = q_ref[...]
    kp = kp_ref[...].astype(BF16)
    vp = vp_ref[...].astype(BF16)
    kn = kn_ref[...]
    vn = vn_ref[...]
    cq_all = cq_ref[...]
    hlane = _iota(cq_all.shape, 1)
    causal = _iota((t, t), 1) <= _iota((t, t), 0)
    outs = []
    for hh in range(2):
        h = hp * 2 + hh
        qm = jnp.where(_div_pow2(lane, dh) == hh, q, jnp.zeros_like(q))
        cq = jnp.sum(jnp.where(hlane == h, cq_all, 0.0), axis=-1, keepdims=True)
        ck = ckt_ref[pl.ds(h, 1), :]
        s_p = _dot_nt(qm, kp) + cq - ck[:, :p_len]
        s_n = jnp.where(causal, _dot_nt(qm, kn) + cq - ck[:, p_len:], NEG_INF)
        m = jnp.maximum(jnp.max(s_p, axis=-1, keepdims=True), jnp.max(s_n, axis=-1, keepdims=True))
        e_p = jnp.exp(s_p - m)
        e_n = jnp.exp(s_n - m)
        den = jnp.sum(e_p, axis=-1, keepdims=True) + jnp.sum(e_n, axis=-1, keepdims=True)
        outs.append((_dot(e_p.astype(BF16), vp) + _dot(e_n.astype(BF16), vn)) / den)
    o = jnp.where(lane < dh,outs[0], outs[1])
    o_ref[...] = (o * _silu(z_ref[...].astype(F32))).astype(BF16)


def _fox_attn_sample(q, k_past, v_past, kb, vb, z, cq_new, cum_t):
    b, t, d = q.shape
    p_len = k_past.shape[1]
    hpairs = d // LANES
    new_spec = pl.BlockSpec((None, t, LANES), lambda bb, hp: (bb, 0, hp))
    past_spec = pl.BlockSpec((None, p_len, LANES), lambda bb, hp: (bb, 0, hp))
    return pl.pallas_call(
        _fox_decode_kernel,
        grid=(b, hpairs),
        in_specs=[new_spec, past_spec, past_spec, new_spec, new_spec, new_spec,
                  pl.BlockSpec((None, t, H_F), lambda bb, hp: (bb, 0, 0)),
                  pl.BlockSpec((None, H_F, p_len + t), lambda bb, hp: (bb, 0, 0))],
        out_specs=new_spec,
        out_shape=jax.ShapeDtypeStruct((b, t, d), BF16),
        compiler_params=_cparams("arbitrary", "arbitrary"),
        name="fox_attention_sample",
    )(q, k_past, v_past, kb, vb, z, cq_new, cum_t)


def _diff_proj_kernel(x_ref, shift_ref, scale_ref, w_ref, q_ref, k32_ref, v32_ref, kb_ref, vb_ref, z_ref,
                      *, q_scale, transposed):
    u = _modulated(x_ref, shift_ref, scale_ref)
    _qkvz_outputs(u, w_ref, q_ref, k32_ref, v32_ref, kb_ref, vb_ref, z_ref, q_scale, transposed)


def _diff_proj(x, mod4, layer, boff, w_in, transposed):
    b, t, d = x.shape
    tm = _row_tile(t, 256)
    out_specs, out_shape = _qkvz_specs(b, t, d, tm, transposed)
    return pl.pallas_call(
        functools.partial(_diff_proj_kernel, q_scale=(d // (2 * H_D)) ** -0.5, transposed=transposed),
        grid=(b, t // tm),
        in_specs=[_rows_spec(tm, d)] + _mod_specs(layer, boff, d, (0, 1)) + [_const_spec((d, 4 * d))],
        out_specs=out_specs,
        out_shape=out_shape,
        compiler_params=_cparams("arbitrary", "arbitrary"),
        name="diff_in_proj",
    )(x, mod4, mod4, _qkvz_weights(w_in, d, transposed))


def _t5_thresholds():
    nb = N_BUCKETS // 2
    max_exact = nb // 2
    steps = nb - max_exact
    ratio = MAX_DISTANCE // max_exact
    out = []
    for kk in range(1, nb - max_exact):
        target = max_exact ** steps * ratio ** kk
        n = max_exact
        while n ** steps < target:
            n += 1
        out.append(n)
    return nb, max_exact, out


def _bias_kernel(tbl_ref, o_ref, *, q0, k0, keys_on_rows):
    h = pl.program_id(0)
    shape = o_ref.shape
    kdim, qdim = (0, 1) if keys_on_rows else (1, 0)
    rel = (k0 + _iota(shape, kdim)) - (q0 + _iota(shape, qdim))
    nb, max_exact, thr = _t5_thresholds()
    n = jnp.abs(rel)
    large = jnp.full(shape, max_exact, jnp.int32)
    for tval in thr:
        large = large + (n >= tval).astype(jnp.int32)
    bucket = jnp.where(rel > 0, nb, 0) + jnp.where(n < max_exact, n, large)
    acc = jnp.zeros(shape, F32)
    for bkt in range(N_BUCKETS):
        acc = jnp.where(bucket == bkt, tbl_ref[bkt * H_D + h], acc)
    if keys_on_rows:
        acc = acc - tbl_ref[(nb - 1) * H_D + h]
    o_ref[...] = acc


def _bias_tile(rel_table, q0, nq, k0, nk, keys_on_rows=False):
    shape = (nk, nq) if keys_on_rows else (nq, nk)
    return pl.pallas_call(
        functools.partial(_bias_kernel, q0=q0, k0=k0, keys_on_rows=keys_on_rows),
        grid=(H_D,),
        in_specs=[pl.BlockSpec(memory_space=pltpu.SMEM)],
        out_specs=pl.BlockSpec((None,) + shape, lambda h: (h, 0, 0)),
        out_shape=jax.ShapeDtypeStruct((H_D,) + shape, F32),
        compiler_params=_cparams("arbitrary"),
        name="t5_bias_tile",
    )(rel_table.astype(F32).reshape(N_BUCKETS * H_D))


def _diff_lambda(lam_ref):
    lam = lam_ref[...]
    s1 = jnp.sum(lam[0:1, :] * lam[1:2, :], axis=-1, keepdims=True)
    s2 = jnp.sum(lam[2:3, :] * lam[3:4, :], axis=-1, keepdims=True)
    return jnp.exp(s1) - jnp.exp(s2) + LAMBDA_INIT


def _diff_epilogue(o, z_ref, subln_ref, o_ref):
    on = o * lax.rsqrt(jnp.mean(o * o, axis=-1, keepdims=True) + NORM_EPS) * subln_ref[...]
    on = on * (1.0 - LAMBDA_INIT)
    o_ref[...] = (on * _silu(z_ref[...].astype(F32))).astype(BF16)


def _diff_attn_kernel(tbl_ref, qt_ref, k_ref, vt_ref, z_ref, biasm_ref, bias0_ref, bias1_ref, lam_ref, subln_ref,
                      o_ref, qcat_ref, s_ref, m_ref, l_ref, acc_ref, *, tk):
    h = pl.program_id(1)
    qi = pl.program_id(2)
    tq = qt_ref.shape[1]
    assert tq == 2 * tk
    dh = LANES // 2
    row = _iota((LANES, tq), 0)
    qt = qt_ref[...].astype(F32)
    nb, _, _ = _t5_thresholds()
    far = _split3(jnp.full((LANES, tq), tbl_ref[(nb - 1) * H_D + h], F32))
    far_rows = jnp.zeros((LANES, tq), F32)
    for part in range(3):
        far_rows = jnp.where(row == part, far[part].astype(F32), far_rows)
    for br in range(2):
        qcat_ref[br, 0:LANES, :] = jnp.where(_div_pow2(row, dh) == br, qt, 0.0).astype(BF16)
        qcat_ref[br, LANES:2 * LANES, :] = far_rows.astype(BF16)
    ones_aug = jnp.where(_iota((tk, LANES), 1) < 3, 1.0, 0.0).astype(BF16)
    m_ref[...] = jnp.full(m_ref.shape, NEG_INF, F32)
    l_ref[...] = jnp.zeros(l_ref.shape, F32)
    acc_ref[...] = jnp.zeros(acc_ref.shape, F32)
    key_in = _iota((tk, tq), 0)
    qry_chunk = _div_pow2(_iota((tk, tq), 1), ATTN_CHUNK)

    def scores(j, slot):
        k0 = pl.multiple_of(j * tk, tk)
        kcat = jnp.concatenate([k_ref[pl.ds(k0, tk), :], ones_aug], axis=1)
        for br in range(2):
            s_ref[slot, br] = _dot(kcat, qcat_ref[br])

    def consume(j, slot, bias_ref=None, key_off=None):
        k0 = pl.multiple_of(j * tk, tk)
        for br in range(2):
            s = s_ref[slot, br]
            if bias_ref is not None:
                s = s + bias_ref[...]
            if key_off is not None:
                s = jnp.where(_div_pow2(key_off + key_in, ATTN_CHUNK) <= qry_chunk, s, NEG_INF)
            _softmax_t_step(s, vt_ref[:, pl.ds(k0, tk)], m_ref.at[br], l_ref.at[br], acc_ref.at[br])

    scores(0, 0)

    def body(i, carry):
        a = 2 * i
        scores(a + 1, 1)
        consume(a, 0)
        scores(a + 2, 0)
        consume(a + 1, 1)
        return carry

    lax.fori_loop(0, jnp.maximum(qi - 1, 0), body, 0)

    @pl.when(qi >= 1)
    def _():
        a = 2 * qi - 2
        scores(a + 1, 1)
        consume(a, 0)
        scores(a + 2, 0)
        consume(a + 1, 1, bias_ref=biasm_ref)

    a = 2 * qi
    scores(a + 1, 1)
    consume(a, 0, bias_ref=bias0_ref, key_off=0)
    consume(a + 1, 1, bias_ref=bias1_ref, key_off=tk)
    o_t = acc_ref[0] / l_ref[0] - _diff_lambda(lam_ref) * (acc_ref[1] / l_ref[1])
    _diff_epilogue(o_t.T, z_ref, subln_ref, o_ref)


def _lam_pack(lam_q1, lam_k1, lam_q2, lam_k2):
    rows = jnp.stack([lam_q1, lam_k1, lam_q2, lam_k2]).astype(F32)
    return jnp.pad(rows, ((0, SUBLANES - 4), (0, LANES - rows.shape[1])))


def _diff_attn_prompt(qt, kb, vt, z, rel_table, lam, subln_w):
    b, d, t = qt.shape
    tq, tk = ATTN_TQ, ATTN_TK
    assert t % tq == 0 and tk % ATTN_CHUNK == 0 and tk >= MAX_DISTANCE
    biasm = _bias_tile(rel_table, tk, tq, 0, tk, keys_on_rows=True)
    bias0 = _bias_tile(rel_table, 0, tq, 0, tk, keys_on_rows=True)
    bias1 = _bias_tile(rel_table, 0, tq, tk, tk, keys_on_rows=True)
    rows_spec = pl.BlockSpec((None, tq, LANES), lambda bb, h, i: (bb, i, h))
    bias_spec = pl.BlockSpec((None, tk, tq), lambda bb, h, i: (h, 0, 0))
    return pl.pallas_call(
        functools.partial(_diff_attn_kernel, tk=tk),
        grid=(b, H_D, t // tq),
        in_specs=[pl.BlockSpec(memory_space=pltpu.SMEM),
                  pl.BlockSpec((None, LANES, tq), lambda bb, h, i: (bb, h, i)),
                  pl.BlockSpec((None, t, LANES), lambda bb, h, i: (bb, 0, h)),
                  pl.BlockSpec((None, LANES, t), lambda bb, h, i: (bb, h, 0)),
                  rows_spec, bias_spec, bias_spec, bias_spec,
                  pl.BlockSpec((SUBLANES, LANES), lambda bb, h, i: (0, 0)),
                  pl.BlockSpec((1, LANES), lambda bb, h, i: (0, 0))],
        out_specs=rows_spec,
        out_shape=jax.ShapeDtypeStruct((b, t, d), BF16),
        scratch_shapes=[pltpu.VMEM((2, 2 * LANES, tq), BF16), pltpu.VMEM((2, 2, tk, tq), F32),
                        pltpu.VMEM((2, 1, tq), F32), pltpu.VMEM((2, 1, tq), F32), pltpu.VMEM((2, LANES, tq), F32)],
        compiler_params=_cparams("arbitrary", "arbitrary", "arbitrary"),
        name="diff_attention_prompt",
    )(rel_table.astype(F32).reshape(N_BUCKETS * H_D), qt, kb, vt, z, biasm, bias0, bias1, lam,
      subln_w.astype(F32).reshape(1, LANES))


def _diff_decode_kernel(q_ref, kp_ref, vp_ref, kn_ref, vn_ref, z_ref, bias_ref, lam_ref, subln_ref, o_ref,
                        *, p_len):
    t = q_ref.shape[0]
    dh = LANES // 2
    lane = _iota((t, LANES), 1)
    q = q_ref[...]
    kp = kp_ref[...].astype(BF16)
    vp = vp_ref[...].astype(BF16)
    kn = kn_ref[...]
    vn = vn_ref[...]
    bias = bias_ref[...]
    q_chunk = _div_pow2(p_len + _iota((t, t), 0), ATTN_CHUNK)
    kn_chunk = _div_pow2(p_len + _iota((t, t), 1), ATTN_CHUNK)
    kp_chunk = _div_pow2(_iota((t, p_len), 1), ATTN_CHUNK)
    qp_chunk = _div_pow2(p_len + _iota((t, p_len), 0), ATTN_CHUNK)
    outs = []
    for br in range(2):
        qm = jnp.where(_div_pow2(lane, dh) == br, q, jnp.zeros_like(q))
        s_p = jnp.where(kp_chunk <= qp_chunk, _dot_nt(qm, kp) + bias[:, :p_len], NEG_INF)
        s_n = jnp.where(kn_chunk <= q_chunk, _dot_nt(qm, kn) + bias[:, p_len:], NEG_INF)
        m = jnp.maximum(jnp.max(s_p, axis=-1, keepdims=True), jnp.max(s_n, axis=-1, keepdims=True))
        e_p = jnp.exp(s_p - m)
        e_n = jnp.exp(s_n - m)
        den = jnp.sum(e_p, axis=-1, keepdims=True) + jnp.sum(e_n, axis=-1, keepdims=True)
        outs.append((_dot(e_p.astype(BF16), vp) + _dot(e_n.astype(BF16), vn)) / den)
    o = outs[0] - _diff_lambda(lam_ref) * outs[1]
    _diff_epilogue(o, z_ref, subln_ref, o_ref)


def _diff_attn_sample(q, k_past, v_past, kb, vb, z, rel_table, lam, subln_w):
    b, t, d = q.shape
    p_len = k_past.shape[1]
    bias = _bias_tile(rel_table, p_len, t, 0, p_len + t)
    new_spec = pl.BlockSpec((None, t, LANES), lambda bb, h: (bb, 0, h))
    past_spec = pl.BlockSpec((None, p_len, LANES), lambda bb, h: (bb, 0, h))
    return pl.pallas_call(
        functools.partial(_diff_decode_kernel, p_len=p_len),
        grid=(b, H_D),
        in_specs=[new_spec, past_spec, past_spec, new_spec, new_spec, new_spec,
                  pl.BlockSpec((None, t, p_len + t), lambda bb, h: (h, 0, 0)),
                  pl.BlockSpec((SUBLANES, LANES), lambda bb, h: (0, 0)),
                  pl.BlockSpec((1, LANES), lambda bb, h: (0, 0))],
        out_specs=new_spec,
        out_shape=jax.ShapeDtypeStruct((b, t, d), BF16),
        compiler_params=_cparams("arbitrary", "arbitrary"),
        name="diff_attention_sample",
    )(q, k_past, v_past, kb, vb, z, bias, lam, subln_w.astype(F32).reshape(1, LANES))


def _rope_kernel(inv_ref, cos_ref, sin_ref, *, start):
    t, w = cos_ref.shape
    pos = (start + pl.program_id(0) * t + _iota((t, w), 0)).astype(F32)
    ang = pos * inv_ref[...]
    even = (_iota((t, w), 1) & 1) == 0
    cos_ref[...] = jnp.cos(ang)
    sn = jnp.sin(ang)
    sin_ref[...] = jnp.where(even, -sn, sn)


def _rope_tables(t, start, dk):
    inv_half = np.power(np.float32(ROPE_BASE), -np.arange(0, dk, 2, dtype=np.float32) / np.float32(dk))
    inv = jnp.asarray(np.repeat(inv_half.astype(np.float32), 2).reshape(1, dk))
    tt = _row_tile(t, 512)
    return pl.pallas_call(
        functools.partial(_rope_kernel, start=start),
        grid=(t // tt,),
        in_specs=[pl.BlockSpec((1, dk), lambda i: (0, 0))],
        out_specs=[pl.BlockSpec((tt, dk), lambda i: (i, 0))] * 2,
        out_shape=[jax.ShapeDtypeStruct((t, dk), F32)] * 2,
        compiler_params=_cparams("arbitrary"),
        name="rope_tables",
    )(inv)


def _rotate_pairs(x, cos, sin_signed):
    slabs = []
    for c0 in range(0, x.shape[-1], LANES):
        xs = x[:, c0:c0 + LANES]
        even = (_iota(xs.shape, 1) & 1) == 0
        slabs.append(jnp.where(even, pltpu.roll(xs, LANES - 1, 1), pltpu.roll(xs, 1, 1)))
    return x * cos + jnp.concatenate(slabs, axis=1) * sin_signed


def _ret_proj_kernel(x_ref, shift_ref, scale_ref, w_ref, cos_ref, sin_ref, q_ref, k_ref, v_ref, z_ref,
                     *, q_scale):
    u = _modulated(x_ref, shift_ref, scale_ref)
    d = x_ref.shape[-1]
    dk = cos_ref.shape[-1]
    cos = cos_ref[...]
    sn = sin_ref[...]
    for h in range(d // dk):
        qh = _dot(u, w_ref[:, h * dk:(h + 1) * dk])
        q_ref[:, h * dk:(h + 1) * dk] = (_rotate_pairs(qh, cos, sn) * q_scale).astype(BF16)
        kh = _dot(u, w_ref[:, d + h * dk:d + (h + 1) * dk])
        k_ref[:, h * dk:(h + 1) * dk] = _rotate_pairs(kh, cos, sn).astype(BF16)
    for s in range(2):
        v_ref[:, s * d:(s + 1) * d] = _dot(u, w_ref[:, (2 + s) * d:(3 + s) * d]).astype(BF16)
        z_ref[:, s * d:(s + 1) * d] = _dot(u, w_ref[:, (4 + s) * d:(5 + s) * d]).astype(BF16)


def _ret_proj(x, mod4, layer, boff, w_in, cos, sin_signed):
    b, t, d = x.shape
    dk = d // H_R
    tm = _row_tile(t, 256)
    tab_spec = pl.BlockSpec((tm, dk), lambda bb, i: (i, 0))
    return pl.pallas_call(
        functools.partial(_ret_proj_kernel, q_scale=dk ** -0.5),
        grid=(b, t // tm),
        in_specs=[_rows_spec(tm, d)] + _mod_specs(layer, boff, d, (0, 1)) + [_const_spec((d, 6 * d)), tab_spec, tab_spec],
        out_specs=[_rows_spec(tm, d), _rows_spec(tm, d), _rows_spec(tm, 2 * d), _rows_spec(tm, 2 * d)],
        out_shape=[jax.ShapeDtypeStruct((b, t, d), BF16), jax.ShapeDtypeStruct((b, t, d), BF16),
                   jax.ShapeDtypeStruct((b, t, 2 * d), BF16), jax.ShapeDtypeStruct((b, t, 2 * d), BF16)],
        compiler_params=_cparams("arbitrary", "arbitrary"),
        name="ret_in_proj",
    )(x, mod4, mod4, w_in.astype(BF16), cos, sin_signed)


def _ret_kernel(q_ref, k_ref, v_ref, z_ref, s0_ref, gn_ref, o_ref, s_ref):
    ti = pl.program_id(1)
    lr = q_ref.shape[0]
    dk = q_ref.shape[-1] // H_R
    dv = v_ref.shape[-1] // H_R

    @pl.when(ti == 0)
    def _():
        s_ref[...] = s0_ref[...]

    rel = (_iota((lr, lr), 0) - _iota((lr, lr), 1)).astype(F32)
    idx = _iota((lr, 1), 0).astype(F32)
    for h in range(H_R):
        log_gamma = math.log1p(-(2.0 ** (-5.0 - h)))
        intra = jnp.where(rel >= 0, jnp.exp(log_gamma * jnp.maximum(rel, 0.0)), 0.0)
        q_dec = jnp.exp(log_gamma * (idx + 1.0))
        k_dec = jnp.exp(log_gamma * (lr - 1.0 - idx))
        c_dec = math.exp(log_gamma * lr)
        qh = q_ref[:, h * dk:(h + 1) * dk]
        kh = k_ref[:, h * dk:(h + 1) * dk]
        vh = v_ref[:, h * dv:(h + 1) * dv]
        s = s_ref[h]
        att = _dot_nt(qh, kh) * intra
        o = _dot(att.astype(BF16), vh) + _dot(qh, s.astype(BF16)) * q_dec
        s_ref[h] = s * c_dec + _dot_tn((kh.astype(F32) * k_dec).astype(BF16), vh)
        mu = jnp.mean(o, axis=-1, keepdims=True)
        oc = o - mu
        var = jnp.mean(oc * oc, axis=-1, keepdims=True)
        on = oc * lax.rsqrt(var + LN_EPS) * gn_ref[:, h * dv:(h + 1) * dv]
        zz = z_ref[:, h * dv:(h + 1) * dv].astype(F32)
        o_ref[:, h * dv:(h + 1) * dv] = (on * _silu(zz)).astype(BF16)


def _ret_mix(q, k, v, z, s0, gn_w):
    b, t, d = q.shape
    dk = d // H_R
    dv = v.shape[-1] // H_R
    lr = _row_tile(t, 256)
    state_spec = pl.BlockSpec((None, H_R, dk, dv), lambda bb, i: (bb, 0, 0, 0))
    return pl.pallas_call(
        _ret_kernel,
        grid=(b, t // lr),
        in_specs=[_rows_spec(lr, d), _rows_spec(lr, d), _rows_spec(lr, 2 * d), _rows_spec(lr, 2 * d), state_spec,
                  _const_spec((1, 2 * d))],
        out_specs=[_rows_spec(lr, 2 * d), state_spec],
        out_shape=[jax.ShapeDtypeStruct((b, t, 2 * d), BF16), jax.ShapeDtypeStruct((b, H_R, dk, dv), F32)],
        compiler_params=_cparams("arbitrary", "arbitrary"),
        name="retention_mixer",
    )(q, k, v, z, s0.astype(F32), gn_w.astype(F32).reshape(1, 2 * d))


def _run_group(x, mod4, boff, state_gdn, state_gdn_conv, cache_fox_k, cache_fox_v, cache_fox_logf,
               cache_diff_k, cache_diff_v, state_ret, start, p):
    b, t, d = x.shape
    dk_g = d // H_G

    qkv, ba, z = _gdn_proj(x, mod4, 0, boff, p["gdn_w_in"])
    if state_gdn is None:
        state_gdn = jnp.zeros((b, H_G, dk_g, dk_g), F32)
        state_gdn_conv = jnp.zeros((b, CONV_W - 1, 3 * d), F32)
    o, gdn_state = _gdn_mix(qkv, ba, z, state_gdn_conv, state_gdn, p["gdn_conv_w"], p["gdn_a_log"],
                            p["gdn_dt_bias"], p["gdn_norm_w"])
    gdn_conv = qkv[:, t - (CONV_W - 1):, :]
    x = _out_proj(o, x, mod4, 0, boff, p["gdn_w_out"], p["ln_g"][0], p["ln_b"][0])

    prompt = cache_fox_k is None
    q, k32, v32, kb, vb, z, logf = _fox_proj(x, mod4, 1, boff, p["fox_w_in"], p["fox_b_f"], transposed=prompt)
    if prompt:
        kaug, qaugt = _fox_aug(logf)
        o = _fox_attn_prompt(q, kb, kaug, vb, qaugt, z)
    else:
        zero_c = jnp.zeros((b, 1, H_F), F32)
        p_len = cache_fox_k.shape[1]
        cum_pn, cum_pt = _cumsum_time(cache_fox_logf.astype(F32), zero_c)
        cum_n, cum_nt = _cumsum_time(logf, cum_pn[:, p_len - 1:, :])
        o = _fox_attn_sample(q, cache_fox_k.reshape(b, p_len, d), cache_fox_v.reshape(b, p_len, d), kb, vb, z,
                             cum_n, jnp.concatenate([cum_pt, cum_nt], axis=2))
    fox_k = k32.reshape(b, t, H_F, d // H_F)
    fox_v = v32.reshape(b, t, H_F, d // H_F)
    x = _out_proj(o, x, mod4, 1, boff, p["fox_w_out"], p["ln_g"][1], p["ln_b"][1])

    q, k32, v32, kb, vb, z = _diff_proj(x, mod4, 2, boff, p["diff_w_in"], transposed=prompt)
    lam = _lam_pack(p["diff_lam_q1"], p["diff_lam_k1"], p["diff_lam_q2"], p["diff_lam_k2"])
    if prompt:
        o = _diff_attn_prompt(q, kb, vb, z, p["rel_bias_table"], lam, p["diff_subln_w"])
    else:
        p_len = cache_diff_k.shape[1]
        o = _diff_attn_sample(q, cache_diff_k.reshape(b, p_len, d), cache_diff_v.reshape(b, p_len, d), kb, vb, z,
                              p["rel_bias_table"], lam, p["diff_subln_w"])
    diff_k = k32.reshape(b, t, H_D, 2, d // (2 * H_D))
    diff_v = v32.reshape(b, t, H_D, d // H_D)
    x = _out_proj(o, x, mod4, 2, boff, p["diff_w_out"], p["ln_g"][2], p["ln_b"][2])

    dk_r = d // H_R
    cos, sin_signed = _rope_tables(t, start, dk_r)
    q, k, v, z = _ret_proj(x, mod4, 3, boff, p["ret_w_in"], cos, sin_signed)
    if state_ret is None:
        state_ret = jnp.zeros((b, H_R, dk_r, 2 * d // H_R), F32)
    o, ret_state = _ret_mix(q, k, v, z, state_ret, p["ret_gn_w"])
    x = _out_proj(o, x, mod4, 3, boff, p["ret_w_out"], p["ln_g"][3], p["ln_b"][3])

    return x, gdn_state, gdn_conv, fox_k, fox_v, logf, diff_k, diff_v, ret_state


def kernel(x_prompt, x_sample, c_prompt, c_sample, state_gdn, state_gdn_conv, cache_fox_k, cache_fox_v, cache_fox_logf, cache_diff_k, cache_diff_v, state_ret, ada_w, ada_b, ln_g, ln_b, gdn_w_in, gdn_conv_w, gdn_a_log, gdn_dt_bias, gdn_norm_w, gdn_w_out, fox_w_in, fox_b_f, fox_w_out, rel_bias_table, diff_w_in, diff_lam_q1, diff_lam_k1, diff_lam_q2, diff_lam_k2, diff_subln_w, diff_w_out, ret_w_in, ret_gn_w, ret_w_out):
    p = dict(ln_g=ln_g, ln_b=ln_b, gdn_w_in=gdn_w_in, gdn_conv_w=gdn_conv_w, gdn_a_log=gdn_a_log,
             gdn_dt_bias=gdn_dt_bias, gdn_norm_w=gdn_norm_w, gdn_w_out=gdn_w_out, fox_w_in=fox_w_in,
             fox_b_f=fox_b_f, fox_w_out=fox_w_out, rel_bias_table=rel_bias_table, diff_w_in=diff_w_in,
             diff_lam_q1=diff_lam_q1, diff_lam_k1=diff_lam_k1, diff_lam_q2=diff_lam_q2, diff_lam_k2=diff_lam_k2,
             diff_subln_w=diff_subln_w, diff_w_out=diff_w_out, ret_w_in=ret_w_in, ret_gn_w=ret_gn_w,
             ret_w_out=ret_w_out)
    bp = x_prompt.shape[0]
    d = x_prompt.shape[-1]
    mod = _modulation(jnp.concatenate([c_prompt, c_sample], axis=0), ada_w, ada_b)
    mod4 = mod.reshape(mod.shape[0], mod.shape[1], 1, 3 * d)
    outs_p = _run_group(x_prompt, mod4, 0, None, None, None, None, None, None, None, None, 0, p)
    outs_s = _run_group(x_sample, mod4, bp, state_gdn, state_gdn_conv, cache_fox_k, cache_fox_v, cache_fox_logf,
                        cache_diff_k, cache_diff_v, state_ret, cache_fox_k.shape[1], p)
    return (outs_p[0], outs_s[0]) + tuple(outs_p[1:]) + tuple(outs_s[1:])
```

```python
import functools
import math

import numpy as np
import jax
import jax.numpy as jnp
from jax import lax
from jax.experimental import pallas as pl
from jax.experimental.pallas import tpu as pltpu

F32 = jnp.float32
BF16 = jnp.bfloat16

DEPTH = 4
ATTN_TQ = 512
ATTN_TK = 256
GDN_CHUNK = 64
ATTN_CHUNK = 64
DEEPNORM_ALPHA = (2.0 * DEPTH) ** 0.25
LN_EPS = 1e-5
NORM_EPS = 1e-6
NEG_INF = -1e30
LOG2E = math.log2(math.e)
H_G, H_F, H_D, H_R = 8, 16, 8, 4
CONV_W = 4
DIFF_LAYER = 2
LAMBDA_INIT = 0.8 - 0.6 * math.exp(-0.3 * DIFF_LAYER)
N_BUCKETS = 32
MAX_DISTANCE = 128
ROPE_BASE = 10000.0

LANES = 128
SUBLANES = 8
VMEM_LIMIT = 56 * 1024 * 1024


def _cparams(*sem):
    return pltpu.CompilerParams(dimension_semantics=sem, vmem_limit_bytes=VMEM_LIMIT)


def _sigmoid(x):
    return 1.0 / (1.0 + jnp.exp(-x))


def _silu(x):
    return x * _sigmoid(x)


def _softplus(x):
    return jnp.maximum(x, 0.0) + jnp.log(1.0 + jnp.exp(-jnp.abs(x)))


def _dot(a, b):
    return jnp.dot(a, b, preferred_element_type=F32)


def _dot_nt(a, b):
    return lax.dot_general(a, b, (((1,), (1,)), ((), ())), preferred_element_type=F32)


def _dot_tn(a, b):
    return lax.dot_general(a, b, (((0,), (0,)), ((), ())), preferred_element_type=F32)


def _split3(x):
    x1 = x.astype(BF16)
    r1 = x - x1.astype(F32)
    x2 = r1.astype(BF16)
    x3 = (r1 - x2.astype(F32)).astype(BF16)
    return x1, x2, x3


def _dot_exact_l(m01, x):
    x1, x2, x3 = _split3(x)
    return _dot(m01, x1) + _dot(m01, x2) + _dot(m01, x3)


def _dot_exact_nt(m01, x):
    x1, x2, x3 = _split3(x)
    return _dot_nt(m01, x1) + _dot_nt(m01, x2) + _dot_nt(m01, x3)


def _iota(shape, dim):
    return lax.broadcasted_iota(jnp.int32, shape, dim)


def _div_pow2(x, n):
    assert n & (n - 1) == 0
    return jnp.right_shift(x, n.bit_length() - 1)


def _row_tile(t, pref):
    return pref if t % pref == 0 else t


def _mod_kernel(c_ref, w_ref, b_ref, o_ref):
    s = _silu(c_ref[...])
    w = w_ref[...]
    s1 = s.astype(BF16)
    s2 = (s - s1.astype(F32)).astype(BF16)
    w1 = w.astype(BF16)
    w2 = (w - w1.astype(F32)).astype(BF16)
    o_ref[...] = _dot(s1, w1) + _dot(s1, w2) + _dot(s2, w1) + b_ref[...]


def _modulation(c_all, ada_w, ada_b):
    nb, d = c_all.shape
    depth, _, n = ada_w.shape
    tn = 1024
    return pl.pallas_call(
        _mod_kernel,
        grid=(depth, n // tn),
        in_specs=[pl.BlockSpec((nb, d), lambda l, j: (0, 0)),
                  pl.BlockSpec((None, d, tn), lambda l, j: (l, 0, j)),
                  pl.BlockSpec((None, 1, tn), lambda l, j: (l, 0, j))],
        out_specs=pl.BlockSpec((None, nb, tn), lambda l, j: (l, 0, j)),
        out_shape=jax.ShapeDtypeStruct((depth, nb, n), F32),
        compiler_params=_cparams("arbitrary", "arbitrary"),
        name="adaln_modulation",
    )(c_all, ada_w, ada_b.reshape(depth, 1, n))


def _mod_specs(layer, boff, d, which):
    return [pl.BlockSpec((None, None, 1, d), lambda b, i, w=w: (layer, boff + b, 0, w)) for w in which]


def _modulated(x_ref, shift_ref, scale_ref):
    return (x_ref[...] * (1.0 + scale_ref[...]) + shift_ref[...]).astype(BF16)


def _const_spec(shape):
    return pl.BlockSpec(shape, lambda b, i: (0,) * len(shape))


def _rows_spec(tm, n):
    return pl.BlockSpec((None, tm, n), lambda b, i: (b, i, 0))


def _gdn_proj_kernel(x_ref, shift_ref, scale_ref, wqkv_ref, wba_ref, wz_ref, qkv_ref, ba_ref, z_ref):
    u = _modulated(x_ref, shift_ref, scale_ref)
    d = x_ref.shape[-1]
    for s in range(3):
        qkv_ref[:, s * d:(s + 1) * d] = _dot(u, wqkv_ref[:, s * d:(s + 1) * d])
    ba_ref[...] = _dot(u, wba_ref[...])
    z_ref[...] = _dot(u, wz_ref[...]).astype(BF16)


def _gdn_proj(x, mod4, layer, boff, w_in):
    b, t, d = x.shape
    tm = _row_tile(t, 256)
    wqkv = w_in[:, :3 * d].astype(BF16)
    wba = jnp.pad(w_in[:, 3 * d:3 * d + 2 * H_G], ((0, 0), (0, LANES - 2 * H_G))).astype(BF16)
    wz = w_in[:, 3 * d + 2 * H_G:].astype(BF16)
    return pl.pallas_call(
        _gdn_proj_kernel,
        grid=(b, t // tm),
        in_specs=[_rows_spec(tm, d)] + _mod_specs(layer, boff, d, (0, 1))
                 + [_const_spec((d, 3 * d)), _const_spec((d, LANES)), _const_spec((d, d))],
        out_specs=[_rows_spec(tm, 3 * d), _rows_spec(tm, LANES), _rows_spec(tm, d)],
        out_shape=[jax.ShapeDtypeStruct((b, t, 3 * d), F32), jax.ShapeDtypeStruct((b, t, LANES), F32),
                   jax.ShapeDtypeStruct((b, t, d), BF16)],
        compiler_params=_cparams("arbitrary", "arbitrary"),
        name="gdn_in_proj",
    )(x, mod4, mod4, wqkv, wba, wz)


def _unit_lower_inverse_minus_identity(mats):
    n = mats[0].shape[0]
    r = _iota((n, n), 0)
    c = _iota((n, n), 1)

    def mm(xs, ys):
        return [_dot(x.astype(BF16), y.astype(BF16)) for x, y in zip(xs, ys)]

    base = 8
    diag = _div_pow2(r, base) == _div_pow2(c, base)
    d = [jnp.where(diag, a, 0.0) for a in mats]
    d2 = mm(d, d)
    d4 = mm(d2, d2)
    nn = [-x for x in d]
    nn = [x + y + z for x, y, z in zip(nn, d2, mm(nn, d2))]
    nn = [x + y + z for x, y, z in zip(nn, d4, mm(nn, d4))]
    m = base
    while m < n:
        pair = (_div_pow2(r, 2 * m) == _div_pow2(c, 2 * m)) & (_div_pow2(r, m) != _div_pow2(c, m))
        off = [jnp.where(pair, a, 0.0) for a in mats]
        y = [o + p for o, p in zip(off, mm(nn, off))]
        x = [p + q for p, q in zip(y, mm(y, nn))]
        nn = [p - q for p, q in zip(nn, x)]
        m *= 2
    return nn


def _gdn_kernel(qkv_ref, ba_ref, z_ref, cbuf_ref, s0_ref, cw_ref, avec_ref, dtvec_ref, nw_ref,
                o_ref, s_ref, ext_ref, *, chunk):
    ti = pl.program_id(1)
    tb = qkv_ref.shape[0]
    d = z_ref.shape[-1]
    dk = d // H_G
    n_chunks = tb // chunk

    @pl.when(ti == 0)
    def _():
        ext_ref[0:SUBLANES, :] = cbuf_ref[...]
        s_ref[...] = s0_ref[...]

    ext_ref[SUBLANES:SUBLANES + tb, :] = qkv_ref[...]

    ri = _iota((chunk, chunk), 0)
    ci = _iota((chunk, chunk), 1)
    tri = (ri >= ci).astype(BF16)
    eye_l = (_iota((LANES, LANES), 0) == _iota((LANES, LANES), 1)).astype(BF16)
    incl = ri >= ci
    strict = ri > ci
    cw = cw_ref[...]
    neg_exp_a = -jnp.exp(avec_ref[...])
    dtv = dtvec_ref[...]
    nw = nw_ref[...]

    def chunk_body(cidx, carry):
        r0 = pl.multiple_of(cidx * chunk, chunk)
        ba = ba_ref[pl.ds(r0, chunk), :]
        beta_all = _sigmoid(ba)
        g_all = neg_exp_a * _softplus(ba + dtv)
        gcum = _dot_exact_l(tri, g_all)
        gcum_t = _dot_exact_nt(eye_l, gcum)
        def conv_slab(c0):
            win = ext_ref[pl.ds(r0, chunk + SUBLANES), c0:c0 + dk]
            acc = None
            for j in range(CONV_W):
                lo = SUBLANES - CONV_W + 1 + j
                term = cw[j:j + 1, c0:c0 + dk] * win[lo:lo + chunk, :]
                acc = term if acc is None else acc + term
            return _silu(acc)

        heads = range(H_G)
        q = [conv_slab(h * dk) for h in heads]
        k = [conv_slab(d + h * dk) for h in heads]
        v = [conv_slab(2 * d + h * dk) for h in heads]
        q = [x * lax.rsqrt(jnp.sum(x * x, axis=-1, keepdims=True) + NORM_EPS) * (dk ** -0.5) for x in q]
        k = [x * lax.rsqrt(jnp.sum(x * x, axis=-1, keepdims=True) + NORM_EPS) for x in k]
        beta = [beta_all[:, h:h + 1] for h in heads]
        gcol = [gcum[:, H_G + h:H_G + h + 1] for h in heads]
        grow = [gcum_t[H_G + h:H_G + h + 1, :] for h in heads]
        dec_incl = [jnp.exp(jnp.where(incl, gc - gr, NEG_INF)) for gc, gr in zip(gcol, grow)]
        kb = [x * bt for x, bt in zip(k, beta)]
        kbf = [x.astype(BF16) for x in k]
        a_mat = [_dot_nt(x.astype(BF16), y) for x, y in zip(kb, kbf)]
        qk = [_dot_nt(x.astype(BF16), y) for x, y in zip(q, kbf)]
        a_mat = [jnp.where(strict, x * e, 0.0) for x, e in zip(a_mat, dec_incl)]
        qk = [x * e for x, e in zip(qk, dec_incl)]
        exp_g = [jnp.exp(gc) for gc in gcol]
        rhs = [jnp.concatenate([x * bt, y * e], axis=1) for x, bt, y, e in zip(v, beta, kb, exp_g)]
        nn = _unit_lower_inverse_minus_identity(a_mat)
        sol = [x + _dot(y.astype(BF16), x.astype(BF16)) for x, y in zip(rhs, nn)]
        s = [s_ref[h] for h in heads]
        sb = [x.astype(BF16) for x in s]
        v_res = [x[:, :dk] - _dot(x[:, dk:].astype(BF16), y) for x, y in zip(sol, sb)]
        vrb = [x.astype(BF16) for x in v_res]
        o = [_dot((x * e).astype(BF16), y) for x, e, y in zip(q, exp_g, sb)]
        o = [x + _dot(y.astype(BF16), z) for x, y, z in zip(o, qk, vrb)]
        g_last = [gc[chunk - 1:chunk, :] for gc in gcol]
        k_dec = [(x * jnp.exp(gl - gc)).astype(BF16) for x, gl, gc in zip(k, g_last, gcol)]
        s_add = [_dot_tn(x, y) for x, y in zip(k_dec, vrb)]
        for h in heads:
            s_ref[h] = s[h] * jnp.exp(g_last[h]) + s_add[h]
            on = o[h] * lax.rsqrt(jnp.mean(o[h] * o[h], axis=-1, keepdims=True) + NORM_EPS) * nw
            zz = z_ref[pl.ds(r0, chunk), h * dk:(h + 1) * dk].astype(F32)
            o_ref[pl.ds(r0, chunk), h * dk:(h + 1) * dk] = (on * _silu(zz)).astype(BF16)
        return carry

    lax.fori_loop(0, n_chunks, chunk_body, 0)
    ext_ref[0:SUBLANES, :] = ext_ref[tb:tb + SUBLANES, :]


def _gdn_mix(qkv, ba, z, conv_buf, s0, conv_w, a_log, dt_bias, norm_w):
    b, t, d3 = qkv.shape
    d = d3 // 3
    dk = d // H_G
    chunk = min(GDN_CHUNK, t)
    tb = _row_tile(t, 4 * chunk)
    assert t >= CONV_W - 1 and tb >= SUBLANES
    cbuf = jnp.pad(conv_buf.astype(F32), ((0, 0), (SUBLANES - (CONV_W - 1), 0), (0, 0)))
    cw = jnp.pad(conv_w.astype(F32), ((0, SUBLANES - CONV_W), (0, 0)))
    avec = jnp.pad(a_log.astype(F32), (H_G, LANES - 2 * H_G)).reshape(1, LANES)
    dtvec = jnp.pad(dt_bias.astype(F32), (H_G, LANES - 2 * H_G)).reshape(1, LANES)
    nw = norm_w.astype(F32).reshape(1, dk)
    state_spec = pl.BlockSpec((None, H_G, dk, dk), lambda bb, i: (bb, 0, 0, 0))
    return pl.pallas_call(
        functools.partial(_gdn_kernel, chunk=chunk),
        grid=(b, t // tb),
        in_specs=[_rows_spec(tb, d3), _rows_spec(tb, LANES), _rows_spec(tb, d),
                  pl.BlockSpec((None, SUBLANES, d3), lambda bb, i: (bb, 0, 0)), state_spec,
                  _const_spec((SUBLANES, d3)), _const_spec((1, LANES)), _const_spec((1, LANES)),
                  _const_spec((1, dk))],
        out_specs=[_rows_spec(tb, d), state_spec],
        out_shape=[jax.ShapeDtypeStruct((b, t, d), BF16), jax.ShapeDtypeStruct((b, H_G, dk, dk), F32)],
        scratch_shapes=[pltpu.VMEM((tb + SUBLANES, d3), F32)],
        compiler_params=_cparams("arbitrary", "arbitrary"),
        name="gdn_mixer",
    )(qkv, ba, z, cbuf, s0.astype(F32), cw, avec, dtvec, nw)


def _out_proj_kernel(o_ref, x_ref, gate_ref, w_ref, g_ref, b_ref, y_ref):
    h = _dot(o_ref[...], w_ref[...])
    y = DEEPNORM_ALPHA * x_ref[...] + (1.0 + gate_ref[...]) * h
    mu = jnp.mean(y, axis=-1, keepdims=True)
    yc = y - mu
    var = jnp.mean(yc * yc, axis=-1, keepdims=True)
    y_ref[...] = yc * lax.rsqrt(var + LN_EPS) * g_ref[...] + b_ref[...]


def _out_proj(o, x, mod4, layer, boff, w_out, ln_g, ln_b):
    b, t, d = x.shape
    kdim = o.shape[-1]
    tm = _row_tile(t, 512)
    return pl.pallas_call(
        _out_proj_kernel,
        grid=(b, t // tm),
        in_specs=[_rows_spec(tm, kdim), _rows_spec(tm, d)] + _mod_specs(layer, boff, d, (2,))
                 + [_const_spec((kdim, d)), _const_spec((1, d)), _const_spec((1, d))],
        out_specs=_rows_spec(tm, d),
        out_shape=jax.ShapeDtypeStruct((b, t, d), F32),
        compiler_params=_cparams("arbitrary", "arbitrary"),
        name="out_proj_postnorm",
    )(o, x, mod4, w_out.astype(BF16), ln_g.reshape(1, d), ln_b.reshape(1, d))


def _qkvz_outputs(u, w_ref, q_ref, k32_ref, v32_ref, kb_ref, vb_ref, z_ref, q_scale, transposed):
    d = u.shape[-1]
    if transposed:
        q_ref[...] = (_dot_nt(w_ref[:, 0:d], u) * q_scale).astype(BF16)
    else:
        q_ref[...] = (_dot(u, w_ref[:, 0:d]) * q_scale).astype(BF16)
    k = _dot(u, w_ref[:, d:2 * d])
    k32_ref[...] = k
    kb_ref[...] = k.astype(BF16)
    v = _dot(u, w_ref[:, 2 * d:3 * d])
    v32_ref[...] = v
    vb_ref[...] = (v.T if transposed else v).astype(BF16)
    z_ref[...] = _dot(u, w_ref[:, 3 * d:4 * d]).astype(BF16)


def _qkvz_weights(w_in, d, transposed):
    w = w_in[:, :4 * d]
    if transposed:
        w = jnp.concatenate([w[:, :d].T, w[:, d:]], axis=1)
    return w.astype(BF16)


def _qkvz_specs(b, t, d, tm, transposed):
    cols_spec = pl.BlockSpec((None, d, tm), lambda bb, i: (bb, 0, i))
    rows = _rows_spec(tm, d)
    f32o = jax.ShapeDtypeStruct((b, t, d), F32)
    bf16o = jax.ShapeDtypeStruct((b, t, d), BF16)
    bf16t = jax.ShapeDtypeStruct((b, d, t), BF16)
    if transposed:
        return [cols_spec, rows, rows, rows, cols_spec, rows], [bf16t, f32o, f32o, bf16o, bf16t, bf16o]
    return [rows] * 6, [bf16o, f32o, f32o, bf16o, bf16o, bf16o]


def _fox_proj_kernel(x_ref, shift_ref, scale_ref, w_ref, wf_ref, bf_ref,
                     q_ref, k32_ref, v32_ref, kb_ref, vb_ref, z_ref, logf_ref, *, q_scale, transposed):
    u = _modulated(x_ref, shift_ref, scale_ref)
    _qkvz_outputs(u, w_ref, q_ref, k32_ref, v32_ref, kb_ref, vb_ref, z_ref, q_scale, transposed)
    f = _dot(u, wf_ref[...])[:, :H_F] + bf_ref[...]
    logf_ref[...] = -_softplus(-f)


def _fox_proj(x, mod4, layer, boff, w_in, b_f, transposed):
    b, t, d = x.shape
    tm = _row_tile(t, 256)
    w = _qkvz_weights(w_in, d, transposed)
    wf = jnp.pad(w_in[:, 4 * d:], ((0, 0), (0, LANES - H_F))).astype(BF16)
    out_specs, out_shape = _qkvz_specs(b, t, d, tm, transposed)
    return pl.pallas_call(
        functools.partial(_fox_proj_kernel, q_scale=(d // H_F) ** -0.5 * (LOG2E if transposed else 1.0),
                          transposed=transposed),
        grid=(b, t // tm),
        in_specs=[_rows_spec(tm, d)] + _mod_specs(layer, boff, d, (0, 1))
                 + [_const_spec((d, 4 * d)), _const_spec((d, LANES)), _const_spec((1, H_F))],
        out_specs=out_specs + [_rows_spec(tm, H_F)],
        out_shape=out_shape + [jax.ShapeDtypeStruct((b, t, H_F), F32)],
        compiler_params=_cparams("arbitrary", "arbitrary"),
        name="fox_in_proj",
    )(x, mod4, mod4, w, wf, b_f.astype(F32).reshape(1, H_F))


def _cumsum_kernel(x_ref, c0_ref, cn_ref, ct_ref, *, blk):
    s, h = x_ref.shape
    tri = (_iota((blk, blk), 0) >= _iota((blk, blk), 1)).astype(BF16)
    eye_h = (_iota((h, h), 0) == _iota((h, h), 1)).astype(BF16)
    carry = c0_ref[...]
    for i in range(s // blk):
        c = _dot_exact_l(tri, x_ref[i * blk:(i + 1) * blk, :]) + carry
        cn_ref[i * blk:(i + 1) * blk, :] = c
        ct_ref[:, i * blk:(i + 1) * blk] = _dot_exact_nt(eye_h, c)
        carry = c[blk - 1:blk, :]


def _cumsum_time(x, c0):
    b, s, h = x.shape
    blk = 256 if s % 256 == 0 else s
    return pl.pallas_call(
        functools.partial(_cumsum_kernel, blk=blk),
        grid=(b,),
        in_specs=[pl.BlockSpec((None, s, h), lambda bb: (bb, 0, 0)),
                  pl.BlockSpec((None, 1, h), lambda bb: (bb, 0, 0))],
        out_specs=[pl.BlockSpec((None, s, h), lambda bb: (bb, 0, 0)),
                   pl.BlockSpec((None, h, s), lambda bb: (bb, 0, 0))],
        out_shape=[jax.ShapeDtypeStruct((b, s, h), F32), jax.ShapeDtypeStruct((b, h, s), F32)],
        compiler_params=_cparams("arbitrary"),
        name="logf_cumsum",
    )(x, c0)


AUG = LANES // H_F


def _fox_aug_kernel(x_ref, kaug_ref, qaugt_ref, *, blk):
    s, h = x_ref.shape
    tri = (_iota((blk, blk), 0) >= _iota((blk, blk), 1)).astype(BF16)
    lane_h = _iota((h, LANES), 1)
    row_h = _iota((h, LANES), 0)
    ek = [jnp.where(lane_h == row_h * AUG + part, -1.0, 0.0).astype(BF16) for part in range(3)]
    row_q = _iota((LANES, h), 0)
    col_q = _iota((LANES, h), 1)
    eq = [jnp.where(row_q == col_q * AUG + 3 + part, 1.0, 0.0).astype(BF16) for part in range(3)]
    k_slot = _iota((blk, LANES), 1) & (AUG - 1)
    k_ones = jnp.where(k_slot >= 3, jnp.where(k_slot < 6, 1.0, 0.0), 0.0)
    q_ones = jnp.where((_iota((LANES, blk), 0) & (AUG - 1)) < 3, 1.0, 0.0)
    carry = jnp.zeros((1, h), F32)
    for i in range(s // blk):
        c = _dot_exact_l(tri, x_ref[i * blk:(i + 1) * blk, :]) + carry
        parts = _split3(c * LOG2E)
        kaug = k_ones
        qaugt = q_ones
        for part in range(3):
            kaug = kaug + _dot(parts[part], ek[part])
            qaugt = qaugt + _dot_nt(eq[part], parts[part])
        kaug_ref[i * blk:(i + 1) * blk, :] = kaug.astype(BF16)
        qaugt_ref[:, i * blk:(i + 1) * blk] = qaugt.astype(BF16)
        carry = c[blk - 1:blk, :]


def _fox_aug(logf):
    b, s, h = logf.shape
    assert h * AUG == LANES and AUG >= 6
    blk = 256 if s % 256 == 0 else s
    return pl.pallas_call(
        functools.partial(_fox_aug_kernel, blk=blk),
        grid=(b,),
        in_specs=[pl.BlockSpec((None, s, h), lambda bb: (bb, 0, 0))],
        out_specs=[pl.BlockSpec((None, s, LANES), lambda bb: (bb, 0, 0)),
                   pl.BlockSpec((None, LANES, s), lambda bb: (bb, 0, 0))],
        out_shape=[jax.ShapeDtypeStruct((b, s, LANES), BF16), jax.ShapeDtypeStruct((b, LANES, s), BF16)],
        compiler_params=_cparams("arbitrary"),
        name="fox_bias_operands",
    )(logf)


SUM_ROWS = 16


def _softmax_t_step(s, vt, m_ref, acc_ref):
    m_prev = m_ref[...]
    m_new = jnp.maximum(m_prev, jnp.max(s, axis=0, keepdims=True))
    alpha = jnp.exp2(m_prev - m_new)
    p = jnp.exp2(s - m_new).astype(BF16)
    vt_ext = jnp.concatenate([vt, jnp.ones((SUM_ROWS, vt.shape[1]), BF16)], axis=0)
    acc_ref[...] = alpha * acc_ref[...] + _dot(vt_ext, p)
    m_ref[...] = m_new


def _softmax_t_result(acc_ref, dv):
    return acc_ref[0:dv, :] / acc_ref[dv:dv + 1, :]


def _fox_attn_kernel(qt_ref, k_ref, kaug_ref, vt_ref, qaugt_ref, z_ref, o_ref,
                     qcat_ref, s_ref, m_ref, acc_ref, *, tk):
    hp = pl.program_id(1)
    qi = pl.program_id(2)
    tq = qt_ref.shape[1]
    assert tq == 2 * tk
    dh = LANES // 2
    row = _iota((LANES, tq), 0)
    qt = qt_ref[...].astype(F32)
    qa = qaugt_ref[...].astype(F32)
    for hh in range(2):
        qcat_ref[hh, 0:LANES, :] = jnp.where(_div_pow2(row, dh) == hh, qt, 0.0).astype(BF16)
        qcat_ref[hh, LANES:2 * LANES, :] = jnp.where(_div_pow2(row, AUG) == hp * 2 + hh, qa, 0.0).astype(BF16)
    m_ref[...] = jnp.full(m_ref.shape, NEG_INF, F32)
    acc_ref[...] = jnp.zeros(acc_ref.shape, F32)
    n_full = qi * 2
    key_in = _iota((tk, tq), 0)
    qry_pos = qi * tq + _iota((tk, tq), 1)

    def scores(j, slot):
        k0 = pl.multiple_of(j * tk, tk)
        kcat = jnp.concatenate([k_ref[pl.ds(k0, tk), :], kaug_ref[pl.ds(k0, tk), :]], axis=1)
        for hh in range(2):
            s_ref[slot, hh] = _dot(kcat, qcat_ref[hh])

    def consume(j, slot, masked):
        k0 = pl.multiple_of(j * tk, tk)
        for hh in range(2):
            s = s_ref[slot, hh]
            if masked:
                s = jnp.where(k0 + key_in <= qry_pos, s, NEG_INF)
            _softmax_t_step(s, vt_ref[hh * dh:(hh + 1) * dh, pl.ds(k0, tk)], m_ref.at[hh], acc_ref.at[hh])

    scores(0, 0)

    def body(i, carry):
        a = 2 * i
        scores(a + 1, 1)
        consume(a, 0, False)
        scores(a + 2, 0)
        consume(a + 1, 1, False)
        return carry

    lax.fori_loop(0, qi, body, 0)
    scores(n_full + 1, 1)
    consume(n_full, 0, True)
    consume(n_full + 1, 1, True)
    o_t = jnp.concatenate([_softmax_t_result(acc_ref.at[hh], dh) for hh in range(2)], axis=0)
    o_ref[...] = (o_t.T * _silu(z_ref[...].astype(F32))).astype(BF16)


def _fox_attn_prompt(qt, kb, kaug, vt, qaugt, z):
    b, d, t = qt.shape
    tq, tk = ATTN_TQ, ATTN_TK
    assert t % tq == 0
    hpairs = d // LANES
    dh = LANES // 2
    return pl.pallas_call(
        functools.partial(_fox_attn_kernel, tk=tk),
        grid=(b, hpairs, t // tq),
        in_specs=[pl.BlockSpec((None, LANES, tq), lambda bb, hp, i: (bb, hp, i)),
                  pl.BlockSpec((None, t, LANES), lambda bb, hp, i: (bb, 0, hp)),
                  pl.BlockSpec((None, t, LANES), lambda bb, hp, i: (bb, 0, 0)),
                  pl.BlockSpec((None, LANES, t), lambda bb, hp, i: (bb, hp, 0)),
                  pl.BlockSpec((None, LANES, tq), lambda bb, hp, i: (bb, 0, i)),
                  pl.BlockSpec((None, tq, LANES), lambda bb, hp, i: (bb, i, hp))],
        out_specs=pl.BlockSpec((None, tq, LANES), lambda bb, hp, i: (bb, i, hp)),
        out_shape=jax.ShapeDtypeStruct((b, t, d), BF16),
        scratch_shapes=[pltpu.VMEM((2, 2 * LANES, tq), BF16), pltpu.VMEM((2, 2, tk, tq), F32),
                        pltpu.VMEM((2, 1, tq), F32), pltpu.VMEM((2, dh + SUM_ROWS, tq), F32)],
        compiler_params=_cparams("arbitrary", "arbitrary", "arbitrary"),
        name="fox_attention_prompt",
    )(qt, kb, kaug, vt, qaugt, z)


def _fox_decode_kernel(q_ref, kp_ref, vp_ref, kn_ref, vn_ref, z_ref, cq_ref, ckt_ref, o_ref):
    hp = pl.program_id(1)
    t = q_ref.shape[0]
    p_len = kp_ref.shape[0]
    dh = LANES // 2
    lane = _iota((t, LANES), 1)
    q = q_ref[...]
    kp = kp_ref[...].astype(BF16)
    vp = vp_ref[...].astype(BF16)
    kn = kn_ref[...]
    vn = vn_ref[...]
    cq_all = cq_ref[...]
    hlane = _iota(cq_all.shape, 1)
    causal = _iota((t, t), 1) <= _iota((t, t), 0)
    outs = []
    for hh in range(2):
        h = hp * 2 + hh
        qm = jnp.where(_div_pow2(lane, dh) == hh, q, jnp.zeros_like(q))
        cq = jnp.sum(jnp.where(hlane == h, cq_all, 0.0), axis=-1, keepdims=True)
        ck = ckt_ref[pl.ds(h, 1), :]
        s_p = _dot_nt(qm, kp) + cq - ck[:, :p_len]
        s_n = jnp.where(causal, _dot_nt(qm, kn) + cq - ck[:, p_len:], NEG_INF)
        m = jnp.maximum(jnp.max(s_p, axis=-1, keepdims=True), jnp.max(s_n, axis=-1, keepdims=True))
        e_p = jnp.exp(s_p - m)
        e_n = jnp.exp(s_n - m)
        den = jnp.sum(e_p, axis=-1, keepdims=True) + jnp.sum(e_n, axis=-1, keepdims=True)
        outs.append((_dot(e_p.astype(BF16), vp) + _dot(e_n.astype(BF16), vn)) / den)
    o = jnp.where(lane < dh,outs[0], outs[1])
    o_ref[...] = (o * _silu(z_ref[...].astype(F32))).astype(BF16)


def _fox_attn_sample(q, k_past, v_past, kb, vb, z, cq_new, cum_t):
    b, t, d = q.shape
    p_len = k_past.shape[1]
    hpairs = d // LANES
    new_spec = pl.BlockSpec((None, t, LANES), lambda bb, hp: (bb, 0, hp))
    past_spec = pl.BlockSpec((None, p_len, LANES), lambda bb, hp: (bb, 0, hp))
    return pl.pallas_call(
        _fox_decode_kernel,
        grid=(b, hpairs),
        in_specs=[new_spec, past_spec, past_spec, new_spec, new_spec, new_spec,
                  pl.BlockSpec((None, t, H_F), lambda bb, hp: (bb, 0, 0)),
                  pl.BlockSpec((None, H_F, p_len + t), lambda bb, hp: (bb, 0, 0))],
        out_specs=new_spec,
        out_shape=jax.ShapeDtypeStruct((b, t, d), BF16),
        compiler_params=_cparams("arbitrary", "arbitrary"),
        name="fox_attention_sample",
    )(q, k_past, v_past, kb, vb, z, cq_new, cum_t)


def _diff_proj_kernel(x_ref, shift_ref, scale_ref, w_ref, q_ref, k32_ref, v32_ref, kb_ref, vb_ref, z_ref,
                      *, q_scale, transposed):
    u = _modulated(x_ref, shift_ref, scale_ref)
    _qkvz_outputs(u, w_ref, q_ref, k32_ref, v32_ref, kb_ref, vb_ref, z_ref, q_scale, transposed)


def _diff_proj(x, mod4, layer, boff, w_in, transposed):
    b, t, d = x.shape
    tm = _row_tile(t, 256)
    out_specs, out_shape = _qkvz_specs(b, t, d, tm, transposed)
    return pl.pallas_call(
        functools.partial(_diff_proj_kernel, q_scale=(d // (2 * H_D)) ** -0.5 * (LOG2E if transposed else 1.0),
                          transposed=transposed),
        grid=(b, t // tm),
        in_specs=[_rows_spec(tm, d)] + _mod_specs(layer, boff, d, (0, 1)) + [_const_spec((d, 4 * d))],
        out_specs=out_specs,
        out_shape=out_shape,
        compiler_params=_cparams("arbitrary", "arbitrary"),
        name="diff_in_proj",
    )(x, mod4, mod4, _qkvz_weights(w_in, d, transposed))


def _t5_thresholds():
    nb = N_BUCKETS // 2
    max_exact = nb // 2
    steps = nb - max_exact
    ratio = MAX_DISTANCE // max_exact
    out = []
    for kk in range(1, nb - max_exact):
        target = max_exact ** steps * ratio ** kk
        n = max_exact
        while n ** steps < target:
            n += 1
        out.append(n)
    return nb, max_exact, out


def _bias_kernel(tbl_ref, o_ref, *, q0, k0, keys_on_rows):
    h = pl.program_id(0)
    shape = o_ref.shape
    kdim, qdim = (0, 1) if keys_on_rows else (1, 0)
    rel = (k0 + _iota(shape, kdim)) - (q0 + _iota(shape, qdim))
    nb, max_exact, thr = _t5_thresholds()
    n = jnp.abs(rel)
    large = jnp.full(shape, max_exact, jnp.int32)
    for tval in thr:
        large = large + (n >= tval).astype(jnp.int32)
    bucket = jnp.where(rel > 0, nb, 0) + jnp.where(n < max_exact, n, large)
    acc = jnp.zeros(shape, F32)
    for bkt in range(N_BUCKETS):
        acc = jnp.where(bucket == bkt, tbl_ref[bkt * H_D + h], acc)
    if keys_on_rows:
        acc = (acc - tbl_ref[(nb - 1) * H_D + h]) * LOG2E
    o_ref[...] = acc


def _bias_tile(rel_table, q0, nq, k0, nk, keys_on_rows=False):
    shape = (nk, nq) if keys_on_rows else (nq, nk)
    return pl.pallas_call(
        functools.partial(_bias_kernel, q0=q0, k0=k0, keys_on_rows=keys_on_rows),
        grid=(H_D,),
        in_specs=[pl.BlockSpec(memory_space=pltpu.SMEM)],
        out_specs=pl.BlockSpec((None,) + shape, lambda h: (h, 0, 0)),
        out_shape=jax.ShapeDtypeStruct((H_D,) + shape, F32),
        compiler_params=_cparams("arbitrary"),
        name="t5_bias_tile",
    )(rel_table.astype(F32).reshape(N_BUCKETS * H_D))


def _diff_lambda(lam_ref):
    lam = lam_ref[...]
    s1 = jnp.sum(lam[0:1, :] * lam[1:2, :], axis=-1, keepdims=True)
    s2 = jnp.sum(lam[2:3, :] * lam[3:4, :], axis=-1, keepdims=True)
    return jnp.exp(s1) - jnp.exp(s2) + LAMBDA_INIT


def _diff_epilogue(o, z_ref, subln_ref, o_ref):
    on = o * lax.rsqrt(jnp.mean(o * o, axis=-1, keepdims=True) + NORM_EPS) * subln_ref[...]
    on = on * (1.0 - LAMBDA_INIT)
    o_ref[...] = (on * _silu(z_ref[...].astype(F32))).astype(BF16)


def _diff_attn_kernel(tbl_ref, qt_ref, k_ref, vt_ref, z_ref, biasm_ref, bias0_ref, bias1_ref, lam_ref, subln_ref,
                      o_ref, qcat_ref, s_ref, m_ref, acc_ref, *, tk):
    h = pl.program_id(1)
    qi = pl.program_id(2)
    tq = qt_ref.shape[1]
    assert tq == 2 * tk
    dh = LANES // 2
    row = _iota((LANES, tq), 0)
    qt = qt_ref[...].astype(F32)
    nb, _, _ = _t5_thresholds()
    far = _split3(jnp.full((LANES, tq), tbl_ref[(nb - 1) * H_D + h], F32) * LOG2E)
    far_rows = jnp.zeros((LANES, tq), F32)
    for part in range(3):
        far_rows = jnp.where(row == part, far[part].astype(F32), far_rows)
    for br in range(2):
        qcat_ref[br, 0:LANES, :] = jnp.where(_div_pow2(row, dh) == br, qt, 0.0).astype(BF16)
        qcat_ref[br, LANES:2 * LANES, :] = far_rows.astype(BF16)
    ones_aug = jnp.where(_iota((tk, LANES), 1) < 3, 1.0, 0.0).astype(BF16)
    m_ref[...] = jnp.full(m_ref.shape, NEG_INF, F32)
    acc_ref[...] = jnp.zeros(acc_ref.shape, F32)
    key_in = _iota((tk, tq), 0)
    qry_chunk = _div_pow2(_iota((tk, tq), 1), ATTN_CHUNK)

    def scores(j, slot):
        k0 = pl.multiple_of(j * tk, tk)
        kcat = jnp.concatenate([k_ref[pl.ds(k0, tk), :], ones_aug], axis=1)
        for br in range(2):
            s_ref[slot, br] = _dot(kcat, qcat_ref[br])

    def consume(j, slot, bias_ref=None, key_off=None):
        k0 = pl.multiple_of(j * tk, tk)
        for br in range(2):
            s = s_ref[slot, br]
            if bias_ref is not None:
                s = s + bias_ref[...]
            if key_off is not None:
                s = jnp.where(_div_pow2(key_off + key_in, ATTN_CHUNK) <= qry_chunk, s, NEG_INF)
            _softmax_t_step(s, vt_ref[:, pl.ds(k0, tk)], m_ref.at[br], acc_ref.at[br])

    scores(0, 0)

    def body(i, carry):
        a = 2 * i
        scores(a + 1, 1)
        consume(a, 0)
        scores(a + 2, 0)
        consume(a + 1, 1)
        return carry

    lax.fori_loop(0, jnp.maximum(qi - 1, 0), body, 0)

    @pl.when(qi >= 1)
    def _():
        a = 2 * qi - 2
        scores(a + 1, 1)
        consume(a, 0)
        scores(a + 2, 0)
        consume(a + 1, 1, bias_ref=biasm_ref)

    a = 2 * qi
    scores(a + 1, 1)
    consume(a, 0, bias_ref=bias0_ref, key_off=0)
    consume(a + 1, 1, bias_ref=bias1_ref, key_off=tk)
    o_t = (_softmax_t_result(acc_ref.at[0], LANES)
           - _diff_lambda(lam_ref) * _softmax_t_result(acc_ref.at[1], LANES))
    _diff_epilogue(o_t.T, z_ref, subln_ref, o_ref)


def _lam_pack(lam_q1, lam_k1, lam_q2, lam_k2):
    rows = jnp.stack([lam_q1, lam_k1, lam_q2, lam_k2]).astype(F32)
    return jnp.pad(rows, ((0, SUBLANES - 4), (0, LANES - rows.shape[1])))


def _diff_attn_prompt(qt, kb, vt, z, rel_table, lam, subln_w):
    b, d, t = qt.shape
    tq, tk = ATTN_TQ, ATTN_TK
    assert t % tq == 0 and tk % ATTN_CHUNK == 0 and tk >= MAX_DISTANCE
    biasm = _bias_tile(rel_table, tk, tq, 0, tk, keys_on_rows=True)
    bias0 = _bias_tile(rel_table, 0, tq, 0, tk, keys_on_rows=True)
    bias1 = _bias_tile(rel_table, 0, tq, tk, tk, keys_on_rows=True)
    rows_spec = pl.BlockSpec((None, tq, LANES), lambda bb, h, i: (bb, i, h))
    bias_spec = pl.BlockSpec((None, tk, tq), lambda bb, h, i: (h, 0, 0))
    return pl.pallas_call(
        functools.partial(_diff_attn_kernel, tk=tk),
        grid=(b, H_D, t // tq),
        in_specs=[pl.BlockSpec(memory_space=pltpu.SMEM),
                  pl.BlockSpec((None, LANES, tq), lambda bb, h, i: (bb, h, i)),
                  pl.BlockSpec((None, t, LANES), lambda bb, h, i: (bb, 0, h)),
                  pl.BlockSpec((None, LANES, t), lambda bb, h, i: (bb, h, 0)),
                  rows_spec, bias_spec, bias_spec, bias_spec,
                  pl.BlockSpec((SUBLANES, LANES), lambda bb, h, i: (0, 0)),
                  pl.BlockSpec((1, LANES), lambda bb, h, i: (0, 0))],
        out_specs=rows_spec,
        out_shape=jax.ShapeDtypeStruct((b, t, d), BF16),
        scratch_shapes=[pltpu.VMEM((2, 2 * LANES, tq), BF16), pltpu.VMEM((2, 2, tk, tq), F32),
                        pltpu.VMEM((2, 1, tq), F32), pltpu.VMEM((2, LANES + SUM_ROWS, tq), F32)],
        compiler_params=_cparams("arbitrary", "arbitrary", "arbitrary"),
        name="diff_attention_prompt",
    )(rel_table.astype(F32).reshape(N_BUCKETS * H_D), qt, kb, vt, z, biasm, bias0, bias1, lam,
      subln_w.astype(F32).reshape(1, LANES))


def _diff_decode_kernel(q_ref, kp_ref, vp_ref, kn_ref, vn_ref, z_ref, bias_ref, lam_ref, subln_ref, o_ref,
                        *, p_len):
    t = q_ref.shape[0]
    dh = LANES // 2
    lane = _iota((t, LANES), 1)
    q = q_ref[...]
    kp = kp_ref[...].astype(BF16)
    vp = vp_ref[...].astype(BF16)
    kn = kn_ref[...]
    vn = vn_ref[...]
    bias = bias_ref[...]
    q_chunk = _div_pow2(p_len + _iota((t, t), 0), ATTN_CHUNK)
    kn_chunk = _div_pow2(p_len + _iota((t, t), 1), ATTN_CHUNK)
    kp_chunk = _div_pow2(_iota((t, p_len), 1), ATTN_CHUNK)
    qp_chunk = _div_pow2(p_len + _iota((t, p_len), 0), ATTN_CHUNK)
    outs = []
    for br in range(2):
        qm = jnp.where(_div_pow2(lane, dh) == br, q, jnp.zeros_like(q))
        s_p = jnp.where(kp_chunk <= qp_chunk, _dot_nt(qm, kp) + bias[:, :p_len], NEG_INF)
        s_n = jnp.where(kn_chunk <= q_chunk, _dot_nt(qm, kn) + bias[:, p_len:], NEG_INF)
        m = jnp.maximum(jnp.max(s_p, axis=-1, keepdims=True), jnp.max(s_n, axis=-1, keepdims=True))
        e_p = jnp.exp(s_p - m)
        e_n = jnp.exp(s_n - m)
        den = jnp.sum(e_p, axis=-1, keepdims=True) + jnp.sum(e_n, axis=-1, keepdims=True)
        outs.append((_dot(e_p.astype(BF16), vp) + _dot(e_n.astype(BF16), vn)) / den)
    o = outs[0] - _diff_lambda(lam_ref) * outs[1]
    _diff_epilogue(o, z_ref, subln_ref, o_ref)


def _diff_attn_sample(q, k_past, v_past, kb, vb, z, rel_table, lam, subln_w):
    b, t, d = q.shape
    p_len = k_past.shape[1]
    bias = _bias_tile(rel_table, p_len, t, 0, p_len + t)
    new_spec = pl.BlockSpec((None, t, LANES), lambda bb, h: (bb, 0, h))
    past_spec = pl.BlockSpec((None, p_len, LANES), lambda bb, h: (bb, 0, h))
    return pl.pallas_call(
        functools.partial(_diff_decode_kernel, p_len=p_len),
        grid=(b, H_D),
        in_specs=[new_spec, past_spec, past_spec, new_spec, new_spec, new_spec,
                  pl.BlockSpec((None, t, p_len + t), lambda bb, h: (h, 0, 0)),
                  pl.BlockSpec((SUBLANES, LANES), lambda bb, h: (0, 0)),
                  pl.BlockSpec((1, LANES), lambda bb, h: (0, 0))],
        out_specs=new_spec,
        out_shape=jax.ShapeDtypeStruct((b, t, d), BF16),
        compiler_params=_cparams("arbitrary", "arbitrary"),
        name="diff_attention_sample",
    )(q, k_past, v_past, kb, vb, z, bias, lam, subln_w.astype(F32).reshape(1, LANES))


def _rope_kernel(inv_ref, cos_ref, sin_ref, *, start):
    t, w = cos_ref.shape
    pos = (start + pl.program_id(0) * t + _iota((t, w), 0)).astype(F32)
    ang = pos * inv_ref[...]
    even = (_iota((t, w), 1) & 1) == 0
    cos_ref[...] = jnp.cos(ang)
    sn = jnp.sin(ang)
    sin_ref[...] = jnp.where(even, -sn, sn)


def _rope_tables(t, start, dk):
    inv_half = np.power(np.float32(ROPE_BASE), -np.arange(0, dk, 2, dtype=np.float32) / np.float32(dk))
    inv = jnp.asarray(np.repeat(inv_half.astype(np.float32), 2).reshape(1, dk))
    tt = _row_tile(t, 512)
    return pl.pallas_call(
        functools.partial(_rope_kernel, start=start),
        grid=(t // tt,),
        in_specs=[pl.BlockSpec((1, dk), lambda i: (0, 0))],
        out_specs=[pl.BlockSpec((tt, dk), lambda i: (i, 0))] * 2,
        out_shape=[jax.ShapeDtypeStruct((t, dk), F32)] * 2,
        compiler_params=_cparams("arbitrary"),
        name="rope_tables",
    )(inv)


def _rotate_pairs(x, cos, sin_signed):
    slabs = []
    for c0 in range(0, x.shape[-1], LANES):
        xs = x[:, c0:c0 + LANES]
        even = (_iota(xs.shape, 1) & 1) == 0
        slabs.append(jnp.where(even, pltpu.roll(xs, LANES - 1, 1), pltpu.roll(xs, 1, 1)))
    return x * cos + jnp.concatenate(slabs, axis=1) * sin_signed


def _ret_proj_kernel(x_ref, shift_ref, scale_ref, w_ref, cos_ref, sin_ref, q_ref, k_ref, v_ref, z_ref,
                     *, q_scale):
    u = _modulated(x_ref, shift_ref, scale_ref)
    d = x_ref.shape[-1]
    dk = cos_ref.shape[-1]
    cos = cos_ref[...]
    sn = sin_ref[...]
    for h in range(d // dk):
        qh = _dot(u, w_ref[:, h * dk:(h + 1) * dk])
        q_ref[:, h * dk:(h + 1) * dk] = (_rotate_pairs(qh, cos, sn) * q_scale).astype(BF16)
        kh = _dot(u, w_ref[:, d + h * dk:d + (h + 1) * dk])
        k_ref[:, h * dk:(h + 1) * dk] = _rotate_pairs(kh, cos, sn).astype(BF16)
    for s in range(2):
        v_ref[:, s * d:(s + 1) * d] = _dot(u, w_ref[:, (2 + s) * d:(3 + s) * d]).astype(BF16)
        z_ref[:, s * d:(s + 1) * d] = _dot(u, w_ref[:, (4 + s) * d:(5 + s) * d]).astype(BF16)


def _ret_proj(x, mod4, layer, boff, w_in, cos, sin_signed):
    b, t, d = x.shape
    dk = d // H_R
    tm = _row_tile(t, 256)
    tab_spec = pl.BlockSpec((tm, dk), lambda bb, i: (i, 0))
    return pl.pallas_call(
        functools.partial(_ret_proj_kernel, q_scale=dk ** -0.5),
        grid=(b, t // tm),
        in_specs=[_rows_spec(tm, d)] + _mod_specs(layer, boff, d, (0, 1)) + [_const_spec((d, 6 * d)), tab_spec, tab_spec],
        out_specs=[_rows_spec(tm, d), _rows_spec(tm, d), _rows_spec(tm, 2 * d), _rows_spec(tm, 2 * d)],
        out_shape=[jax.ShapeDtypeStruct((b, t, d), BF16), jax.ShapeDtypeStruct((b, t, d), BF16),
                   jax.ShapeDtypeStruct((b, t, 2 * d), BF16), jax.ShapeDtypeStruct((b, t, 2 * d), BF16)],
        compiler_params=_cparams("arbitrary", "arbitrary"),
        name="ret_in_proj",
    )(x, mod4, mod4, w_in.astype(BF16), cos, sin_signed)


def _ret_kernel(q_ref, k_ref, v_ref, z_ref, s0_ref, gn_ref, o_ref, s_ref):
    ti = pl.program_id(1)
    lr = q_ref.shape[0]
    dk = q_ref.shape[-1] // H_R
    dv = v_ref.shape[-1] // H_R

    @pl.when(ti == 0)
    def _():
        s_ref[...] = s0_ref[...]

    rel = (_iota((lr, lr), 0) - _iota((lr, lr), 1)).astype(F32)
    idx = _iota((lr, 1), 0).astype(F32)
    for h in range(H_R):
        log_gamma = math.log1p(-(2.0 ** (-5.0 - h)))
        intra = jnp.where(rel >= 0, jnp.exp(log_gamma * jnp.maximum(rel, 0.0)), 0.0)
        q_dec = jnp.exp(log_gamma * (idx + 1.0))
        k_dec = jnp.exp(log_gamma * (lr - 1.0 - idx))
        c_dec = math.exp(log_gamma * lr)
        qh = q_ref[:, h * dk:(h + 1) * dk]
        kh = k_ref[:, h * dk:(h + 1) * dk]
        vh = v_ref[:, h * dv:(h + 1) * dv]
        s = s_ref[h]
        att = _dot_nt(qh, kh) * intra
        o = _dot(att.astype(BF16), vh) + _dot(qh, s.astype(BF16)) * q_dec
        s_ref[h] = s * c_dec + _dot_tn((kh.astype(F32) * k_dec).astype(BF16), vh)
        mu = jnp.mean(o, axis=-1, keepdims=True)
        oc = o - mu
        var = jnp.mean(oc * oc, axis=-1, keepdims=True)
        on = oc * lax.rsqrt(var + LN_EPS) * gn_ref[:, h * dv:(h + 1) * dv]
        zz = z_ref[:, h * dv:(h + 1) * dv].astype(F32)
        o_ref[:, h * dv:(h + 1) * dv] = (on * _silu(zz)).astype(BF16)


def _ret_mix(q, k, v, z, s0, gn_w):
    b, t, d = q.shape
    dk = d // H_R
    dv = v.shape[-1] // H_R
    lr = _row_tile(t, 256)
    state_spec = pl.BlockSpec((None, H_R, dk, dv), lambda bb, i: (bb, 0, 0, 0))
    return pl.pallas_call(
        _ret_kernel,
        grid=(b, t // lr),
        in_specs=[_rows_spec(lr, d), _rows_spec(lr, d), _rows_spec(lr, 2 * d), _rows_spec(lr, 2 * d), state_spec,
                  _const_spec((1, 2 * d))],
        out_specs=[_rows_spec(lr, 2 * d), state_spec],
        out_shape=[jax.ShapeDtypeStruct((b, t, 2 * d), BF16), jax.ShapeDtypeStruct((b, H_R, dk, dv), F32)],
        compiler_params=_cparams("arbitrary", "arbitrary"),
        name="retention_mixer",
    )(q, k, v, z, s0.astype(F32), gn_w.astype(F32).reshape(1, 2 * d))


def _run_group(x, mod4, boff, state_gdn, state_gdn_conv, cache_fox_k, cache_fox_v, cache_fox_logf,
               cache_diff_k, cache_diff_v, state_ret, start, p):
    b, t, d = x.shape
    dk_g = d // H_G

    qkv, ba, z = _gdn_proj(x, mod4, 0, boff, p["gdn_w_in"])
    if state_gdn is None:
        state_gdn = jnp.zeros((b, H_G, dk_g, dk_g), F32)
        state_gdn_conv = jnp.zeros((b, CONV_W - 1, 3 * d), F32)
    o, gdn_state = _gdn_mix(qkv, ba, z, state_gdn_conv, state_gdn, p["gdn_conv_w"], p["gdn_a_log"],
                            p["gdn_dt_bias"], p["gdn_norm_w"])
    gdn_conv = qkv[:, t - (CONV_W - 1):, :]
    x = _out_proj(o, x, mod4, 0, boff, p["gdn_w_out"], p["ln_g"][0], p["ln_b"][0])

    prompt = cache_fox_k is None
    q, k32, v32, kb, vb, z, logf = _fox_proj(x, mod4, 1, boff, p["fox_w_in"], p["fox_b_f"], transposed=prompt)
    if prompt:
        kaug, qaugt = _fox_aug(logf)
        o = _fox_attn_prompt(q, kb, kaug, vb, qaugt, z)
    else:
        zero_c = jnp.zeros((b, 1, H_F), F32)
        p_len = cache_fox_k.shape[1]
        cum_pn, cum_pt = _cumsum_time(cache_fox_logf.astype(F32), zero_c)
        cum_n, cum_nt = _cumsum_time(logf, cum_pn[:, p_len - 1:, :])
        o = _fox_attn_sample(q, cache_fox_k.reshape(b, p_len, d), cache_fox_v.reshape(b, p_len, d), kb, vb, z,
                             cum_n, jnp.concatenate([cum_pt, cum_nt], axis=2))
    fox_k = k32.reshape(b, t, H_F, d // H_F)
    fox_v = v32.reshape(b, t, H_F, d // H_F)
    x = _out_proj(o, x, mod4, 1, boff, p["fox_w_out"], p["ln_g"][1], p["ln_b"][1])

    q, k32, v32, kb, vb, z = _diff_proj(x, mod4, 2, boff, p["diff_w_in"], transposed=prompt)
    lam = _lam_pack(p["diff_lam_q1"], p["diff_lam_k1"], p["diff_lam_q2"], p["diff_lam_k2"])
    if prompt:
        o = _diff_attn_prompt(q, kb, vb, z, p["rel_bias_table"], lam, p["diff_subln_w"])
    else:
        p_len = cache_diff_k.shape[1]
        o = _diff_attn_sample(q, cache_diff_k.reshape(b, p_len, d), cache_diff_v.reshape(b, p_len, d), kb, vb, z,
                              p["rel_bias_table"], lam, p["diff_subln_w"])
    diff_k = k32.reshape(b, t, H_D, 2, d // (2 * H_D))
    diff_v = v32.reshape(b, t, H_D, d // H_D)
    x = _out_proj(o, x, mod4, 2, boff, p["diff_w_out"], p["ln_g"][2], p["ln_b"][2])

    dk_r = d // H_R
    cos, sin_signed = _rope_tables(t, start, dk_r)
    q, k, v, z = _ret_proj(x, mod4, 3, boff, p["ret_w_in"], cos, sin_signed)
    if state_ret is None:
        state_ret = jnp.zeros((b, H_R, dk_r, 2 * d // H_R), F32)
    o, ret_state = _ret_mix(q, k, v, z, state_ret, p["ret_gn_w"])
    x = _out_proj(o, x, mod4, 3, boff, p["ret_w_out"], p["ln_g"][3], p["ln_b"][3])

    return x, gdn_state, gdn_conv, fox_k, fox_v, logf, diff_k, diff_v, ret_state


def kernel(x_prompt, x_sample, c_prompt, c_sample, state_gdn, state_gdn_conv, cache_fox_k, cache_fox_v, cache_fox_logf, cache_diff_k, cache_diff_v, state_ret, ada_w, ada_b, ln_g, ln_b, gdn_w_in, gdn_conv_w, gdn_a_log, gdn_dt_bias, gdn_norm_w, gdn_w_out, fox_w_in, fox_b_f, fox_w_out, rel_bias_table, diff_w_in, diff_lam_q1, diff_lam_k1, diff_lam_q2, diff_lam_k2, diff_subln_w, diff_w_out, ret_w_in, ret_gn_w, ret_w_out):
    p = dict(ln_g=ln_g, ln_b=ln_b, gdn_w_in=gdn_w_in, gdn_conv_w=gdn_conv_w, gdn_a_log=gdn_a_log,
             gdn_dt_bias=gdn_dt_bias, gdn_norm_w=gdn_norm_w, gdn_w_out=gdn_w_out, fox_w_in=fox_w_in,
             fox_b_f=fox_b_f, fox_w_out=fox_w_out, rel_bias_table=rel_bias_table, diff_w_in=diff_w_in,
             diff_lam_q1=diff_lam_q1, diff_lam_k1=diff_lam_k1, diff_lam_q2=diff_lam_q2, diff_lam_k2=diff_lam_k2,
             diff_subln_w=diff_subln_w, diff_w_out=diff_w_out, ret_w_in=ret_w_in, ret_gn_w=ret_gn_w,
             ret_w_out=ret_w_out)
    bp = x_prompt.shape[0]
    d = x_prompt.shape[-1]
    mod = _modulation(jnp.concatenate([c_prompt, c_sample], axis=0), ada_w, ada_b)
    mod4 = mod.reshape(mod.shape[0], mod.shape[1], 1, 3 * d)
    outs_p = _run_group(x_prompt, mod4, 0, None, None, None, None, None, None, None, None, 0, p)
    outs_s = _run_group(x_sample, mod4, bp, state_gdn, state_gdn_conv, cache_fox_k, cache_fox_v, cache_fox_logf,
                        cache_diff_k, cache_diff_v, state_ret, cache_fox_k.shape[1], p)
    return (outs_p[0], outs_s[0]) + tuple(outs_p[1:]) + tuple(outs_s[1:])
```

```python
import functools
import math

import numpy as np
import jax
import jax.numpy as jnp
from jax import lax
from jax.experimental import pallas as pl
from jax.experimental.pallas import tpu as pltpu

F32 = jnp.float32
BF16 = jnp.bfloat16

DEPTH = 4
ATTN_TQ = 512
ATTN_TK = 256
GDN_CHUNK = 64
ATTN_CHUNK = 64
DEEPNORM_ALPHA = (2.0 * DEPTH) ** 0.25
LN_EPS = 1e-5
NORM_EPS = 1e-6
NEG_INF = -1e30
LOG2E = math.log2(math.e)
H_G, H_F, H_D, H_R = 8, 16, 8, 4
CONV_W = 4
DIFF_LAYER = 2
LAMBDA_INIT = 0.8 - 0.6 * math.exp(-0.3 * DIFF_LAYER)
N_BUCKETS = 32
MAX_DISTANCE = 128
ROPE_BASE = 10000.0

LANES = 128
SUBLANES = 8
VMEM_LIMIT = 56 * 1024 * 1024


def _cparams(*sem):
    return pltpu.CompilerParams(dimension_semantics=sem, vmem_limit_bytes=VMEM_LIMIT)


def _sigmoid(x):
    return 1.0 / (1.0 + jnp.exp(-x))


def _silu(x):
    hx = 0.5 * x
    return hx + hx * jnp.tanh(hx)


def _softplus(x):
    return jnp.maximum(x, 0.0) + jnp.log(1.0 + jnp.exp(-jnp.abs(x)))


def _dot(a, b):
    return jnp.dot(a, b, preferred_element_type=F32)


def _dot_nt(a, b):
    return lax.dot_general(a, b, (((1,), (1,)), ((), ())), preferred_element_type=F32)


def _dot_tn(a, b):
    return lax.dot_general(a, b, (((0,), (0,)), ((), ())), preferred_element_type=F32)


def _split3(x):
    x1 = x.astype(BF16)
    r1 = x - x1.astype(F32)
    x2 = r1.astype(BF16)
    x3 = (r1 - x2.astype(F32)).astype(BF16)
    return x1, x2, x3


def _dot_exact_l(m01, x):
    x1, x2, x3 = _split3(x)
    return _dot(m01, x1) + _dot(m01, x2) + _dot(m01, x3)


def _dot_exact_nt(m01, x):
    x1, x2, x3 = _split3(x)
    return _dot_nt(m01, x1) + _dot_nt(m01, x2) + _dot_nt(m01, x3)


def _iota(shape, dim):
    return lax.broadcasted_iota(jnp.int32, shape, dim)


def _div_pow2(x, n):
    assert n & (n - 1) == 0
    return jnp.right_shift(x, n.bit_length() - 1)


def _row_tile(t, pref):
    return pref if t % pref == 0 else t


def _mod_kernel(c_ref, w_ref, b_ref, o_ref):
    s = _silu(c_ref[...])
    w = w_ref[...]
    s1 = s.astype(BF16)
    s2 = (s - s1.astype(F32)).astype(BF16)
    w1 = w.astype(BF16)
    w2 = (w - w1.astype(F32)).astype(BF16)
    o_ref[...] = _dot(s1, w1) + _dot(s1, w2) + _dot(s2, w1) + b_ref[...]


def _modulation(c_all, ada_w, ada_b):
    nb, d = c_all.shape
    depth, _, n = ada_w.shape
    tn = 1024
    return pl.pallas_call(
        _mod_kernel,
        grid=(depth, n // tn),
        in_specs=[pl.BlockSpec((nb, d), lambda l, j: (0, 0)),
                  pl.BlockSpec((None, d, tn), lambda l, j: (l, 0, j)),
                  pl.BlockSpec((None, 1, tn), lambda l, j: (l, 0, j))],
        out_specs=pl.BlockSpec((None, nb, tn), lambda l, j: (l, 0, j)),
        out_shape=jax.ShapeDtypeStruct((depth, nb, n), F32),
        compiler_params=_cparams("arbitrary", "arbitrary"),
        name="adaln_modulation",
    )(c_all, ada_w, ada_b.reshape(depth, 1, n))


def _mod_specs(layer, boff, d, which):
    return [pl.BlockSpec((None, None, 1, d), lambda b, i, w=w: (layer, boff + b, 0, w)) for w in which]


def _modulated(x_ref, shift_ref, scale_ref):
    return (x_ref[...] * (1.0 + scale_ref[...]) + shift_ref[...]).astype(BF16)


def _const_spec(shape):
    return pl.BlockSpec(shape, lambda b, i: (0,) * len(shape))


def _rows_spec(tm, n):
    return pl.BlockSpec((None, tm, n), lambda b, i: (b, i, 0))


def _gdn_proj_kernel(x_ref, shift_ref, scale_ref, wqkv_ref, wba_ref, wz_ref, cbuf_ref, cw_ref, avec_ref, dtvec_ref,
                     q_ref, k_ref, v_ref, gates_ref, z_ref, tail_ref, ext_ref, *, chunk):
    i = pl.program_id(1)
    tm, d = x_ref.shape
    dk = d // H_G
    u = _modulated(x_ref, shift_ref, scale_ref)

    @pl.when(i == 0)
    def _():
        ext_ref[0:SUBLANES, :] = cbuf_ref[...]

    for s in range(3):
        ext_ref[SUBLANES:SUBLANES + tm, s * d:(s + 1) * d] = _dot(u, wqkv_ref[:, s * d:(s + 1) * d])
    outs = (q_ref, k_ref, v_ref)
    for s in range(3):
        for h in range(H_G):
            c0 = s * d + h * dk
            e = ext_ref[:, c0:c0 + dk]
            acc = cw_ref[0:1, c0:c0 + dk] * e
            for j in range(1, CONV_W):
                acc = pltpu.roll(acc, 1, 0) + cw_ref[j:j + 1, c0:c0 + dk] * e
            hy = acc[SUBLANES:, :]
            y = hy + hy * jnp.tanh(hy)
            if s < 2:
                inv = lax.rsqrt(jnp.sum(y * y, axis=-1, keepdims=True) + NORM_EPS)
                y = y * (inv * (dk ** -0.5) if s == 0 else inv)
            outs[s][:, h * dk:(h + 1) * dk] = y.astype(BF16)
    tail = ext_ref[tm:tm + SUBLANES, :]
    tail_ref[...] = tail
    ext_ref[0:SUBLANES, :] = tail

    ba = _dot(u, wba_ref[...])
    g = -jnp.exp(avec_ref[...]) * _softplus(ba + dtvec_ref[...])
    r = _iota((tm, tm), 0)
    c = _iota((tm, tm), 1)
    tri = jnp.where(_div_pow2(r, chunk) == _div_pow2(c, chunk), jnp.where(r >= c, 1.0, 0.0), 0.0).astype(BF16)
    gcum = _dot_exact_l(tri, g)
    gates_ref[...] = jnp.where(_iota((tm, LANES), 1) < H_G, _sigmoid(ba), gcum)
    z_ref[...] = _dot(u, wz_ref[...]).astype(BF16)


def _gdn_proj(x, mod4, layer, boff, w_in, conv_buf, conv_w, a_log, dt_bias):
    b, t, d = x.shape
    tm = _row_tile(t, 256)
    chunk = min(GDN_CHUNK, t)
    assert t >= CONV_W - 1 and tm >= SUBLANES and tm % chunk == 0
    wqkv = w_in[:, :3 * d].astype(BF16)
    wba = jnp.pad(w_in[:, 3 * d:3 * d + 2 * H_G], ((0, 0), (0, LANES - 2 * H_G))).astype(BF16)
    wz = w_in[:, 3 * d + 2 * H_G:].astype(BF16)
    cbuf = jnp.pad(conv_buf.astype(F32), ((0, 0), (SUBLANES - (CONV_W - 1), 0), (0, 0)))
    cw = jnp.pad(0.5 * conv_w.astype(F32), ((0, SUBLANES - CONV_W), (0, 0)))
    avec = jnp.pad(a_log.astype(F32), (H_G, LANES - 2 * H_G)).reshape(1, LANES)
    dtvec = jnp.pad(dt_bias.astype(F32), (H_G, LANES - 2 * H_G)).reshape(1, LANES)
    bf16o = jax.ShapeDtypeStruct((b, t, d), BF16)
    tail_spec = pl.BlockSpec((None, SUBLANES, 3 * d), lambda bb, i: (bb, 0, 0))
    return pl.pallas_call(
        functools.partial(_gdn_proj_kernel, chunk=chunk),
        grid=(b, t // tm),
        in_specs=[_rows_spec(tm, d)] + _mod_specs(layer, boff, d, (0, 1))
                 + [_const_spec((d, 3 * d)), _const_spec((d, LANES)), _const_spec((d, d)), tail_spec,
                    _const_spec((SUBLANES, 3 * d)), _const_spec((1, LANES)), _const_spec((1, LANES))],
        out_specs=[_rows_spec(tm, d)] * 3 + [_rows_spec(tm, LANES), _rows_spec(tm, d), tail_spec],
        out_shape=[bf16o, bf16o, bf16o, jax.ShapeDtypeStruct((b, t, LANES), F32), bf16o,
                   jax.ShapeDtypeStruct((b, SUBLANES, 3 * d), F32)],
        scratch_shapes=[pltpu.VMEM((tm + SUBLANES, 3 * d), F32)],
        compiler_params=_cparams("arbitrary", "arbitrary"),
        name="gdn_in_proj",
    )(x, mod4, mod4, wqkv, wba, wz, cbuf, cw, avec, dtvec)


def _unit_lower_inverse_minus_identity(mats):
    n = mats[0].shape[0]
    r = _iota((n, n), 0)
    c = _iota((n, n), 1)

    def mm(xs, ys):
        return [_dot(x.astype(BF16), y.astype(BF16)) for x, y in zip(xs, ys)]

    base = 8
    diag = _div_pow2(r, base) == _div_pow2(c, base)
    d = [jnp.where(diag, a, 0.0) for a in mats]
    d2 = mm(d, d)
    d4 = mm(d2, d2)
    nn = [-x for x in d]
    nn = [x + y + z for x, y, z in zip(nn, d2, mm(nn, d2))]
    nn = [x + y + z for x, y, z in zip(nn, d4, mm(nn, d4))]
    m = base
    while m < n:
        pair = (_div_pow2(r, 2 * m) == _div_pow2(c, 2 * m)) & (_div_pow2(r, m) != _div_pow2(c, m))
        off = [jnp.where(pair, a, 0.0) for a in mats]
        y = [o + p for o, p in zip(off, mm(nn, off))]
        x = [p + q for p, q in zip(y, mm(y, nn))]
        nn = [p - q for p, q in zip(nn, x)]
        m *= 2
    return nn


def _gdn_kernel(q_ref, k_ref, v_ref, gates_ref, z_ref, s0_ref, nw_ref, o_ref, s_ref, *, chunk):
    ti = pl.program_id(1)
    tb, d = z_ref.shape
    dk = d // H_G
    n_chunks = tb // chunk

    @pl.when(ti == 0)
    def _():
        s_ref[...] = s0_ref[...]

    ri = _iota((chunk, chunk), 0)
    ci = _iota((chunk, chunk), 1)
    eye_l = (_iota((LANES, LANES), 0) == _iota((LANES, LANES), 1)).astype(BF16)
    incl = ri >= ci
    strict = ri > ci
    nw = nw_ref[...]

    def chunk_body(cidx, carry):
        r0 = pl.multiple_of(cidx * chunk, chunk)
        gates = gates_ref[pl.ds(r0, chunk), :]
        gates_t = _dot_exact_nt(eye_l, gates)

        heads = range(H_G)
        kbf = [k_ref[pl.ds(r0, chunk), h * dk:(h + 1) * dk] for h in heads]
        qbf = [q_ref[pl.ds(r0, chunk), h * dk:(h + 1) * dk] for h in heads]
        q = [x.astype(F32) for x in qbf]
        k = [x.astype(F32) for x in kbf]
        v = [v_ref[pl.ds(r0, chunk), h * dk:(h + 1) * dk].astype(F32) for h in heads]
        beta = [gates[:, h:h + 1] for h in heads]
        gcol = [gates[:, H_G + h:H_G + h + 1] for h in heads]
        grow = [gates_t[H_G + h:H_G + h + 1, :] for h in heads]
        dec_incl = [jnp.exp(jnp.where(incl, gc - gr, NEG_INF)) for gc, gr in zip(gcol, grow)]
        kb = [x * bt for x, bt in zip(k, beta)]
        a_mat = [_dot_nt(x.astype(BF16), y) for x, y in zip(kb, kbf)]
        qk = [_dot_nt(x, y) for x, y in zip(qbf, kbf)]
        a_mat = [jnp.where(strict, x * e, 0.0) for x, e in zip(a_mat, dec_incl)]
        qk = [x * e for x, e in zip(qk, dec_incl)]
        exp_g = [jnp.exp(gc) for gc in gcol]
        rhs = [jnp.concatenate([x * bt, y * e], axis=1) for x, bt, y, e in zip(v, beta, kb, exp_g)]
        nn = _unit_lower_inverse_minus_identity(a_mat)
        sol = [x + _dot(y.astype(BF16), x.astype(BF16)) for x, y in zip(rhs, nn)]
        s = [s_ref[h] for h in heads]
        sb = [x.astype(BF16) for x in s]
        v_res = [x[:, :dk] - _dot(x[:, dk:].astype(BF16), y) for x, y in zip(sol, sb)]
        vrb = [x.astype(BF16) for x in v_res]
        o = [_dot((x * e).astype(BF16), y) for x, e, y in zip(q, exp_g, sb)]
        o = [x + _dot(y.astype(BF16), z) for x, y, z in zip(o, qk, vrb)]
        g_last = [gc[chunk - 1:chunk, :] for gc in gcol]
        k_dec = [(x * jnp.exp(gl - gc)).astype(BF16) for x, gl, gc in zip(k, g_last, gcol)]
        s_add = [_dot_tn(x, y) for x, y in zip(k_dec, vrb)]
        for h in heads:
            s_ref[h] = s[h] * jnp.exp(g_last[h]) + s_add[h]
            on = o[h] * lax.rsqrt(jnp.mean(o[h] * o[h], axis=-1, keepdims=True) + NORM_EPS) * nw
            zz = z_ref[pl.ds(r0, chunk), h * dk:(h + 1) * dk].astype(F32)
            o_ref[pl.ds(r0, chunk), h * dk:(h + 1) * dk] = (on * _silu(zz)).astype(BF16)
        return carry

    lax.fori_loop(0, n_chunks, chunk_body, 0)


def _gdn_mix(q, k, v, gates, z, s0, norm_w):
    b, t, d = q.shape
    dk = d // H_G
    chunk = min(GDN_CHUNK, t)
    tb = _row_tile(t, 4 * chunk)
    nw = norm_w.astype(F32).reshape(1, dk)
    state_spec = pl.BlockSpec((None, H_G, dk, dk), lambda bb, i: (bb, 0, 0, 0))
    return pl.pallas_call(
        functools.partial(_gdn_kernel, chunk=chunk),
        grid=(b, t // tb),
        in_specs=[_rows_spec(tb, d)] * 3 + [_rows_spec(tb, LANES), _rows_spec(tb, d), state_spec,
                                            _const_spec((1, dk))],
        out_specs=[_rows_spec(tb, d), state_spec],
        out_shape=[jax.ShapeDtypeStruct((b, t, d), BF16), jax.ShapeDtypeStruct((b, H_G, dk, dk), F32)],
        compiler_params=_cparams("arbitrary", "arbitrary"),
        name="gdn_mixer",
    )(q, k, v, gates, z, s0.astype(F32), nw)


def _out_proj_kernel(o_ref, x_ref, gate_ref, w_ref, g_ref, b_ref, y_ref):
    h = _dot(o_ref[...], w_ref[...])
    y = DEEPNORM_ALPHA * x_ref[...] + (1.0 + gate_ref[...]) * h
    mu = jnp.mean(y, axis=-1, keepdims=True)
    yc = y - mu
    var = jnp.mean(yc * yc, axis=-1, keepdims=True)
    y_ref[...] = yc * lax.rsqrt(var + LN_EPS) * g_ref[...] + b_ref[...]


def _out_proj(o, x, mod4, layer, boff, w_out, ln_g, ln_b):
    b, t, d = x.shape
    kdim = o.shape[-1]
    tm = _row_tile(t, 512)
    return pl.pallas_call(
        _out_proj_kernel,
        grid=(b, t // tm),
        in_specs=[_rows_spec(tm, kdim), _rows_spec(tm, d)] + _mod_specs(layer, boff, d, (2,))
                 + [_const_spec((kdim, d)), _const_spec((1, d)), _const_spec((1, d))],
        out_specs=_rows_spec(tm, d),
        out_shape=jax.ShapeDtypeStruct((b, t, d), F32),
        compiler_params=_cparams("arbitrary", "arbitrary"),
        name="out_proj_postnorm",
    )(o, x, mod4, w_out.astype(BF16), ln_g.reshape(1, d), ln_b.reshape(1, d))


def _qkvz_outputs(u, w_ref, q_ref, k32_ref, v32_ref, kb_ref, vb_ref, z_ref, q_scale, transposed):
    d = u.shape[-1]
    if transposed:
        q_ref[...] = (_dot_nt(w_ref[:, 0:d], u) * q_scale).astype(BF16)
    else:
        q_ref[...] = (_dot(u, w_ref[:, 0:d]) * q_scale).astype(BF16)
    k = _dot(u, w_ref[:, d:2 * d])
    k32_ref[...] = k
    kb_ref[...] = k.astype(BF16)
    v = _dot(u, w_ref[:, 2 * d:3 * d])
    v32_ref[...] = v
    vb_ref[...] = (v.T if transposed else v).astype(BF16)
    z_ref[...] = _dot(u, w_ref[:, 3 * d:4 * d]).astype(BF16)


def _qkvz_weights(w_in, d, transposed):
    w = w_in[:, :4 * d]
    if transposed:
        w = jnp.concatenate([w[:, :d].T, w[:, d:]], axis=1)
    return w.astype(BF16)


def _qkvz_specs(b, t, d, tm, transposed):
    cols_spec = pl.BlockSpec((None, d, tm), lambda bb, i: (bb, 0, i))
    rows = _rows_spec(tm, d)
    f32o = jax.ShapeDtypeStruct((b, t, d), F32)
    bf16o = jax.ShapeDtypeStruct((b, t, d), BF16)
    bf16t = jax.ShapeDtypeStruct((b, d, t), BF16)
    if transposed:
        return [cols_spec, rows, rows, rows, cols_spec, rows], [bf16t, f32o, f32o, bf16o, bf16t, bf16o]
    return [rows] * 6, [bf16o, f32o, f32o, bf16o, bf16o, bf16o]


def _fox_proj_kernel(x_ref, shift_ref, scale_ref, w_ref, wf_ref, bf_ref,
                     q_ref, k32_ref, v32_ref, kb_ref, vb_ref, z_ref, logf_ref, *, q_scale, transposed):
    u = _modulated(x_ref, shift_ref, scale_ref)
    _qkvz_outputs(u, w_ref, q_ref, k32_ref, v32_ref, kb_ref, vb_ref, z_ref, q_scale, transposed)
    f = _dot(u, wf_ref[...])[:, :H_F] + bf_ref[...]
    logf_ref[...] = -_softplus(-f)


def _fox_proj(x, mod4, layer, boff, w_in, b_f, transposed):
    b, t, d = x.shape
    tm = _row_tile(t, 256)
    w = _qkvz_weights(w_in, d, transposed)
    wf = jnp.pad(w_in[:, 4 * d:], ((0, 0), (0, LANES - H_F))).astype(BF16)
    out_specs, out_shape = _qkvz_specs(b, t, d, tm, transposed)
    return pl.pallas_call(
        functools.partial(_fox_proj_kernel, q_scale=(d // H_F) ** -0.5 * (LOG2E if transposed else 1.0),
                          transposed=transposed),
        grid=(b, t // tm),
        in_specs=[_rows_spec(tm, d)] + _mod_specs(layer, boff, d, (0, 1))
                 + [_const_spec((d, 4 * d)), _const_spec((d, LANES)), _const_spec((1, H_F))],
        out_specs=out_specs + [_rows_spec(tm, H_F)],
        out_shape=out_shape + [jax.ShapeDtypeStruct((b, t, H_F), F32)],
        compiler_params=_cparams("arbitrary", "arbitrary"),
        name="fox_in_proj",
    )(x, mod4, mod4, w, wf, b_f.astype(F32).reshape(1, H_F))


def _cumsum_kernel(x_ref, c0_ref, cn_ref, ct_ref, *, blk):
    s, h = x_ref.shape
    tri = (_iota((blk, blk), 0) >= _iota((blk, blk), 1)).astype(BF16)
    eye_h = (_iota((h, h), 0) == _iota((h, h), 1)).astype(BF16)
    carry = c0_ref[...]
    for i in range(s // blk):
        c = _dot_exact_l(tri, x_ref[i * blk:(i + 1) * blk, :]) + carry
        cn_ref[i * blk:(i + 1) * blk, :] = c
        ct_ref[:, i * blk:(i + 1) * blk] = _dot_exact_nt(eye_h, c)
        carry = c[blk - 1:blk, :]


def _cumsum_time(x, c0):
    b, s, h = x.shape
    blk = 256 if s % 256 == 0 else s
    return pl.pallas_call(
        functools.partial(_cumsum_kernel, blk=blk),
        grid=(b,),
        in_specs=[pl.BlockSpec((None, s, h), lambda bb: (bb, 0, 0)),
                  pl.BlockSpec((None, 1, h), lambda bb: (bb, 0, 0))],
        out_specs=[pl.BlockSpec((None, s, h), lambda bb: (bb, 0, 0)),
                   pl.BlockSpec((None, h, s), lambda bb: (bb, 0, 0))],
        out_shape=[jax.ShapeDtypeStruct((b, s, h), F32), jax.ShapeDtypeStruct((b, h, s), F32)],
        compiler_params=_cparams("arbitrary"),
        name="logf_cumsum",
    )(x, c0)


AUG = LANES // H_F


def _fox_aug_kernel(x_ref, kaug_ref, qaugt_ref, *, blk):
    s, h = x_ref.shape
    tri = (_iota((blk, blk), 0) >= _iota((blk, blk), 1)).astype(BF16)
    lane_h = _iota((h, LANES), 1)
    row_h = _iota((h, LANES), 0)
    ek = [jnp.where(lane_h == row_h * AUG + part, -1.0, 0.0).astype(BF16) for part in range(3)]
    row_q = _iota((LANES, h), 0)
    col_q = _iota((LANES, h), 1)
    eq = [jnp.where(row_q == col_q * AUG + 3 + part, 1.0, 0.0).astype(BF16) for part in range(3)]
    k_slot = _iota((blk, LANES), 1) & (AUG - 1)
    k_ones = jnp.where(k_slot >= 3, jnp.where(k_slot < 6, 1.0, 0.0), 0.0)
    q_ones = jnp.where((_iota((LANES, blk), 0) & (AUG - 1)) < 3, 1.0, 0.0)
    carry = jnp.zeros((1, h), F32)
    for i in range(s // blk):
        c = _dot_exact_l(tri, x_ref[i * blk:(i + 1) * blk, :]) + carry
        parts = _split3(c * LOG2E)
        kaug = k_ones
        qaugt = q_ones
        for part in range(3):
            kaug = kaug + _dot(parts[part], ek[part])
            qaugt = qaugt + _dot_nt(eq[part], parts[part])
        kaug_ref[i * blk:(i + 1) * blk, :] = kaug.astype(BF16)
        qaugt_ref[:, i * blk:(i + 1) * blk] = qaugt.astype(BF16)
        carry = c[blk - 1:blk, :]


def _fox_aug(logf):
    b, s, h = logf.shape
    assert h * AUG == LANES and AUG >= 6
    blk = 256 if s % 256 == 0 else s
    return pl.pallas_call(
        functools.partial(_fox_aug_kernel, blk=blk),
        grid=(b,),
        in_specs=[pl.BlockSpec((None, s, h), lambda bb: (bb, 0, 0))],
        out_specs=[pl.BlockSpec((None, s, LANES), lambda bb: (bb, 0, 0)),
                   pl.BlockSpec((None, LANES, s), lambda bb: (bb, 0, 0))],
        out_shape=[jax.ShapeDtypeStruct((b, s, LANES), BF16), jax.ShapeDtypeStruct((b, LANES, s), BF16)],
        compiler_params=_cparams("arbitrary"),
        name="fox_bias_operands",
    )(logf)


SUM_ROWS = 16


def _softmax_t_step(s, vt, m_ref, acc_ref):
    m_prev = m_ref[...]
    m_new = jnp.maximum(m_prev, jnp.max(s, axis=0, keepdims=True))
    alpha = jnp.exp2(m_prev - m_new)
    p = jnp.exp2(s - m_new).astype(BF16)
    vt_ext = jnp.concatenate([vt, jnp.ones((SUM_ROWS, vt.shape[1]), BF16)], axis=0)
    acc_ref[...] = alpha * acc_ref[...] + _dot(vt_ext, p)
    m_ref[...] = m_new


def _softmax_t_result(acc_ref, dv):
    return acc_ref[0:dv, :] / acc_ref[dv:dv + 1, :]


def _fox_attn_kernel(qt_ref, k_ref, kaug_ref, vt_ref, qaugt_ref, z_ref, o_ref,
                     qcat_ref, s_ref, m_ref, acc_ref, *, tk):
    hp = pl.program_id(1)
    qi = pl.program_id(2)
    tq = qt_ref.shape[1]
    assert tq == 2 * tk
    dh = LANES // 2
    row = _iota((LANES, tq), 0)
    qt = qt_ref[...].astype(F32)
    qa = qaugt_ref[...].astype(F32)
    for hh in range(2):
        qcat_ref[hh, 0:LANES, :] = jnp.where(_div_pow2(row, dh) == hh, qt, 0.0).astype(BF16)
        qcat_ref[hh, LANES:2 * LANES, :] = jnp.where(_div_pow2(row, AUG) == hp * 2 + hh, qa, 0.0).astype(BF16)
    m_ref[...] = jnp.full(m_ref.shape, NEG_INF, F32)
    acc_ref[...] = jnp.zeros(acc_ref.shape, F32)
    n_full = qi * 2
    key_in = _iota((tk, tk), 0)
    qry_in = _iota((tk, tk), 1)

    def scores(j, slot, q0=0):
        k0 = pl.multiple_of(j * tk, tk)
        kcat = jnp.concatenate([k_ref[pl.ds(k0, tk), :], kaug_ref[pl.ds(k0, tk), :]], axis=1)
        for hh in range(2):
            s_ref[slot, hh, :, q0:] = _dot(kcat, qcat_ref[hh, :, q0:])

    def consume(j, slot, masked, q0=0):
        k0 = pl.multiple_of(j * tk, tk)
        for hh in range(2):
            for half in range(q0 // tk, 2):
                s = s_ref[slot, hh, :, half * tk:(half + 1) * tk]
                if masked:
                    s = jnp.where(k0 + key_in <= qi * tq + half * tk + qry_in, s, NEG_INF)
                _softmax_t_step(s, vt_ref[hh * dh:(hh + 1) * dh, pl.ds(k0, tk)], m_ref.at[hh, half],
                                acc_ref.at[hh, half])

    scores(0, 0)

    def body(i, carry):
        a = 2 * i
        scores(a + 1, 1)
        consume(a, 0, False)
        scores(a + 2, 0)
        consume(a + 1, 1, False)
        return carry

    lax.fori_loop(0, qi, body, 0)
    scores(n_full + 1, 1, q0=tk)
    consume(n_full, 0, True)
    consume(n_full + 1, 1, True, q0=tk)
    o_t = jnp.concatenate(
        [jnp.concatenate([_softmax_t_result(acc_ref.at[hh, half], dh) for half in range(2)], axis=1)
         for hh in range(2)], axis=0)
    o_ref[...] = (o_t.T * _silu(z_ref[...].astype(F32))).astype(BF16)


def _fox_attn_prompt(qt, kb, kaug, vt, qaugt, z):
    b, d, t = qt.shape
    tq, tk = ATTN_TQ, ATTN_TK
    assert t % tq == 0
    hpairs = d // LANES
    dh = LANES // 2
    return pl.pallas_call(
        functools.partial(_fox_attn_kernel, tk=tk),
        grid=(b, hpairs, t // tq),
        in_specs=[pl.BlockSpec((None, LANES, tq), lambda bb, hp, i: (bb, hp, i)),
                  pl.BlockSpec((None, t, LANES), lambda bb, hp, i: (bb, 0, hp)),
                  pl.BlockSpec((None, t, LANES), lambda bb, hp, i: (bb, 0, 0)),
                  pl.BlockSpec((None, LANES, t), lambda bb, hp, i: (bb, hp, 0)),
                  pl.BlockSpec((None, LANES, tq), lambda bb, hp, i: (bb, 0, i)),
                  pl.BlockSpec((None, tq, LANES), lambda bb, hp, i: (bb, i, hp))],
        out_specs=pl.BlockSpec((None, tq, LANES), lambda bb, hp, i: (bb, i, hp)),
        out_shape=jax.ShapeDtypeStruct((b, t, d), BF16),
        scratch_shapes=[pltpu.VMEM((2, 2 * LANES, tq), BF16), pltpu.VMEM((2, 2, tk, tq), F32),
                        pltpu.VMEM((2, 2, 1, tk), F32), pltpu.VMEM((2, 2, dh + SUM_ROWS, tk), F32)],
        compiler_params=_cparams("arbitrary", "arbitrary", "arbitrary"),
        name="fox_attention_prompt",
    )(qt, kb, kaug, vt, qaugt, z)


def _fox_decode_kernel(q_ref, kp_ref, vp_ref, kn_ref, vn_ref, z_ref, cq_ref, ckt_ref, o_ref):
    hp = pl.program_id(1)
    t = q_ref.shape[0]
    p_len = kp_ref.shape[0]
    dh = LANES // 2
    lane = _iota((t, LANES), 1)
    q = q_ref[...]
    kp = kp_ref[...].astype(BF16)
    vp = vp_ref[...].astype(BF16)
    kn = kn_ref[...]
    vn = vn_ref[...]
    cq_all = cq_ref[...]
    hlane = _iota(cq_all.shape, 1)
    causal = _iota((t, t), 1) <= _iota((t, t), 0)
    outs = []
    for hh in range(2):
        h = hp * 2 + hh
        qm = jnp.where(_div_pow2(lane, dh) == hh, q, jnp.zeros_like(q))
        cq = jnp.sum(jnp.where(hlane == h, cq_all, 0.0), axis=-1, keepdims=True)
        ck = ckt_ref[pl.ds(h, 1), :]
        s_p = _dot_nt(qm, kp) + cq - ck[:, :p_len]
        s_n = jnp.where(causal, _dot_nt(qm, kn) + cq - ck[:, p_len:], NEG_INF)
        m = jnp.maximum(jnp.max(s_p, axis=-1, keepdims=True), jnp.max(s_n, axis=-1, keepdims=True))
        e_p = jnp.exp(s_p - m)
        e_n = jnp.exp(s_n - m)
        den = jnp.sum(e_p, axis=-1, keepdims=True) + jnp.sum(e_n, axis=-1, keepdims=True)
        outs.append((_dot(e_p.astype(BF16), vp) + _dot(e_n.astype(BF16), vn)) / den)
    o = jnp.where(lane < dh,outs[0], outs[1])
    o_ref[...] = (o * _silu(z_ref[...].astype(F32))).astype(BF16)


def _fox_attn_sample(q, k_past, v_past, kb, vb, z, cq_new, cum_t):
    b, t, d = q.shape
    p_len = k_past.shape[1]
    hpairs = d // LANES
    new_spec = pl.BlockSpec((None, t, LANES), lambda bb, hp: (bb, 0, hp))
    past_spec = pl.BlockSpec((None, p_len, LANES), lambda bb, hp: (bb, 0, hp))
    return pl.pallas_call(
        _fox_decode_kernel,
        grid=(b, hpairs),
        in_specs=[new_spec, past_spec, past_spec, new_spec, new_spec, new_spec,
                  pl.BlockSpec((None, t, H_F), lambda bb, hp: (bb, 0, 0)),
                  pl.BlockSpec((None, H_F, p_len + t), lambda bb, hp: (bb, 0, 0))],
        out_specs=new_spec,
        out_shape=jax.ShapeDtypeStruct((b, t, d), BF16),
        compiler_params=_cparams("arbitrary", "arbitrary"),
        name="fox_attention_sample",
    )(q, k_past, v_past, kb, vb, z, cq_new, cum_t)


def _diff_proj_kernel(x_ref, shift_ref, scale_ref, w_ref, q_ref, k32_ref, v32_ref, kb_ref, vb_ref, z_ref,
                      *, q_scale, transposed):
    u = _modulated(x_ref, shift_ref, scale_ref)
    _qkvz_outputs(u, w_ref, q_ref, k32_ref, v32_ref, kb_ref, vb_ref, z_ref, q_scale, transposed)


def _diff_proj(x, mod4, layer, boff, w_in, transposed):
    b, t, d = x.shape
    tm = _row_tile(t, 256)
    out_specs, out_shape = _qkvz_specs(b, t, d, tm, transposed)
    return pl.pallas_call(
        functools.partial(_diff_proj_kernel, q_scale=(d // (2 * H_D)) ** -0.5 * (LOG2E if transposed else 1.0),
                          transposed=transposed),
        grid=(b, t // tm),
        in_specs=[_rows_spec(tm, d)] + _mod_specs(layer, boff, d, (0, 1)) + [_const_spec((d, 4 * d))],
        out_specs=out_specs,
        out_shape=out_shape,
        compiler_params=_cparams("arbitrary", "arbitrary"),
        name="diff_in_proj",
    )(x, mod4, mod4, _qkvz_weights(w_in, d, transposed))


def _t5_thresholds():
    nb = N_BUCKETS // 2
    max_exact = nb // 2
    steps = nb - max_exact
    ratio = MAX_DISTANCE // max_exact
    out = []
    for kk in range(1, nb - max_exact):
        target = max_exact ** steps * ratio ** kk
        n = max_exact
        while n ** steps < target:
            n += 1
        out.append(n)
    return nb, max_exact, out


def _bias_kernel(tbl_ref, o_ref, *, q0, k0, keys_on_rows):
    h = pl.program_id(0)
    shape = o_ref.shape
    kdim, qdim = (0, 1) if keys_on_rows else (1, 0)
    rel = (k0 + _iota(shape, kdim)) - (q0 + _iota(shape, qdim))
    nb, max_exact, thr = _t5_thresholds()
    n = jnp.abs(rel)
    large = jnp.full(shape, max_exact, jnp.int32)
    for tval in thr:
        large = large + (n >= tval).astype(jnp.int32)
    bucket = jnp.where(rel > 0, nb, 0) + jnp.where(n < max_exact, n, large)
    acc = jnp.zeros(shape, F32)
    for bkt in range(N_BUCKETS):
        acc = jnp.where(bucket == bkt, tbl_ref[bkt * H_D + h], acc)
    if keys_on_rows:
        acc = (acc - tbl_ref[(nb - 1) * H_D + h]) * LOG2E
    o_ref[...] = acc


def _bias_tile(rel_table, q0, nq, k0, nk, keys_on_rows=False):
    shape = (nk, nq) if keys_on_rows else (nq, nk)
    return pl.pallas_call(
        functools.partial(_bias_kernel, q0=q0, k0=k0, keys_on_rows=keys_on_rows),
        grid=(H_D,),
        in_specs=[pl.BlockSpec(memory_space=pltpu.SMEM)],
        out_specs=pl.BlockSpec((None,) + shape, lambda h: (h, 0, 0)),
        out_shape=jax.ShapeDtypeStruct((H_D,) + shape, F32),
        compiler_params=_cparams("arbitrary"),
        name="t5_bias_tile",
    )(rel_table.astype(F32).reshape(N_BUCKETS * H_D))


def _diff_lambda(lam_ref):
    lam = lam_ref[...]
    s1 = jnp.sum(lam[0:1, :] * lam[1:2, :], axis=-1, keepdims=True)
    s2 = jnp.sum(lam[2:3, :] * lam[3:4, :], axis=-1, keepdims=True)
    return jnp.exp(s1) - jnp.exp(s2) + LAMBDA_INIT


def _diff_epilogue(o, z_ref, subln_ref, o_ref):
    on = o * lax.rsqrt(jnp.mean(o * o, axis=-1, keepdims=True) + NORM_EPS) * subln_ref[...]
    on = on * (1.0 - LAMBDA_INIT)
    o_ref[...] = (on * _silu(z_ref[...].astype(F32))).astype(BF16)


def _diff_attn_kernel(tbl_ref, qt_ref, k_ref, vt_ref, z_ref, biasm_ref, bias0_ref, bias1_ref, lam_ref, subln_ref,
                      o_ref, qcat_ref, s_ref, m_ref, acc_ref, *, tk):
    h = pl.program_id(1)
    qi = pl.program_id(2)
    tq = qt_ref.shape[1]
    assert tq == 2 * tk
    dh = LANES // 2
    row = _iota((LANES, tq), 0)
    qt = qt_ref[...].astype(F32)
    nb, _, _ = _t5_thresholds()
    far = _split3(jnp.full((LANES, tq), tbl_ref[(nb - 1) * H_D + h], F32) * LOG2E)
    far_rows = jnp.zeros((LANES, tq), F32)
    for part in range(3):
        far_rows = jnp.where(row == part, far[part].astype(F32), far_rows)
    for br in range(2):
        qcat_ref[br, 0:LANES, :] = jnp.where(_div_pow2(row, dh) == br, qt, 0.0).astype(BF16)
        qcat_ref[br, LANES:2 * LANES, :] = far_rows.astype(BF16)
    ones_aug = jnp.where(_iota((tk, LANES), 1) < 3, 1.0, 0.0).astype(BF16)
    m_ref[...] = jnp.full(m_ref.shape, NEG_INF, F32)
    acc_ref[...] = jnp.zeros(acc_ref.shape, F32)
    key_in = _iota((tk, tk), 0)
    qry_in = _iota((tk, tk), 1)

    def scores(j, slot, q0=0):
        k0 = pl.multiple_of(j * tk, tk)
        kcat = jnp.concatenate([k_ref[pl.ds(k0, tk), :], ones_aug], axis=1)
        for br in range(2):
            s_ref[slot, br, :, q0:] = _dot(kcat, qcat_ref[br, :, q0:])

    def consume(j, slot, bias_ref=None, key_off=None, q0=0):
        k0 = pl.multiple_of(j * tk, tk)
        for br in range(2):
            for half in range(q0 // tk, 2):
                lanes = slice(half * tk, (half + 1) * tk)
                s = s_ref[slot, br, :, lanes]
                if bias_ref is not None:
                    s = s + bias_ref[:, lanes]
                if key_off is not None:
                    s = jnp.where(_div_pow2(key_off + key_in, ATTN_CHUNK)
                                  <= _div_pow2(half * tk + qry_in, ATTN_CHUNK), s, NEG_INF)
                _softmax_t_step(s, vt_ref[:, pl.ds(k0, tk)], m_ref.at[br, half], acc_ref.at[br, half])

    scores(0, 0)

    def body(i, carry):
        a = 2 * i
        scores(a + 1, 1)
        consume(a, 0)
        scores(a + 2, 0)
        consume(a + 1, 1)
        return carry

    lax.fori_loop(0, jnp.maximum(qi - 1, 0), body, 0)

    @pl.when(qi >= 1)
    def _():
        a = 2 * qi - 2
        scores(a + 1, 1)
        consume(a, 0)
        scores(a + 2, 0)
        consume(a + 1, 1, bias_ref=biasm_ref)

    a = 2 * qi
    scores(a + 1, 1, q0=tk)
    consume(a, 0, bias_ref=bias0_ref, key_off=0)
    consume(a + 1, 1, bias_ref=bias1_ref, key_off=tk, q0=tk)
    branch = [jnp.concatenate([_softmax_t_result(acc_ref.at[br, half], LANES) for half in range(2)], axis=1)
              for br in range(2)]
    o_t = branch[0] - _diff_lambda(lam_ref) * branch[1]
    _diff_epilogue(o_t.T, z_ref, subln_ref, o_ref)


def _lam_pack(lam_q1, lam_k1, lam_q2, lam_k2):
    rows = jnp.stack([lam_q1, lam_k1, lam_q2, lam_k2]).astype(F32)
    return jnp.pad(rows, ((0, SUBLANES - 4), (0, LANES - rows.shape[1])))


def _diff_attn_prompt(qt, kb, vt, z, rel_table, lam, subln_w):
    b, d, t = qt.shape
    tq, tk = ATTN_TQ, ATTN_TK
    assert t % tq == 0 and tk % ATTN_CHUNK == 0 and tk >= MAX_DISTANCE
    biasm = _bias_tile(rel_table, tk, tq, 0, tk, keys_on_rows=True)
    bias0 = _bias_tile(rel_table, 0, tq, 0, tk, keys_on_rows=True)
    bias1 = _bias_tile(rel_table, 0, tq, tk, tk, keys_on_rows=True)
    rows_spec = pl.BlockSpec((None, tq, LANES), lambda bb, h, i: (bb, i, h))
    bias_spec = pl.BlockSpec((None, tk, tq), lambda bb, h, i: (h, 0, 0))
    return pl.pallas_call(
        functools.partial(_diff_attn_kernel, tk=tk),
        grid=(b, H_D, t // tq),
        in_specs=[pl.BlockSpec(memory_space=pltpu.SMEM),
                  pl.BlockSpec((None, LANES, tq), lambda bb, h, i: (bb, h, i)),
                  pl.BlockSpec((None, t, LANES), lambda bb, h, i: (bb, 0, h)),
                  pl.BlockSpec((None, LANES, t), lambda bb, h, i: (bb, h, 0)),
                  rows_spec, bias_spec, bias_spec, bias_spec,
                  pl.BlockSpec((SUBLANES, LANES), lambda bb, h, i: (0, 0)),
                  pl.BlockSpec((1, LANES), lambda bb, h, i: (0, 0))],
        out_specs=rows_spec,
        out_shape=jax.ShapeDtypeStruct((b, t, d), BF16),
        scratch_shapes=[pltpu.VMEM((2, 2 * LANES, tq), BF16), pltpu.VMEM((2, 2, tk, tq), F32),
                        pltpu.VMEM((2, 2, 1, tk), F32), pltpu.VMEM((2, 2, LANES + SUM_ROWS, tk), F32)],
        compiler_params=_cparams("arbitrary", "arbitrary", "arbitrary"),
        name="diff_attention_prompt",
    )(rel_table.astype(F32).reshape(N_BUCKETS * H_D), qt, kb, vt, z, biasm, bias0, bias1, lam,
      subln_w.astype(F32).reshape(1, LANES))


def _diff_decode_kernel(q_ref, kp_ref, vp_ref, kn_ref, vn_ref, z_ref, bias_ref, lam_ref, subln_ref, o_ref,
                        *, p_len):
    t = q_ref.shape[0]
    dh = LANES // 2
    lane = _iota((t, LANES), 1)
    q = q_ref[...]
    kp = kp_ref[...].astype(BF16)
    vp = vp_ref[...].astype(BF16)
    kn = kn_ref[...]
    vn = vn_ref[...]
    bias = bias_ref[...]
    q_chunk = _div_pow2(p_len + _iota((t, t), 0), ATTN_CHUNK)
    kn_chunk = _div_pow2(p_len + _iota((t, t), 1), ATTN_CHUNK)
    kp_chunk = _div_pow2(_iota((t, p_len), 1), ATTN_CHUNK)
    qp_chunk = _div_pow2(p_len + _iota((t, p_len), 0), ATTN_CHUNK)
    outs = []
    for br in range(2):
        qm = jnp.where(_div_pow2(lane, dh) == br, q, jnp.zeros_like(q))
        s_p = jnp.where(kp_chunk <= qp_chunk, _dot_nt(qm, kp) + bias[:, :p_len], NEG_INF)
        s_n = jnp.where(kn_chunk <= q_chunk, _dot_nt(qm, kn) + bias[:, p_len:], NEG_INF)
        m = jnp.maximum(jnp.max(s_p, axis=-1, keepdims=True), jnp.max(s_n, axis=-1, keepdims=True))
        e_p = jnp.exp(s_p - m)
        e_n = jnp.exp(s_n - m)
        den = jnp.sum(e_p, axis=-1, keepdims=True) + jnp.sum(e_n, axis=-1, keepdims=True)
        outs.append((_dot(e_p.astype(BF16), vp) + _dot(e_n.astype(BF16), vn)) / den)
    o = outs[0] - _diff_lambda(lam_ref) * outs[1]
    _diff_epilogue(o, z_ref, subln_ref, o_ref)


def _diff_attn_sample(q, k_past, v_past, kb, vb, z, rel_table, lam, subln_w):
    b, t, d = q.shape
    p_len = k_past.shape[1]
    bias = _bias_tile(rel_table, p_len, t, 0, p_len + t)
    new_spec = pl.BlockSpec((None, t, LANES), lambda bb, h: (bb, 0, h))
    past_spec = pl.BlockSpec((None, p_len, LANES), lambda bb, h: (bb, 0, h))
    return pl.pallas_call(
        functools.partial(_diff_decode_kernel, p_len=p_len),
        grid=(b, H_D),
        in_specs=[new_spec, past_spec, past_spec, new_spec, new_spec, new_spec,
                  pl.BlockSpec((None, t, p_len + t), lambda bb, h: (h, 0, 0)),
                  pl.BlockSpec((SUBLANES, LANES), lambda bb, h: (0, 0)),
                  pl.BlockSpec((1, LANES), lambda bb, h: (0, 0))],
        out_specs=new_spec,
        out_shape=jax.ShapeDtypeStruct((b, t, d), BF16),
        compiler_params=_cparams("arbitrary", "arbitrary"),
        name="diff_attention_sample",
    )(q, k_past, v_past, kb, vb, z, bias, lam, subln_w.astype(F32).reshape(1, LANES))


def _rope_kernel(inv_ref, cos_ref, sin_ref, *, start):
    t, w = cos_ref.shape
    pos = (start + pl.program_id(0) * t + _iota((t, w), 0)).astype(F32)
    ang = pos * inv_ref[...]
    even = (_iota((t, w), 1) & 1) == 0
    cos_ref[...] = jnp.cos(ang)
    sn = jnp.sin(ang)
    sin_ref[...] = jnp.where(even, -sn, sn)


def _rope_tables(t, start, dk):
    inv_half = np.power(np.float32(ROPE_BASE), -np.arange(0, dk, 2, dtype=np.float32) / np.float32(dk))
    inv = jnp.asarray(np.repeat(inv_half.astype(np.float32), 2).reshape(1, dk))
    tt = _row_tile(t, 512)
    return pl.pallas_call(
        functools.partial(_rope_kernel, start=start),
        grid=(t // tt,),
        in_specs=[pl.BlockSpec((1, dk), lambda i: (0, 0))],
        out_specs=[pl.BlockSpec((tt, dk), lambda i: (i, 0))] * 2,
        out_shape=[jax.ShapeDtypeStruct((t, dk), F32)] * 2,
        compiler_params=_cparams("arbitrary"),
        name="rope_tables",
    )(inv)


def _rotate_pairs(x, cos, sin_signed):
    slabs = []
    for c0 in range(0, x.shape[-1], LANES):
        xs = x[:, c0:c0 + LANES]
        even = (_iota(xs.shape, 1) & 1) == 0
        slabs.append(jnp.where(even, pltpu.roll(xs, LANES - 1, 1), pltpu.roll(xs, 1, 1)))
    return x * cos + jnp.concatenate(slabs, axis=1) * sin_signed


def _ret_proj_kernel(x_ref, shift_ref, scale_ref, w_ref, cos_ref, sin_ref, q_ref, k_ref, v_ref, z_ref,
                     *, q_scale):
    u = _modulated(x_ref, shift_ref, scale_ref)
    d = x_ref.shape[-1]
    dk = cos_ref.shape[-1]
    cos = cos_ref[...]
    sn = sin_ref[...]
    for h in range(d // dk):
        qh = _dot(u, w_ref[:, h * dk:(h + 1) * dk])
        q_ref[:, h * dk:(h + 1) * dk] = (_rotate_pairs(qh, cos, sn) * q_scale).astype(BF16)
        kh = _dot(u, w_ref[:, d + h * dk:d + (h + 1) * dk])
        k_ref[:, h * dk:(h + 1) * dk] = _rotate_pairs(kh, cos, sn).astype(BF16)
    for s in range(2):
        v_ref[:, s * d:(s + 1) * d] = _dot(u, w_ref[:, (2 + s) * d:(3 + s) * d]).astype(BF16)
        z_ref[:, s * d:(s + 1) * d] = _dot(u, w_ref[:, (4 + s) * d:(5 + s) * d]).astype(BF16)


def _ret_proj(x, mod4, layer, boff, w_in, cos, sin_signed):
    b, t, d = x.shape
    dk = d // H_R
    tm = _row_tile(t, 256)
    tab_spec = pl.BlockSpec((tm, dk), lambda bb, i: (i, 0))
    return pl.pallas_call(
        functools.partial(_ret_proj_kernel, q_scale=dk ** -0.5),
        grid=(b, t // tm),
        in_specs=[_rows_spec(tm, d)] + _mod_specs(layer, boff, d, (0, 1)) + [_const_spec((d, 6 * d)), tab_spec, tab_spec],
        out_specs=[_rows_spec(tm, d), _rows_spec(tm, d), _rows_spec(tm, 2 * d), _rows_spec(tm, 2 * d)],
        out_shape=[jax.ShapeDtypeStruct((b, t, d), BF16), jax.ShapeDtypeStruct((b, t, d), BF16),
                   jax.ShapeDtypeStruct((b, t, 2 * d), BF16), jax.ShapeDtypeStruct((b, t, 2 * d), BF16)],
        compiler_params=_cparams("arbitrary", "arbitrary"),
        name="ret_in_proj",
    )(x, mod4, mod4, w_in.astype(BF16), cos, sin_signed)


def _ret_kernel(q_ref, k_ref, v_ref, z_ref, s0_ref, gn_ref, o_ref, s_ref, intra_ref):
    ti = pl.program_id(1)
    lr = q_ref.shape[0]
    dk = q_ref.shape[-1] // H_R
    dv = v_ref.shape[-1] // H_R
    heads = range(H_R)
    log_gamma = [math.log1p(-(2.0 ** (-5.0 - h))) for h in heads]

    @pl.when(ti == 0)
    def _():
        s_ref[...] = s0_ref[...]
        rel = (_iota((lr, lr), 0) - _iota((lr, lr), 1)).astype(F32)
        for h in heads:
            intra_ref[h] = jnp.where(rel >= 0, jnp.exp(log_gamma[h] * jnp.maximum(rel, 0.0)), 0.0)

    idx = _iota((lr, 1), 0).astype(F32)
    for h in heads:
        q_dec = jnp.exp(log_gamma[h] * (idx + 1.0))
        k_dec = jnp.exp(log_gamma[h] * (lr - 1.0 - idx))
        qh = q_ref[:, h * dk:(h + 1) * dk]
        kh = k_ref[:, h * dk:(h + 1) * dk]
        vh = v_ref[:, h * dv:(h + 1) * dv]
        s = s_ref[h]
        att = _dot_nt(qh, kh) * intra_ref[h]
        o = _dot(att.astype(BF16), vh) + _dot(qh, s.astype(BF16)) * q_dec
        s_ref[h] = s * math.exp(log_gamma[h] * lr) + _dot_tn((kh.astype(F32) * k_dec).astype(BF16), vh)
        mu = jnp.mean(o, axis=-1, keepdims=True)
        oc = o - mu
        var = jnp.mean(oc * oc, axis=-1, keepdims=True)
        on = oc * lax.rsqrt(var + LN_EPS) * gn_ref[:, h * dv:(h + 1) * dv]
        zz = z_ref[:, h * dv:(h + 1) * dv].astype(F32)
        o_ref[:, h * dv:(h + 1) * dv] = (on * _silu(zz)).astype(BF16)


def _ret_mix(q, k, v, z, s0, gn_w):
    b, t, d = q.shape
    dk = d // H_R
    dv = v.shape[-1] // H_R
    lr = _row_tile(t, 256)
    state_spec = pl.BlockSpec((None, H_R, dk, dv), lambda bb, i: (bb, 0, 0, 0))
    return pl.pallas_call(
        _ret_kernel,
        grid=(b, t // lr),
        in_specs=[_rows_spec(lr, d), _rows_spec(lr, d), _rows_spec(lr, 2 * d), _rows_spec(lr, 2 * d), state_spec,
                  _const_spec((1, 2 * d))],
        out_specs=[_rows_spec(lr, 2 * d), state_spec],
        out_shape=[jax.ShapeDtypeStruct((b, t, 2 * d), BF16), jax.ShapeDtypeStruct((b, H_R, dk, dv), F32)],
        scratch_shapes=[pltpu.VMEM((H_R, lr, lr), F32)],
        compiler_params=_cparams("arbitrary", "arbitrary"),
        name="retention_mixer",
    )(q, k, v, z, s0.astype(F32), gn_w.astype(F32).reshape(1, 2 * d))


def _run_group(x, mod4, boff, state_gdn, state_gdn_conv, cache_fox_k, cache_fox_v, cache_fox_logf,
               cache_diff_k, cache_diff_v, state_ret, start, p):
    b, t, d = x.shape
    dk_g = d // H_G

    if state_gdn is None:
        state_gdn = jnp.zeros((b, H_G, dk_g, dk_g), F32)
        state_gdn_conv = jnp.zeros((b, CONV_W - 1, 3 * d), F32)
    q, k, v, gates, z, tail = _gdn_proj(x, mod4, 0, boff, p["gdn_w_in"], state_gdn_conv, p["gdn_conv_w"],
                                        p["gdn_a_log"], p["gdn_dt_bias"])
    o, gdn_state = _gdn_mix(q, k, v, gates, z, state_gdn, p["gdn_norm_w"])
    gdn_conv = tail[:, SUBLANES - (CONV_W - 1):, :]
    x = _out_proj(o, x, mod4, 0, boff, p["gdn_w_out"], p["ln_g"][0], p["ln_b"][0])

    prompt = cache_fox_k is None
    q, k32, v32, kb, vb, z, logf = _fox_proj(x, mod4, 1, boff, p["fox_w_in"], p["fox_b_f"], transposed=prompt)
    if prompt:
        kaug, qaugt = _fox_aug(logf)
        o = _fox_attn_prompt(q, kb, kaug, vb, qaugt, z)
    else:
        zero_c = jnp.zeros((b, 1, H_F), F32)
        p_len = cache_fox_k.shape[1]
        cum_pn, cum_pt = _cumsum_time(cache_fox_logf.astype(F32), zero_c)
        cum_n, cum_nt = _cumsum_time(logf, cum_pn[:, p_len - 1:, :])
        o = _fox_attn_sample(q, cache_fox_k.reshape(b, p_len, d), cache_fox_v.reshape(b, p_len, d), kb, vb, z,
                             cum_n, jnp.concatenate([cum_pt, cum_nt], axis=2))
    fox_k = k32.reshape(b, t, H_F, d // H_F)
    fox_v = v32.reshape(b, t, H_F, d // H_F)
    x = _out_proj(o, x, mod4, 1, boff, p["fox_w_out"], p["ln_g"][1], p["ln_b"][1])

    q, k32, v32, kb, vb, z = _diff_proj(x, mod4, 2, boff, p["diff_w_in"], transposed=prompt)
    lam = _lam_pack(p["diff_lam_q1"], p["diff_lam_k1"], p["diff_lam_q2"], p["diff_lam_k2"])
    if prompt:
        o = _diff_attn_prompt(q, kb, vb, z, p["rel_bias_table"], lam, p["diff_subln_w"])
    else:
        p_len = cache_diff_k.shape[1]
        o = _diff_attn_sample(q, cache_diff_k.reshape(b, p_len, d), cache_diff_v.reshape(b, p_len, d), kb, vb, z,
                              p["rel_bias_table"], lam, p["diff_subln_w"])
    diff_k = k32.reshape(b, t, H_D, 2, d // (2 * H_D))
    diff_v = v32.reshape(b, t, H_D, d // H_D)
    x = _out_proj(o, x, mod4, 2, boff, p["diff_w_out"], p["ln_g"][2], p["ln_b"][2])

    dk_r = d // H_R
    cos, sin_signed = _rope_tables(t, start, dk_r)
    q, k, v, z = _ret_proj(x, mod4, 3, boff, p["ret_w_in"], cos, sin_signed)
    if state_ret is None:
        state_ret = jnp.zeros((b, H_R, dk_r, 2 * d // H_R), F32)
    o, ret_state = _ret_mix(q, k, v, z, state_ret, p["ret_gn_w"])
    x = _out_proj(o, x, mod4, 3, boff, p["ret_w_out"], p["ln_g"][3], p["ln_b"][3])

    return x, gdn_state, gdn_conv, fox_k, fox_v, logf, diff_k, diff_v, ret_state


def kernel(x_prompt, x_sample, c_prompt, c_sample, state_gdn, state_gdn_conv, cache_fox_k, cache_fox_v, cache_fox_logf, cache_diff_k, cache_diff_v, state_ret, ada_w, ada_b, ln_g, ln_b, gdn_w_in, gdn_conv_w, gdn_a_log, gdn_dt_bias, gdn_norm_w, gdn_w_out, fox_w_in, fox_b_f, fox_w_out, rel_bias_table, diff_w_in, diff_lam_q1, diff_lam_k1, diff_lam_q2, diff_lam_k2, diff_subln_w, diff_w_out, ret_w_in, ret_gn_w, ret_w_out):
    p = dict(ln_g=ln_g, ln_b=ln_b, gdn_w_in=gdn_w_in, gdn_conv_w=gdn_conv_w, gdn_a_log=gdn_a_log,
             gdn_dt_bias=gdn_dt_bias, gdn_norm_w=gdn_norm_w, gdn_w_out=gdn_w_out, fox_w_in=fox_w_in,
             fox_b_f=fox_b_f, fox_w_out=fox_w_out, rel_bias_table=rel_bias_table, diff_w_in=diff_w_in,
             diff_lam_q1=diff_lam_q1, diff_lam_k1=diff_lam_k1, diff_lam_q2=diff_lam_q2, diff_lam_k2=diff_lam_k2,
             diff_subln_w=diff_subln_w, diff_w_out=diff_w_out, ret_w_in=ret_w_in, ret_gn_w=ret_gn_w,
             ret_w_out=ret_w_out)
    bp = x_prompt.shape[0]
    d = x_prompt.shape[-1]
    mod = _modulation(jnp.concatenate([c_prompt, c_sample], axis=0), ada_w, ada_b)
    mod4 = mod.reshape(mod.shape[0], mod.shape[1], 1, 3 * d)
    outs_p = _run_group(x_prompt, mod4, 0, None, None, None, None, None, None, None, None, 0, p)
    outs_s = _run_group(x_sample, mod4, bp, state_gdn, state_gdn_conv, cache_fox_k, cache_fox_v, cache_fox_logf,
                        cache_diff_k, cache_diff_v, state_ret, cache_fox_k.shape[1], p)
    return (outs_p[0], outs_s[0]) + tuple(outs_p[1:]) + tuple(outs_s[1:])
```

```python
import functools
import math

import numpy as np
import jax
import jax.numpy as jnp
from jax import lax
from jax.experimental import pallas as pl
from jax.experimental.pallas import tpu as pltpu

F32 = jnp.float32
BF16 = jnp.bfloat16

DEPTH = 4
ATTN_TQ = 512
ATTN_TK = 256
GDN_CHUNK = 64
ATTN_CHUNK = 64
DEEPNORM_ALPHA = (2.0 * DEPTH) ** 0.25
LN_EPS = 1e-5
NORM_EPS = 1e-6
NEG_INF = -1e30
LOG2E = math.log2(math.e)
H_G, H_F, H_D, H_R = 8, 16, 8, 4
CONV_W = 4
DIFF_LAYER = 2
LAMBDA_INIT = 0.8 - 0.6 * math.exp(-0.3 * DIFF_LAYER)
N_BUCKETS = 32
MAX_DISTANCE = 128
ROPE_BASE = 10000.0

LANES = 128
SUBLANES = 8
VMEM_LIMIT = 56 * 1024 * 1024


def _cparams(*sem):
    return pltpu.CompilerParams(dimension_semantics=sem, vmem_limit_bytes=VMEM_LIMIT)


def _sigmoid(x):
    return 1.0 / (1.0 + jnp.exp(-x))


def _silu(x):
    hx = 0.5 * x
    return hx + hx * jnp.tanh(hx)


def _softplus(x):
    return jnp.maximum(x, 0.0) + jnp.log(1.0 + jnp.exp(-jnp.abs(x)))


def _dot(a, b):
    return jnp.dot(a, b, preferred_element_type=F32)


def _dot_nt(a, b):
    return lax.dot_general(a, b, (((1,), (1,)), ((), ())), preferred_element_type=F32)


def _dot_tn(a, b):
    return lax.dot_general(a, b, (((0,), (0,)), ((), ())), preferred_element_type=F32)


def _split3(x):
    x1 = x.astype(BF16)
    r1 = x - x1.astype(F32)
    x2 = r1.astype(BF16)
    x3 = (r1 - x2.astype(F32)).astype(BF16)
    return x1, x2, x3


def _dot_exact_l(m01, x):
    x1, x2, x3 = _split3(x)
    return _dot(m01, x1) + _dot(m01, x2) + _dot(m01, x3)


def _dot_exact_nt(m01, x):
    x1, x2, x3 = _split3(x)
    return _dot_nt(m01, x1) + _dot_nt(m01, x2) + _dot_nt(m01, x3)


def _iota(shape, dim):
    return lax.broadcasted_iota(jnp.int32, shape, dim)


def _div_pow2(x, n):
    assert n & (n - 1) == 0
    return jnp.right_shift(x, n.bit_length() - 1)


def _row_tile(t, pref):
    return pref if t % pref == 0 else t


def _mod_kernel(c_ref, w_ref, b_ref, o_ref):
    s = _silu(c_ref[...])
    w = w_ref[...]
    s1 = s.astype(BF16)
    s2 = (s - s1.astype(F32)).astype(BF16)
    w1 = w.astype(BF16)
    w2 = (w - w1.astype(F32)).astype(BF16)
    o_ref[...] = _dot(s1, w1) + _dot(s1, w2) + _dot(s2, w1) + b_ref[...]


def _modulation(c_all, ada_w, ada_b):
    nb, d = c_all.shape
    depth, _, n = ada_w.shape
    tn = 1024
    return pl.pallas_call(
        _mod_kernel,
        grid=(depth, n // tn),
        in_specs=[pl.BlockSpec((nb, d), lambda l, j: (0, 0)),
                  pl.BlockSpec((None, d, tn), lambda l, j: (l, 0, j)),
                  pl.BlockSpec((None, 1, tn), lambda l, j: (l, 0, j))],
        out_specs=pl.BlockSpec((None, nb, tn), lambda l, j: (l, 0, j)),
        out_shape=jax.ShapeDtypeStruct((depth, nb, n), F32),
        compiler_params=_cparams("arbitrary", "arbitrary"),
        name="adaln_modulation",
    )(c_all, ada_w, ada_b.reshape(depth, 1, n))


def _mod_specs(layer, boff, d, which):
    return [pl.BlockSpec((None, None, 1, d), lambda b, i, w=w: (layer, boff + b, 0, w)) for w in which]


def _modulated(x_ref, shift_ref, scale_ref):
    return (x_ref[...] * (1.0 + scale_ref[...]) + shift_ref[...]).astype(BF16)


def _const_spec(shape):
    return pl.BlockSpec(shape, lambda b, i: (0,) * len(shape))


def _rows_spec(tm, n):
    return pl.BlockSpec((None, tm, n), lambda b, i: (b, i, 0))


def _gdn_proj_kernel(x_ref, shift_ref, scale_ref, wqkv_ref, wba_ref, wz_ref, cbuf_ref, cw_ref, avec_ref, dtvec_ref,
                     q_ref, k_ref, v_ref, gates_ref, z_ref, tail_ref, ext_ref, *, chunk):
    i = pl.program_id(1)
    tm, d = x_ref.shape
    dk = d // H_G
    u = _modulated(x_ref, shift_ref, scale_ref)

    @pl.when(i == 0)
    def _():
        ext_ref[0:SUBLANES, :] = cbuf_ref[...]

    for s in range(3):
        ext_ref[SUBLANES:SUBLANES + tm, s * d:(s + 1) * d] = _dot(u, wqkv_ref[:, s * d:(s + 1) * d])
    outs = (q_ref, k_ref, v_ref)
    for s in range(3):
        for h in range(H_G):
            c0 = s * d + h * dk
            e = ext_ref[:, c0:c0 + dk]
            acc = cw_ref[0:1, c0:c0 + dk] * e
            for j in range(1, CONV_W):
                acc = pltpu.roll(acc, 1, 0) + cw_ref[j:j + 1, c0:c0 + dk] * e
            hy = acc[SUBLANES:, :]
            y = hy + hy * jnp.tanh(hy)
            if s < 2:
                inv = lax.rsqrt(jnp.sum(y * y, axis=-1, keepdims=True) + NORM_EPS)
                y = y * (inv * (dk ** -0.5) if s == 0 else inv)
            outs[s][:, h * dk:(h + 1) * dk] = y.astype(BF16)
    tail = ext_ref[tm:tm + SUBLANES, :]
    tail_ref[...] = tail
    ext_ref[0:SUBLANES, :] = tail

    ba = _dot(u, wba_ref[...])
    g = -jnp.exp(avec_ref[...]) * _softplus(ba + dtvec_ref[...])
    r = _iota((tm, tm), 0)
    c = _iota((tm, tm), 1)
    tri = jnp.where(_div_pow2(r, chunk) == _div_pow2(c, chunk), jnp.where(r >= c, 1.0, 0.0), 0.0).astype(BF16)
    gcum = _dot_exact_l(tri, g)
    gates_ref[...] = jnp.where(_iota((tm, LANES), 1) < H_G, _sigmoid(ba), gcum)
    z_ref[...] = _dot(u, wz_ref[...]).astype(BF16)


def _gdn_proj(x, mod4, layer, boff, w_in, conv_buf, conv_w, a_log, dt_bias):
    b, t, d = x.shape
    tm = _row_tile(t, 256)
    chunk = min(GDN_CHUNK, t)
    assert t >= CONV_W - 1 and tm >= SUBLANES and tm % chunk == 0
    wqkv = w_in[:, :3 * d].astype(BF16)
    wba = jnp.pad(w_in[:, 3 * d:3 * d + 2 * H_G], ((0, 0), (0, LANES - 2 * H_G))).astype(BF16)
    wz = w_in[:, 3 * d + 2 * H_G:].astype(BF16)
    cbuf = jnp.pad(conv_buf.astype(F32), ((0, 0), (SUBLANES - (CONV_W - 1), 0), (0, 0)))
    cw = jnp.pad(0.5 * conv_w.astype(F32), ((0, SUBLANES - CONV_W), (0, 0)))
    avec = jnp.pad(a_log.astype(F32), (H_G, LANES - 2 * H_G)).reshape(1, LANES)
    dtvec = jnp.pad(dt_bias.astype(F32), (H_G, LANES - 2 * H_G)).reshape(1, LANES)
    bf16o = jax.ShapeDtypeStruct((b, t, d), BF16)
    tail_spec = pl.BlockSpec((None, SUBLANES, 3 * d), lambda bb, i: (bb, 0, 0))
    return pl.pallas_call(
        functools.partial(_gdn_proj_kernel, chunk=chunk),
        grid=(b, t // tm),
        in_specs=[_rows_spec(tm, d)] + _mod_specs(layer, boff, d, (0, 1))
                 + [_const_spec((d, 3 * d)), _const_spec((d, LANES)), _const_spec((d, d)), tail_spec,
                    _const_spec((SUBLANES, 3 * d)), _const_spec((1, LANES)), _const_spec((1, LANES))],
        out_specs=[_rows_spec(tm, d)] * 3 + [_rows_spec(tm, LANES), _rows_spec(tm, d), tail_spec],
        out_shape=[bf16o, bf16o, bf16o, jax.ShapeDtypeStruct((b, t, LANES), F32), bf16o,
                   jax.ShapeDtypeStruct((b, SUBLANES, 3 * d), F32)],
        scratch_shapes=[pltpu.VMEM((tm + SUBLANES, 3 * d), F32)],
        compiler_params=_cparams("arbitrary", "arbitrary"),
        name="gdn_in_proj",
    )(x, mod4, mod4, wqkv, wba, wz, cbuf, cw, avec, dtvec)


def _unit_lower_inverse_minus_identity(mats):
    n = mats[0].shape[0]
    r = _iota((n, n), 0)
    c = _iota((n, n), 1)

    def mm(xs, ys):
        return [_dot(x.astype(BF16), y.astype(BF16)) for x, y in zip(xs, ys)]

    base = 8
    diag = _div_pow2(r, base) == _div_pow2(c, base)
    d = [jnp.where(diag, a, 0.0) for a in mats]
    d2 = mm(d, d)
    d4 = mm(d2, d2)
    nn = [-x for x in d]
    nn = [x + y + z for x, y, z in zip(nn, d2, mm(nn, d2))]
    nn = [x + y + z for x, y, z in zip(nn, d4, mm(nn, d4))]
    m = base
    while m < n:
        pair = (_div_pow2(r, 2 * m) == _div_pow2(c, 2 * m)) & (_div_pow2(r, m) != _div_pow2(c, m))
        off = [jnp.where(pair, a, 0.0) for a in mats]
        y = [o + p for o, p in zip(off, mm(nn, off))]
        x = [p + q for p, q in zip(y, mm(y, nn))]
        nn = [p - q for p, q in zip(nn, x)]
        m *= 2
    return nn


def _gdn_kernel(q_ref, k_ref, v_ref, gates_ref, z_ref, s0_ref, nw_ref, o_ref, s_ref, *, chunk):
    ti = pl.program_id(1)
    grp, tb, d = z_ref.shape
    dk = d // H_G
    n_chunks = tb // chunk

    @pl.when(ti == 0)
    def _():
        s_ref[...] = s0_ref[...]

    ri = _iota((chunk, chunk), 0)
    ci = _iota((chunk, chunk), 1)
    eye_l = (_iota((LANES, LANES), 0) == _iota((LANES, LANES), 1)).astype(BF16)
    incl = ri >= ci
    strict = ri > ci
    nw = nw_ref[...]
    items = [(g, h) for g in range(grp) for h in range(H_G)]

    def chunk_body(cidx, carry):
        r0 = pl.multiple_of(cidx * chunk, chunk)
        rows = pl.ds(r0, chunk)
        gates = [gates_ref[g, rows, :] for g in range(grp)]
        gates_t = [_dot_exact_nt(eye_l, x) for x in gates]

        kbf = [k_ref[g, rows, h * dk:(h + 1) * dk] for g, h in items]
        qbf = [q_ref[g, rows, h * dk:(h + 1) * dk] for g, h in items]
        q = [x.astype(F32) for x in qbf]
        k = [x.astype(F32) for x in kbf]
        v = [v_ref[g, rows, h * dk:(h + 1) * dk].astype(F32) for g, h in items]
        beta = [gates[g][:, h:h + 1] for g, h in items]
        gcol = [gates[g][:, H_G + h:H_G + h + 1] for g, h in items]
        grow = [gates_t[g][H_G + h:H_G + h + 1, :] for g, h in items]
        dec_incl = [jnp.exp(jnp.where(incl, gc - gr, NEG_INF)) for gc, gr in zip(gcol, grow)]
        kb = [x * bt for x, bt in zip(k, beta)]
        a_mat = [_dot_nt(x.astype(BF16), y) for x, y in zip(kb, kbf)]
        qk = [_dot_nt(x, y) for x, y in zip(qbf, kbf)]
        a_mat = [jnp.where(strict, x * e, 0.0) for x, e in zip(a_mat, dec_incl)]
        qk = [x * e for x, e in zip(qk, dec_incl)]
        exp_g = [jnp.exp(gc) for gc in gcol]
        rhs = [jnp.concatenate([x * bt, y * e], axis=1) for x, bt, y, e in zip(v, beta, kb, exp_g)]
        nn = _unit_lower_inverse_minus_identity(a_mat)
        sol = [x + _dot(y.astype(BF16), x.astype(BF16)) for x, y in zip(rhs, nn)]
        s = [s_ref[g, h] for g, h in items]
        sb = [x.astype(BF16) for x in s]
        v_res = [x[:, :dk] - _dot(x[:, dk:].astype(BF16), y) for x, y in zip(sol, sb)]
        vrb = [x.astype(BF16) for x in v_res]
        o = [_dot((x * e).astype(BF16), y) for x, e, y in zip(q, exp_g, sb)]
        o = [x + _dot(y.astype(BF16), z) for x, y, z in zip(o, qk, vrb)]
        g_last = [gc[chunk - 1:chunk, :] for gc in gcol]
        k_dec = [(x * jnp.exp(gl - gc)).astype(BF16) for x, gl, gc in zip(k, g_last, gcol)]
        s_add = [_dot_tn(x, y) for x, y in zip(k_dec, vrb)]
        for i, (g, h) in enumerate(items):
            s_ref[g, h] = s[i] * jnp.exp(g_last[i]) + s_add[i]
            on = o[i] * lax.rsqrt(jnp.mean(o[i] * o[i], axis=-1, keepdims=True) + NORM_EPS) * nw
            zz = z_ref[g, rows, h * dk:(h + 1) * dk].astype(F32)
            o_ref[g, rows, h * dk:(h + 1) * dk] = (on * _silu(zz)).astype(BF16)
        return carry

    lax.fori_loop(0, n_chunks, chunk_body, 0)


def _gdn_mix(q, k, v, gates, z, s0, norm_w):
    b, t, d = q.shape
    dk = d // H_G
    chunk = min(GDN_CHUNK, t)
    tb = _row_tile(t, 4 * chunk)
    grp = 2 if b % 2 == 0 else 1
    nw = norm_w.astype(F32).reshape(1, dk)

    def rows(n):
        return pl.BlockSpec((grp, tb, n), lambda bb, i: (bb, i, 0))

    state_spec = pl.BlockSpec((grp, H_G, dk, dk), lambda bb, i: (bb, 0, 0, 0))
    return pl.pallas_call(
        functools.partial(_gdn_kernel, chunk=chunk),
        grid=(b // grp, t // tb),
        in_specs=[rows(d)] * 3 + [rows(LANES), rows(d), state_spec, _const_spec((1, dk))],
        out_specs=[rows(d), state_spec],
        out_shape=[jax.ShapeDtypeStruct((b, t, d), BF16), jax.ShapeDtypeStruct((b, H_G, dk, dk), F32)],
        compiler_params=_cparams("arbitrary", "arbitrary"),
        name="gdn_mixer",
    )(q, k, v, gates, z, s0.astype(F32), nw)


def _out_proj_kernel(o_ref, x_ref, gate_ref, w_ref, g_ref, b_ref, y_ref):
    h = _dot(o_ref[...], w_ref[...])
    y = DEEPNORM_ALPHA * x_ref[...] + (1.0 + gate_ref[...]) * h
    mu = jnp.mean(y, axis=-1, keepdims=True)
    yc = y - mu
    var = jnp.mean(yc * yc, axis=-1, keepdims=True)
    y_ref[...] = yc * lax.rsqrt(var + LN_EPS) * g_ref[...] + b_ref[...]


def _out_proj(o, x, mod4, layer, boff, w_out, ln_g, ln_b):
    b, t, d = x.shape
    kdim = o.shape[-1]
    tm = _row_tile(t, 512)
    return pl.pallas_call(
        _out_proj_kernel,
        grid=(b, t // tm),
        in_specs=[_rows_spec(tm, kdim), _rows_spec(tm, d)] + _mod_specs(layer, boff, d, (2,))
                 + [_const_spec((kdim, d)), _const_spec((1, d)), _const_spec((1, d))],
        out_specs=_rows_spec(tm, d),
        out_shape=jax.ShapeDtypeStruct((b, t, d), F32),
        compiler_params=_cparams("arbitrary", "arbitrary"),
        name="out_proj_postnorm",
    )(o, x, mod4, w_out.astype(BF16), ln_g.reshape(1, d), ln_b.reshape(1, d))


def _qkvz_outputs(u, w_ref, q_ref, k32_ref, v32_ref, kb_ref, vb_ref, z_ref, q_scale, transposed):
    d = u.shape[-1]
    if transposed:
        q_ref[...] = (_dot_nt(w_ref[:, 0:d], u) * q_scale).astype(BF16)
    else:
        q_ref[...] = (_dot(u, w_ref[:, 0:d]) * q_scale).astype(BF16)
    k = _dot(u, w_ref[:, d:2 * d])
    k32_ref[...] = k
    kb_ref[...] = k.astype(BF16)
    v = _dot(u, w_ref[:, 2 * d:3 * d])
    v32_ref[...] = v
    vb_ref[...] = (v.T if transposed else v).astype(BF16)
    z_ref[...] = _dot(u, w_ref[:, 3 * d:4 * d]).astype(BF16)


def _qkvz_weights(w_in, d, transposed):
    w = w_in[:, :4 * d]
    if transposed:
        w = jnp.concatenate([w[:, :d].T, w[:, d:]], axis=1)
    return w.astype(BF16)


def _qkvz_specs(b, t, d, tm, transposed):
    cols_spec = pl.BlockSpec((None, d, tm), lambda bb, i: (bb, 0, i))
    rows = _rows_spec(tm, d)
    f32o = jax.ShapeDtypeStruct((b, t, d), F32)
    bf16o = jax.ShapeDtypeStruct((b, t, d), BF16)
    bf16t = jax.ShapeDtypeStruct((b, d, t), BF16)
    if transposed:
        return [cols_spec, rows, rows, rows, cols_spec, rows], [bf16t, f32o, f32o, bf16o, bf16t, bf16o]
    return [rows] * 6, [bf16o, f32o, f32o, bf16o, bf16o, bf16o]


def _fox_proj_kernel(x_ref, shift_ref, scale_ref, w_ref, wf_ref, bf_ref,
                     q_ref, k32_ref, v32_ref, kb_ref, vb_ref, z_ref, logf_ref, *, q_scale, transposed):
    u = _modulated(x_ref, shift_ref, scale_ref)
    _qkvz_outputs(u, w_ref, q_ref, k32_ref, v32_ref, kb_ref, vb_ref, z_ref, q_scale, transposed)
    f = _dot(u, wf_ref[...])[:, :H_F] + bf_ref[...]
    logf_ref[...] = -_softplus(-f)


def _fox_proj(x, mod4, layer, boff, w_in, b_f, transposed):
    b, t, d = x.shape
    tm = _row_tile(t, 256)
    w = _qkvz_weights(w_in, d, transposed)
    wf = jnp.pad(w_in[:, 4 * d:], ((0, 0), (0, LANES - H_F))).astype(BF16)
    out_specs, out_shape = _qkvz_specs(b, t, d, tm, transposed)
    return pl.pallas_call(
        functools.partial(_fox_proj_kernel, q_scale=(d // H_F) ** -0.5 * (LOG2E if transposed else 1.0),
                          transposed=transposed),
        grid=(b, t // tm),
        in_specs=[_rows_spec(tm, d)] + _mod_specs(layer, boff, d, (0, 1))
                 + [_const_spec((d, 4 * d)), _const_spec((d, LANES)), _const_spec((1, H_F))],
        out_specs=out_specs + [_rows_spec(tm, H_F)],
        out_shape=out_shape + [jax.ShapeDtypeStruct((b, t, H_F), F32)],
        compiler_params=_cparams("arbitrary", "arbitrary"),
        name="fox_in_proj",
    )(x, mod4, mod4, w, wf, b_f.astype(F32).reshape(1, H_F))


def _cumsum_kernel(x_ref, c0_ref, cn_ref, ct_ref, *, blk):
    s, h = x_ref.shape
    tri = (_iota((blk, blk), 0) >= _iota((blk, blk), 1)).astype(BF16)
    eye_h = (_iota((h, h), 0) == _iota((h, h), 1)).astype(BF16)
    carry = c0_ref[...]
    for i in range(s // blk):
        c = _dot_exact_l(tri, x_ref[i * blk:(i + 1) * blk, :]) + carry
        cn_ref[i * blk:(i + 1) * blk, :] = c
        ct_ref[:, i * blk:(i + 1) * blk] = _dot_exact_nt(eye_h, c)
        carry = c[blk - 1:blk, :]


def _cumsum_time(x, c0):
    b, s, h = x.shape
    blk = 256 if s % 256 == 0 else s
    return pl.pallas_call(
        functools.partial(_cumsum_kernel, blk=blk),
        grid=(b,),
        in_specs=[pl.BlockSpec((None, s, h), lambda bb: (bb, 0, 0)),
                  pl.BlockSpec((None, 1, h), lambda bb: (bb, 0, 0))],
        out_specs=[pl.BlockSpec((None, s, h), lambda bb: (bb, 0, 0)),
                   pl.BlockSpec((None, h, s), lambda bb: (bb, 0, 0))],
        out_shape=[jax.ShapeDtypeStruct((b, s, h), F32), jax.ShapeDtypeStruct((b, h, s), F32)],
        compiler_params=_cparams("arbitrary"),
        name="logf_cumsum",
    )(x, c0)


AUG = LANES // H_F


def _fox_aug_kernel(x_ref, kaug_ref, qaugt_ref, *, blk):
    s, h = x_ref.shape
    tri = (_iota((blk, blk), 0) >= _iota((blk, blk), 1)).astype(BF16)
    lane_h = _iota((h, LANES), 1)
    row_h = _iota((h, LANES), 0)
    ek = [jnp.where(lane_h == row_h * AUG + part, -1.0, 0.0).astype(BF16) for part in range(3)]
    row_q = _iota((LANES, h), 0)
    col_q = _iota((LANES, h), 1)
    eq = [jnp.where(row_q == col_q * AUG + 3 + part, 1.0, 0.0).astype(BF16) for part in range(3)]
    k_slot = _iota((blk, LANES), 1) & (AUG - 1)
    k_ones = jnp.where(k_slot >= 3, jnp.where(k_slot < 6, 1.0, 0.0), 0.0)
    q_ones = jnp.where((_iota((LANES, blk), 0) & (AUG - 1)) < 3, 1.0, 0.0)
    carry = jnp.zeros((1, h), F32)
    for i in range(s // blk):
        c = _dot_exact_l(tri, x_ref[i * blk:(i + 1) * blk, :]) + carry
        parts = _split3(c * LOG2E)
        kaug = k_ones
        qaugt = q_ones
        for part in range(3):
            kaug = kaug + _dot(parts[part], ek[part])
            qaugt = qaugt + _dot_nt(eq[part], parts[part])
        kaug_ref[i * blk:(i + 1) * blk, :] = kaug.astype(BF16)
        qaugt_ref[:, i * blk:(i + 1) * blk] = qaugt.astype(BF16)
        carry = c[blk - 1:blk, :]


def _fox_aug(logf):
    b, s, h = logf.shape
    assert h * AUG == LANES and AUG >= 6
    blk = 256 if s % 256 == 0 else s
    return pl.pallas_call(
        functools.partial(_fox_aug_kernel, blk=blk),
        grid=(b,),
        in_specs=[pl.BlockSpec((None, s, h), lambda bb: (bb, 0, 0))],
        out_specs=[pl.BlockSpec((None, s, LANES), lambda bb: (bb, 0, 0)),
                   pl.BlockSpec((None, LANES, s), lambda bb: (bb, 0, 0))],
        out_shape=[jax.ShapeDtypeStruct((b, s, LANES), BF16), jax.ShapeDtypeStruct((b, LANES, s), BF16)],
        compiler_params=_cparams("arbitrary"),
        name="fox_bias_operands",
    )(logf)


SUM_ROWS = 16


def _softmax_t_probs(s, m_ref):
    m_prev = m_ref[...]
    m_new = jnp.maximum(m_prev, jnp.max(s, axis=0, keepdims=True))
    m_ref[...] = m_new
    return jnp.exp2(s - m_new).astype(BF16), jnp.exp2(m_prev - m_new)


def _softmax_t_accumulate(vt, p, alpha, acc_ref):
    vt_ext = jnp.concatenate([vt, jnp.ones((SUM_ROWS, vt.shape[1]), BF16)], axis=0)
    acc_ref[...] = alpha * acc_ref[...] + _dot(vt_ext, p)


def _softmax_t_result(acc_ref, dv):
    return acc_ref[0:dv, :] / acc_ref[dv:dv + 1, :]


def _fox_attn_kernel(qt_ref, k_ref, kaug_ref, vt_ref, qaugt_ref, z_ref, o_ref,
                     qcat_ref, s_ref, m_ref, acc_ref, *, tk):
    hp = pl.program_id(1)
    qi = pl.program_id(2)
    tq = qt_ref.shape[1]
    assert tq == 2 * tk
    dh = LANES // 2
    row = _iota((LANES, tq), 0)
    qt = qt_ref[...].astype(F32)
    qa = qaugt_ref[...].astype(F32)
    for hh in range(2):
        qcat_ref[hh, 0:LANES, :] = jnp.where(_div_pow2(row, dh) == hh, qt, 0.0).astype(BF16)
        qcat_ref[hh, LANES:2 * LANES, :] = jnp.where(_div_pow2(row, AUG) == hp * 2 + hh, qa, 0.0).astype(BF16)
    m_ref[...] = jnp.full(m_ref.shape, NEG_INF, F32)
    acc_ref[...] = jnp.zeros(acc_ref.shape, F32)
    n_full = qi * 2
    key_in = _iota((tk, tk), 0)
    qry_in = _iota((tk, tk), 1)

    def scores(j, slot, q0=0):
        k0 = pl.multiple_of(j * tk, tk)
        kcat = jnp.concatenate([k_ref[pl.ds(k0, tk), :], kaug_ref[pl.ds(k0, tk), :]], axis=1)
        for hh in range(2):
            s_ref[slot, hh, :, q0:] = _dot(kcat, qcat_ref[hh, :, q0:])

    def consume(j, slot, masked, q0=0):
        k0 = pl.multiple_of(j * tk, tk)
        for hh in range(2):
            for half in range(q0 // tk, 2):
                s = s_ref[slot, hh, :, half * tk:(half + 1) * tk]
                if masked:
                    s = jnp.where(k0 + key_in <= qi * tq + half * tk + qry_in, s, NEG_INF)
                p, alpha = _softmax_t_probs(s, m_ref.at[hh, half])
                _softmax_t_accumulate(vt_ref[hh * dh:(hh + 1) * dh, pl.ds(k0, tk)], p, alpha, acc_ref.at[hh, half])

    scores(0, 0)

    def body(i, carry):
        a = 2 * i
        scores(a + 1, 1)
        consume(a, 0, False)
        scores(a + 2, 0)
        consume(a + 1, 1, False)
        return carry

    lax.fori_loop(0, qi, body, 0)
    scores(n_full + 1, 1, q0=tk)
    consume(n_full, 0, True)
    consume(n_full + 1, 1, True, q0=tk)
    o_t = jnp.concatenate(
        [jnp.concatenate([_softmax_t_result(acc_ref.at[hh, half], dh) for half in range(2)], axis=1)
         for hh in range(2)], axis=0)
    o_ref[...] = (o_t.T * _silu(z_ref[...].astype(F32))).astype(BF16)


def _fox_attn_prompt(qt, kb, kaug, vt, qaugt, z):
    b, d, t = qt.shape
    tq, tk = ATTN_TQ, ATTN_TK
    assert t % tq == 0
    hpairs = d // LANES
    dh = LANES // 2
    return pl.pallas_call(
        functools.partial(_fox_attn_kernel, tk=tk),
        grid=(b, hpairs, t // tq),
        in_specs=[pl.BlockSpec((None, LANES, tq), lambda bb, hp, i: (bb, hp, i)),
                  pl.BlockSpec((None, t, LANES), lambda bb, hp, i: (bb, 0, hp)),
                  pl.BlockSpec((None, t, LANES), lambda bb, hp, i: (bb, 0, 0)),
                  pl.BlockSpec((None, LANES, t), lambda bb, hp, i: (bb, hp, 0)),
                  pl.BlockSpec((None, LANES, tq), lambda bb, hp, i: (bb, 0, i)),
                  pl.BlockSpec((None, tq, LANES), lambda bb, hp, i: (bb, i, hp))],
        out_specs=pl.BlockSpec((None, tq, LANES), lambda bb, hp, i: (bb, i, hp)),
        out_shape=jax.ShapeDtypeStruct((b, t, d), BF16),
        scratch_shapes=[pltpu.VMEM((2, 2 * LANES, tq), BF16), pltpu.VMEM((2, 2, tk, tq), F32),
                        pltpu.VMEM((2, 2, 1, tk), F32), pltpu.VMEM((2, 2, dh + SUM_ROWS, tk), F32)],
        compiler_params=_cparams("arbitrary", "arbitrary", "arbitrary"),
        name="fox_attention_prompt",
    )(qt, kb, kaug, vt, qaugt, z)


def _fox_decode_kernel(q_ref, kp_ref, vp_ref, kn_ref, vn_ref, z_ref, cq_ref, ckt_ref, o_ref):
    hp = pl.program_id(1)
    t = q_ref.shape[0]
    p_len = kp_ref.shape[0]
    dh = LANES // 2
    lane = _iota((t, LANES), 1)
    q = q_ref[...]
    kp = kp_ref[...].astype(BF16)
    vp = vp_ref[...].astype(BF16)
    kn = kn_ref[...]
    vn = vn_ref[...]
    cq_all = cq_ref[...]
    hlane = _iota(cq_all.shape, 1)
    row2 = _iota((2 * t, t), 0)
    causal = _iota((2 * t, t), 1) <= jnp.where(row2 >= t, row2 - t, row2)
    qm, gate = [], []
    for hh in range(2):
        h = hp * 2 + hh
        qm.append(jnp.where(_div_pow2(lane, dh) == hh, q, jnp.zeros_like(q)))
        cq = jnp.sum(jnp.where(hlane == h, cq_all, 0.0), axis=-1, keepdims=True)
        gate.append(cq - ckt_ref[pl.ds(h, 1), :])
    qm = jnp.concatenate(qm, axis=0)
    gate = jnp.concatenate(gate, axis=0)
    s_p = _dot_nt(qm, kp) + gate[:, :p_len]
    s_n = jnp.where(causal, _dot_nt(qm, kn) + gate[:, p_len:], NEG_INF)
    m = jnp.maximum(jnp.max(s_p, axis=-1, keepdims=True), jnp.max(s_n, axis=-1, keepdims=True))
    e_p = jnp.exp(s_p - m)
    e_n = jnp.exp(s_n - m)
    den = jnp.sum(e_p, axis=-1, keepdims=True) + jnp.sum(e_n, axis=-1, keepdims=True)
    o2 = (_dot(e_p.astype(BF16), vp) + _dot(e_n.astype(BF16), vn)) / den
    o = jnp.where(lane < dh, o2[:t], o2[t:])
    o_ref[...] = (o * _silu(z_ref[...].astype(F32))).astype(BF16)


def _fox_attn_sample(q, k_past, v_past, kb, vb, z, cq_new, cum_t):
    b, t, d = q.shape
    p_len = k_past.shape[1]
    hpairs = d // LANES
    new_spec = pl.BlockSpec((None, t, LANES), lambda bb, hp: (bb, 0, hp))
    past_spec = pl.BlockSpec((None, p_len, LANES), lambda bb, hp: (bb, 0, hp))
    return pl.pallas_call(
        _fox_decode_kernel,
        grid=(b, hpairs),
        in_specs=[new_spec, past_spec, past_spec, new_spec, new_spec, new_spec,
                  pl.BlockSpec((None, t, H_F), lambda bb, hp: (bb, 0, 0)),
                  pl.BlockSpec((None, H_F, p_len + t), lambda bb, hp: (bb, 0, 0))],
        out_specs=new_spec,
        out_shape=jax.ShapeDtypeStruct((b, t, d), BF16),
        compiler_params=_cparams("arbitrary", "arbitrary"),
        name="fox_attention_sample",
    )(q, k_past, v_past, kb, vb, z, cq_new, cum_t)


def _diff_proj_kernel(x_ref, shift_ref, scale_ref, w_ref, q_ref, k32_ref, v32_ref, kb_ref, vb_ref, z_ref,
                      *, q_scale, transposed):
    u = _modulated(x_ref, shift_ref, scale_ref)
    _qkvz_outputs(u, w_ref, q_ref, k32_ref, v32_ref, kb_ref, vb_ref, z_ref, q_scale, transposed)


def _diff_proj(x, mod4, layer, boff, w_in, transposed):
    b, t, d = x.shape
    tm = _row_tile(t, 256)
    out_specs, out_shape = _qkvz_specs(b, t, d, tm, transposed)
    return pl.pallas_call(
        functools.partial(_diff_proj_kernel, q_scale=(d // (2 * H_D)) ** -0.5 * (LOG2E if transposed else 1.0),
                          transposed=transposed),
        grid=(b, t // tm),
        in_specs=[_rows_spec(tm, d)] + _mod_specs(layer, boff, d, (0, 1)) + [_const_spec((d, 4 * d))],
        out_specs=out_specs,
        out_shape=out_shape,
        compiler_params=_cparams("arbitrary", "arbitrary"),
        name="diff_in_proj",
    )(x, mod4, mod4, _qkvz_weights(w_in, d, transposed))


def _t5_thresholds():
    nb = N_BUCKETS // 2
    max_exact = nb // 2
    steps = nb - max_exact
    ratio = MAX_DISTANCE // max_exact
    out = []
    for kk in range(1, nb - max_exact):
        target = max_exact ** steps * ratio ** kk
        n = max_exact
        while n ** steps < target:
            n += 1
        out.append(n)
    return nb, max_exact, out


def _bias_kernel(tbl_ref, o_ref, *, q0, k0, keys_on_rows):
    h = pl.program_id(0)
    shape = o_ref.shape
    kdim, qdim = (0, 1) if keys_on_rows else (1, 0)
    rel = (k0 + _iota(shape, kdim)) - (q0 + _iota(shape, qdim))
    nb, max_exact, thr = _t5_thresholds()
    n = jnp.abs(rel)
    large = jnp.full(shape, max_exact, jnp.int32)
    for tval in thr:
        large = large + (n >= tval).astype(jnp.int32)
    bucket = jnp.where(rel > 0, nb, 0) + jnp.where(n < max_exact, n, large)
    acc = jnp.zeros(shape, F32)
    for bkt in range(N_BUCKETS):
        acc = jnp.where(bucket == bkt, tbl_ref[bkt * H_D + h], acc)
    if keys_on_rows:
        acc = (acc - tbl_ref[(nb - 1) * H_D + h]) * LOG2E
    o_ref[...] = acc


def _bias_tile(rel_table, q0, nq, k0, nk, keys_on_rows=False):
    shape = (nk, nq) if keys_on_rows else (nq, nk)
    return pl.pallas_call(
        functools.partial(_bias_kernel, q0=q0, k0=k0, keys_on_rows=keys_on_rows),
        grid=(H_D,),
        in_specs=[pl.BlockSpec(memory_space=pltpu.SMEM)],
        out_specs=pl.BlockSpec((None,) + shape, lambda h: (h, 0, 0)),
        out_shape=jax.ShapeDtypeStruct((H_D,) + shape, F32),
        compiler_params=_cparams("arbitrary"),
        name="t5_bias_tile",
    )(rel_table.astype(F32).reshape(N_BUCKETS * H_D))


def _diff_lambda(lam_ref):
    lam = lam_ref[...]
    s1 = jnp.sum(lam[0:1, :] * lam[1:2, :], axis=-1, keepdims=True)
    s2 = jnp.sum(lam[2:3, :] * lam[3:4, :], axis=-1, keepdims=True)
    return jnp.exp(s1) - jnp.exp(s2) + LAMBDA_INIT


def _diff_epilogue(o, z_ref, subln_ref, o_ref):
    on = o * lax.rsqrt(jnp.mean(o * o, axis=-1, keepdims=True) + NORM_EPS) * subln_ref[...]
    on = on * (1.0 - LAMBDA_INIT)
    o_ref[...] = (on * _silu(z_ref[...].astype(F32))).astype(BF16)


def _diff_attn_kernel(tbl_ref, qt_ref, k_ref, vt_ref, z_ref, biasm_ref, bias0_ref, bias1_ref, lam_ref, subln_ref,
                      o_ref, qcat_ref, s_ref, m_ref, acc_ref, *, tk):
    h = pl.program_id(1)
    qi = pl.program_id(2)
    tq = qt_ref.shape[1]
    assert tq == 2 * tk
    dh = LANES // 2
    row = _iota((LANES, tq), 0)
    qt = qt_ref[...].astype(F32)
    nb, _, _ = _t5_thresholds()
    far = _split3(jnp.full((LANES, tq), tbl_ref[(nb - 1) * H_D + h], F32) * LOG2E)
    far_rows = jnp.zeros((LANES, tq), F32)
    for part in range(3):
        far_rows = jnp.where(row == part, far[part].astype(F32), far_rows)
    for br in range(2):
        qcat_ref[br, 0:LANES, :] = jnp.where(_div_pow2(row, dh) == br, qt, 0.0).astype(BF16)
        qcat_ref[br, LANES:2 * LANES, :] = far_rows.astype(BF16)
    ones_aug = jnp.where(_iota((tk, LANES), 1) < 3, 1.0, 0.0).astype(BF16)
    m_ref[...] = jnp.full(m_ref.shape, NEG_INF, F32)
    acc_ref[...] = jnp.zeros(acc_ref.shape, F32)
    key_in = _iota((tk, tk), 0)
    qry_in = _iota((tk, tk), 1)

    def scores(j, slot, q0=0):
        k0 = pl.multiple_of(j * tk, tk)
        kcat = jnp.concatenate([k_ref[pl.ds(k0, tk), :], ones_aug], axis=1)
        for br in range(2):
            s_ref[slot, br, :, q0:] = _dot(kcat, qcat_ref[br, :, q0:])

    def consume(j, slot, bias_ref=None, key_off=None, q0=0):
        k0 = pl.multiple_of(j * tk, tk)
        for br in range(2):
            for half in range(q0 // tk, 2):
                lanes = slice(half * tk, (half + 1) * tk)
                s = s_ref[slot, br, :, lanes]
                if bias_ref is not None:
                    s = s + bias_ref[:, lanes]
                if key_off is not None:
                    s = jnp.where(_div_pow2(key_off + key_in, ATTN_CHUNK)
                                  <= _div_pow2(half * tk + qry_in, ATTN_CHUNK), s, NEG_INF)
                p, alpha = _softmax_t_probs(s, m_ref.at[br, half])
                _softmax_t_accumulate(vt_ref[:, pl.ds(k0, tk)], p, alpha, acc_ref.at[br, half])

    def pair(a, second_bias_ref=None):
        scores(a + 1, 1)
        consume(a, 0)
        scores(a + 2, 0)
        consume(a + 1, 1, bias_ref=second_bias_ref)

    scores(0, 0)

    def body(i, carry):
        pair(2 * i)
        return carry

    lax.fori_loop(0, jnp.maximum(qi - 1, 0), body, 0)

    @pl.when(qi >= 1)
    def _():
        pair(2 * qi - 2, second_bias_ref=biasm_ref)

    a = 2 * qi
    scores(a + 1, 1, q0=tk)
    consume(a, 0, bias_ref=bias0_ref, key_off=0)
    consume(a + 1, 1, bias_ref=bias1_ref, key_off=tk, q0=tk)
    branch = [jnp.concatenate([_softmax_t_result(acc_ref.at[br, half], LANES) for half in range(2)], axis=1)
              for br in range(2)]
    o_t = branch[0] - _diff_lambda(lam_ref) * branch[1]
    _diff_epilogue(o_t.T, z_ref, subln_ref, o_ref)


def _lam_pack(lam_q1, lam_k1, lam_q2, lam_k2):
    rows = jnp.stack([lam_q1, lam_k1, lam_q2, lam_k2]).astype(F32)
    return jnp.pad(rows, ((0, SUBLANES - 4), (0, LANES - rows.shape[1])))


def _diff_attn_prompt(qt, kb, vt, z, rel_table, lam, subln_w):
    b, d, t = qt.shape
    tq, tk = ATTN_TQ, ATTN_TK
    assert t % tq == 0 and tk % ATTN_CHUNK == 0 and tk >= MAX_DISTANCE
    biasm = _bias_tile(rel_table, tk, tq, 0, tk, keys_on_rows=True)
    bias0 = _bias_tile(rel_table, 0, tq, 0, tk, keys_on_rows=True)
    bias1 = _bias_tile(rel_table, 0, tq, tk, tk, keys_on_rows=True)
    rows_spec = pl.BlockSpec((None, tq, LANES), lambda bb, h, i: (bb, i, h))
    bias_spec = pl.BlockSpec((None, tk, tq), lambda bb, h, i: (h, 0, 0))
    return pl.pallas_call(
        functools.partial(_diff_attn_kernel, tk=tk),
        grid=(b, H_D, t // tq),
        in_specs=[pl.BlockSpec(memory_space=pltpu.SMEM),
                  pl.BlockSpec((None, LANES, tq), lambda bb, h, i: (bb, h, i)),
                  pl.BlockSpec((None, t, LANES), lambda bb, h, i: (bb, 0, h)),
                  pl.BlockSpec((None, LANES, t), lambda bb, h, i: (bb, h, 0)),
                  rows_spec, bias_spec, bias_spec, bias_spec,
                  pl.BlockSpec((SUBLANES, LANES), lambda bb, h, i: (0, 0)),
                  pl.BlockSpec((1, LANES), lambda bb, h, i: (0, 0))],
        out_specs=rows_spec,
        out_shape=jax.ShapeDtypeStruct((b, t, d), BF16),
        scratch_shapes=[pltpu.VMEM((2, 2 * LANES, tq), BF16), pltpu.VMEM((2, 2, tk, tq), F32),
                        pltpu.VMEM((2, 2, 1, tk), F32), pltpu.VMEM((2, 2, LANES + SUM_ROWS, tk), F32)],
        compiler_params=_cparams("arbitrary", "arbitrary", "arbitrary"),
        name="diff_attention_prompt",
    )(rel_table.astype(F32).reshape(N_BUCKETS * H_D), qt, kb, vt, z, biasm, bias0, bias1, lam,
      subln_w.astype(F32).reshape(1, LANES))


def _diff_decode_kernel(q_ref, kp_ref, vp_ref, kn_ref, vn_ref, z_ref, bias_ref, lam_ref, subln_ref, o_ref,
                        *, p_len):
    t = q_ref.shape[0]
    dh = LANES // 2
    lane = _iota((t, LANES), 1)
    q = q_ref[...]
    kp = kp_ref[...].astype(BF16)
    vp = vp_ref[...].astype(BF16)
    kn = kn_ref[...]
    vn = vn_ref[...]
    bias = jnp.concatenate([bias_ref[...], bias_ref[...]], axis=0)
    row_p = _iota((2 * t, p_len), 0)
    row_n = _iota((2 * t, t), 0)
    qp_chunk = _div_pow2(p_len + jnp.where(row_p >= t, row_p - t, row_p), ATTN_CHUNK)
    qn_chunk = _div_pow2(p_len + jnp.where(row_n >= t, row_n - t, row_n), ATTN_CHUNK)
    kp_chunk = _div_pow2(_iota((2 * t, p_len), 1), ATTN_CHUNK)
    kn_chunk = _div_pow2(p_len + _iota((2 * t, t), 1), ATTN_CHUNK)
    qm = jnp.concatenate([jnp.where(_div_pow2(lane, dh) == br, q, jnp.zeros_like(q)) for br in range(2)], axis=0)
    s_p = jnp.where(kp_chunk <= qp_chunk, _dot_nt(qm, kp) + bias[:, :p_len], NEG_INF)
    s_n = jnp.where(kn_chunk <= qn_chunk, _dot_nt(qm, kn) + bias[:, p_len:], NEG_INF)
    m = jnp.maximum(jnp.max(s_p, axis=-1, keepdims=True), jnp.max(s_n, axis=-1, keepdims=True))
    e_p = jnp.exp(s_p - m)
    e_n = jnp.exp(s_n - m)
    den = jnp.sum(e_p, axis=-1, keepdims=True) + jnp.sum(e_n, axis=-1, keepdims=True)
    o2 = (_dot(e_p.astype(BF16), vp) + _dot(e_n.astype(BF16), vn)) / den
    o = o2[:t] - _diff_lambda(lam_ref) * o2[t:]
    _diff_epilogue(o, z_ref, subln_ref, o_ref)


def _diff_attn_sample(q, k_past, v_past, kb, vb, z, rel_table, lam, subln_w):
    b, t, d = q.shape
    p_len = k_past.shape[1]
    bias = _bias_tile(rel_table, p_len, t, 0, p_len + t)
    new_spec = pl.BlockSpec((None, t, LANES), lambda bb, h: (bb, 0, h))
    past_spec = pl.BlockSpec((None, p_len, LANES), lambda bb, h: (bb, 0, h))
    return pl.pallas_call(
        functools.partial(_diff_decode_kernel, p_len=p_len),
        grid=(b, H_D),
        in_specs=[new_spec, past_spec, past_spec, new_spec, new_spec, new_spec,
                  pl.BlockSpec((None, t, p_len + t), lambda bb, h: (h, 0, 0)),
                  pl.BlockSpec((SUBLANES, LANES), lambda bb, h: (0, 0)),
                  pl.BlockSpec((1, LANES), lambda bb, h: (0, 0))],
        out_specs=new_spec,
        out_shape=jax.ShapeDtypeStruct((b, t, d), BF16),
        compiler_params=_cparams("arbitrary", "arbitrary"),
        name="diff_attention_sample",
    )(q, k_past, v_past, kb, vb, z, bias, lam, subln_w.astype(F32).reshape(1, LANES))


def _rope_kernel(inv_ref, cos_ref, sin_ref, *, start):
    t, w = cos_ref.shape
    pos = (start + pl.program_id(0) * t + _iota((t, w), 0)).astype(F32)
    ang = pos * inv_ref[...]
    even = (_iota((t, w), 1) & 1) == 0
    cos_ref[...] = jnp.cos(ang)
    sn = jnp.sin(ang)
    sin_ref[...] = jnp.where(even, -sn, sn)


def _rope_tables(t, start, dk):
    inv_half = np.power(np.float32(ROPE_BASE), -np.arange(0, dk, 2, dtype=np.float32) / np.float32(dk))
    inv = jnp.asarray(np.repeat(inv_half.astype(np.float32), 2).reshape(1, dk))
    tt = _row_tile(t, 512)
    return pl.pallas_call(
        functools.partial(_rope_kernel, start=start),
        grid=(t // tt,),
        in_specs=[pl.BlockSpec((1, dk), lambda i: (0, 0))],
        out_specs=[pl.BlockSpec((tt, dk), lambda i: (i, 0))] * 2,
        out_shape=[jax.ShapeDtypeStruct((t, dk), F32)] * 2,
        compiler_params=_cparams("arbitrary"),
        name="rope_tables",
    )(inv)


def _rotate_pairs(x, cos, sin_signed):
    slabs = []
    for c0 in range(0, x.shape[-1], LANES):
        xs = x[:, c0:c0 + LANES]
        even = (_iota(xs.shape, 1) & 1) == 0
        slabs.append(jnp.where(even, pltpu.roll(xs, LANES - 1, 1), pltpu.roll(xs, 1, 1)))
    return x * cos + jnp.concatenate(slabs, axis=1) * sin_signed


def _ret_proj_kernel(x_ref, shift_ref, scale_ref, w_ref, cos_ref, sin_ref, q_ref, k_ref, v_ref, z_ref,
                     *, q_scale):
    u = _modulated(x_ref, shift_ref, scale_ref)
    d = x_ref.shape[-1]
    dk = cos_ref.shape[-1]
    cos = cos_ref[...]
    sn = sin_ref[...]
    for h in range(d // dk):
        qh = _dot(u, w_ref[:, h * dk:(h + 1) * dk])
        q_ref[:, h * dk:(h + 1) * dk] = (_rotate_pairs(qh, cos, sn) * q_scale).astype(BF16)
        kh = _dot(u, w_ref[:, d + h * dk:d + (h + 1) * dk])
        k_ref[:, h * dk:(h + 1) * dk] = _rotate_pairs(kh, cos, sn).astype(BF16)
    for s in range(2):
        v_ref[:, s * d:(s + 1) * d] = _dot(u, w_ref[:, (2 + s) * d:(3 + s) * d]).astype(BF16)
        z_ref[:, s * d:(s + 1) * d] = _dot(u, w_ref[:, (4 + s) * d:(5 + s) * d]).astype(BF16)


def _ret_proj(x, mod4, layer, boff, w_in, cos, sin_signed):
    b, t, d = x.shape
    dk = d // H_R
    tm = _row_tile(t, 256)
    tab_spec = pl.BlockSpec((tm, dk), lambda bb, i: (i, 0))
    return pl.pallas_call(
        functools.partial(_ret_proj_kernel, q_scale=dk ** -0.5),
        grid=(b, t // tm),
        in_specs=[_rows_spec(tm, d)] + _mod_specs(layer, boff, d, (0, 1)) + [_const_spec((d, 6 * d)), tab_spec, tab_spec],
        out_specs=[_rows_spec(tm, d), _rows_spec(tm, d), _rows_spec(tm, 2 * d), _rows_spec(tm, 2 * d)],
        out_shape=[jax.ShapeDtypeStruct((b, t, d), BF16), jax.ShapeDtypeStruct((b, t, d), BF16),
                   jax.ShapeDtypeStruct((b, t, 2 * d), BF16), jax.ShapeDtypeStruct((b, t, 2 * d), BF16)],
        compiler_params=_cparams("arbitrary", "arbitrary"),
        name="ret_in_proj",
    )(x, mod4, mod4, w_in.astype(BF16), cos, sin_signed)


def _ret_kernel(q_ref, k_ref, v_ref, z_ref, s0_ref, gn_ref, o_ref, s_ref, intra_ref):
    ti = pl.program_id(1)
    lr = q_ref.shape[0]
    dk = q_ref.shape[-1] // H_R
    dv = v_ref.shape[-1] // H_R
    heads = range(H_R)
    log_gamma = [math.log1p(-(2.0 ** (-5.0 - h))) for h in heads]

    @pl.when(ti == 0)
    def _():
        s_ref[...] = s0_ref[...]
        rel = (_iota((lr, lr), 0) - _iota((lr, lr), 1)).astype(F32)
        for h in heads:
            intra_ref[h] = jnp.where(rel >= 0, jnp.exp(log_gamma[h] * jnp.maximum(rel, 0.0)), 0.0)

    idx = _iota((lr, 1), 0).astype(F32)
    for h in heads:
        q_dec = jnp.exp(log_gamma[h] * (idx + 1.0))
        k_dec = jnp.exp(log_gamma[h] * (lr - 1.0 - idx))
        qh = q_ref[:, h * dk:(h + 1) * dk]
        kh = k_ref[:, h * dk:(h + 1) * dk]
        vh = v_ref[:, h * dv:(h + 1) * dv]
        s = s_ref[h]
        att = _dot_nt(qh, kh) * intra_ref[h]
        o = _dot(att.astype(BF16), vh) + _dot(qh, s.astype(BF16)) * q_dec
        s_ref[h] = s * math.exp(log_gamma[h] * lr) + _dot_tn((kh.astype(F32) * k_dec).astype(BF16), vh)
        mu = jnp.mean(o, axis=-1, keepdims=True)
        oc = o - mu
        var = jnp.mean(oc * oc, axis=-1, keepdims=True)
        on = oc * lax.rsqrt(var + LN_EPS) * gn_ref[:, h * dv:(h + 1) * dv]
        zz = z_ref[:, h * dv:(h + 1) * dv].astype(F32)
        o_ref[:, h * dv:(h + 1) * dv] = (on * _silu(zz)).astype(BF16)


def _ret_mix(q, k, v, z, s0, gn_w):
    b, t, d = q.shape
    dk = d // H_R
    dv = v.shape[-1] // H_R
    lr = _row_tile(t, 256)
    state_spec = pl.BlockSpec((None, H_R, dk, dv), lambda bb, i: (bb, 0, 0, 0))
    return pl.pallas_call(
        _ret_kernel,
        grid=(b, t // lr),
        in_specs=[_rows_spec(lr, d), _rows_spec(lr, d), _rows_spec(lr, 2 * d), _rows_spec(lr, 2 * d), state_spec,
                  _const_spec((1, 2 * d))],
        out_specs=[_rows_spec(lr, 2 * d), state_spec],
        out_shape=[jax.ShapeDtypeStruct((b, t, 2 * d), BF16), jax.ShapeDtypeStruct((b, H_R, dk, dv), F32)],
        scratch_shapes=[pltpu.VMEM((H_R, lr, lr), F32)],
        compiler_params=_cparams("arbitrary", "arbitrary"),
        name="retention_mixer",
    )(q, k, v, z, s0.astype(F32), gn_w.astype(F32).reshape(1, 2 * d))


def _run_group(x, mod4, boff, state_gdn, state_gdn_conv, cache_fox_k, cache_fox_v, cache_fox_logf,
               cache_diff_k, cache_diff_v, state_ret, start, p):
    b, t, d = x.shape
    dk_g = d // H_G

    if state_gdn is None:
        state_gdn = jnp.zeros((b, H_G, dk_g, dk_g), F32)
        state_gdn_conv = jnp.zeros((b, CONV_W - 1, 3 * d), F32)
    q, k, v, gates, z, tail = _gdn_proj(x, mod4, 0, boff, p["gdn_w_in"], state_gdn_conv, p["gdn_conv_w"],
                                        p["gdn_a_log"], p["gdn_dt_bias"])
    o, gdn_state = _gdn_mix(q, k, v, gates, z, state_gdn, p["gdn_norm_w"])
    gdn_conv = tail[:, SUBLANES - (CONV_W - 1):, :]
    x = _out_proj(o, x, mod4, 0, boff, p["gdn_w_out"], p["ln_g"][0], p["ln_b"][0])

    prompt = cache_fox_k is None
    q, k32, v32, kb, vb, z, logf = _fox_proj(x, mod4, 1, boff, p["fox_w_in"], p["fox_b_f"], transposed=prompt)
    if prompt:
        kaug, qaugt = _fox_aug(logf)
        o = _fox_attn_prompt(q, kb, kaug, vb, qaugt, z)
    else:
        zero_c = jnp.zeros((b, 1, H_F), F32)
        p_len = cache_fox_k.shape[1]
        cum_pn, cum_pt = _cumsum_time(cache_fox_logf.astype(F32), zero_c)
        cum_n, cum_nt = _cumsum_time(logf, cum_pn[:, p_len - 1:, :])
        o = _fox_attn_sample(q, cache_fox_k.reshape(b, p_len, d), cache_fox_v.reshape(b, p_len, d), kb, vb, z,
                             cum_n, jnp.concatenate([cum_pt, cum_nt], axis=2))
    fox_k = k32.reshape(b, t, H_F, d // H_F)
    fox_v = v32.reshape(b, t, H_F, d // H_F)
    x = _out_proj(o, x, mod4, 1, boff, p["fox_w_out"], p["ln_g"][1], p["ln_b"][1])

    q, k32, v32, kb, vb, z = _diff_proj(x, mod4, 2, boff, p["diff_w_in"], transposed=prompt)
    lam = _lam_pack(p["diff_lam_q1"], p["diff_lam_k1"], p["diff_lam_q2"], p["diff_lam_k2"])
    if prompt:
        o = _diff_attn_prompt(q, kb, vb, z, p["rel_bias_table"], lam, p["diff_subln_w"])
    else:
        p_len = cache_diff_k.shape[1]
        o = _diff_attn_sample(q, cache_diff_k.reshape(b, p_len, d), cache_diff_v.reshape(b, p_len, d), kb, vb, z,
                              p["rel_bias_table"], lam, p["diff_subln_w"])
    diff_k = k32.reshape(b, t, H_D, 2, d // (2 * H_D))
    diff_v = v32.reshape(b, t, H_D, d // H_D)
    x = _out_proj(o, x, mod4, 2, boff, p["diff_w_out"], p["ln_g"][2], p["ln_b"][2])

    dk_r = d // H_R
    cos, sin_signed = _rope_tables(t, start, dk_r)
    q, k, v, z = _ret_proj(x, mod4, 3, boff, p["ret_w_in"], cos, sin_signed)
    if state_ret is None:
        state_ret = jnp.zeros((b, H_R, dk_r, 2 * d // H_R), F32)
    o, ret_state = _ret_mix(q, k, v, z, state_ret, p["ret_gn_w"])
    x = _out_proj(o, x, mod4, 3, boff, p["ret_w_out"], p["ln_g"][3], p["ln_b"][3])

    return x, gdn_state, gdn_conv, fox_k, fox_v, logf, diff_k, diff_v, ret_state


def kernel(x_prompt, x_sample, c_prompt, c_sample, state_gdn, state_gdn_conv, cache_fox_k, cache_fox_v, cache_fox_logf, cache_diff_k, cache_diff_v, state_ret, ada_w, ada_b, ln_g, ln_b, gdn_w_in, gdn_conv_w, gdn_a_log, gdn_dt_bias, gdn_norm_w, gdn_w_out, fox_w_in, fox_b_f, fox_w_out, rel_bias_table, diff_w_in, diff_lam_q1, diff_lam_k1, diff_lam_q2, diff_lam_k2, diff_subln_w, diff_w_out, ret_w_in, ret_gn_w, ret_w_out):
    p = dict(ln_g=ln_g, ln_b=ln_b, gdn_w_in=gdn_w_in, gdn_conv_w=gdn_conv_w, gdn_a_log=gdn_a_log,
             gdn_dt_bias=gdn_dt_bias, gdn_norm_w=gdn_norm_w, gdn_w_out=gdn_w_out, fox_w_in=fox_w_in,
             fox_b_f=fox_b_f, fox_w_out=fox_w_out, rel_bias_table=rel_bias_table, diff_w_in=diff_w_in,
             diff_lam_q1=diff_lam_q1, diff_lam_k1=diff_lam_k1, diff_lam_q2=diff_lam_q2, diff_lam_k2=diff_lam_k2,
             diff_subln_w=diff_subln_w, diff_w_out=diff_w_out, ret_w_in=ret_w_in, ret_gn_w=ret_gn_w,
             ret_w_out=ret_w_out)
    bp = x_prompt.shape[0]
    d = x_prompt.shape[-1]
    mod = _modulation(jnp.concatenate([c_prompt, c_sample], axis=0), ada_w, ada_b)
    mod4 = mod.reshape(mod.shape[0], mod.shape[1], 1, 3 * d)
    outs_p = _run_group(x_prompt, mod4, 0, None, None, None, None, None, None, None, None, 0, p)
    outs_s = _run_group(x_sample, mod4, bp, state_gdn, state_gdn_conv, cache_fox_k, cache_fox_v, cache_fox_logf,
                        cache_diff_k, cache_diff_v, state_ret, cache_fox_k.shape[1], p)
    return (outs_p[0], outs_s[0]) + tuple(outs_p[1:]) + tuple(outs_s[1:])
```

```python
import functools
import math

import numpy as np
import jax
import jax.numpy as jnp
from jax import lax
from jax.experimental import pallas as pl
from jax.experimental.pallas import tpu as pltpu

F32 = jnp.float32
BF16 = jnp.bfloat16

DEPTH = 4
ATTN_TQ = 512
ATTN_TK = 256
GDN_CHUNK = 64
ATTN_CHUNK = 64
DEEPNORM_ALPHA = (2.0 * DEPTH) ** 0.25
LN_EPS = 1e-5
NORM_EPS = 1e-6
NEG_INF = -1e30
LOG2E = math.log2(math.e)
H_G, H_F, H_D, H_R = 8, 16, 8, 4
CONV_W = 4
DIFF_LAYER = 2
LAMBDA_INIT = 0.8 - 0.6 * math.exp(-0.3 * DIFF_LAYER)
N_BUCKETS = 32
MAX_DISTANCE = 128
ROPE_BASE = 10000.0

LANES = 128
SUBLANES = 8
VMEM_LIMIT = 56 * 1024 * 1024


def _cparams(*sem):
    return pltpu.CompilerParams(dimension_semantics=sem, vmem_limit_bytes=VMEM_LIMIT)


def _sigmoid(x):
    return 1.0 / (1.0 + jnp.exp(-x))


def _silu(x):
    hx = 0.5 * x
    return hx + hx * jnp.tanh(hx)


def _softplus(x):
    return jnp.maximum(x, 0.0) + jnp.log(1.0 + jnp.exp(-jnp.abs(x)))


def _dot(a, b):
    return jnp.dot(a, b, preferred_element_type=F32)


def _dot_nt(a, b):
    return lax.dot_general(a, b, (((1,), (1,)), ((), ())), preferred_element_type=F32)


def _dot_tn(a, b):
    return lax.dot_general(a, b, (((0,), (0,)), ((), ())), preferred_element_type=F32)


def _split3(x):
    x1 = x.astype(BF16)
    r1 = x - x1.astype(F32)
    x2 = r1.astype(BF16)
    x3 = (r1 - x2.astype(F32)).astype(BF16)
    return x1, x2, x3


def _dot_exact_l(m01, x):
    x1, x2, x3 = _split3(x)
    return _dot(m01, x1) + _dot(m01, x2) + _dot(m01, x3)


def _dot_exact_nt(m01, x):
    x1, x2, x3 = _split3(x)
    return _dot_nt(m01, x1) + _dot_nt(m01, x2) + _dot_nt(m01, x3)


def _iota(shape, dim):
    return lax.broadcasted_iota(jnp.int32, shape, dim)


def _div_pow2(x, n):
    assert n & (n - 1) == 0
    return jnp.right_shift(x, n.bit_length() - 1)


def _row_tile(t, pref):
    return pref if t % pref == 0 else t


def _mod_kernel(c_ref, w_ref, b_ref, o_ref):
    s = _silu(c_ref[...])
    w = w_ref[...]
    s1 = s.astype(BF16)
    s2 = (s - s1.astype(F32)).astype(BF16)
    w1 = w.astype(BF16)
    w2 = (w - w1.astype(F32)).astype(BF16)
    o_ref[...] = _dot(s1, w1) + _dot(s1, w2) + _dot(s2, w1) + b_ref[...]


def _modulation(c_all, ada_w, ada_b):
    nb, d = c_all.shape
    depth, _, n = ada_w.shape
    tn = 1024
    return pl.pallas_call(
        _mod_kernel,
        grid=(depth, n // tn),
        in_specs=[pl.BlockSpec((nb, d), lambda l, j: (0, 0)),
                  pl.BlockSpec((None, d, tn), lambda l, j: (l, 0, j)),
                  pl.BlockSpec((None, 1, tn), lambda l, j: (l, 0, j))],
        out_specs=pl.BlockSpec((None, nb, tn), lambda l, j: (l, 0, j)),
        out_shape=jax.ShapeDtypeStruct((depth, nb, n), F32),
        compiler_params=_cparams("arbitrary", "arbitrary"),
        name="adaln_modulation",
    )(c_all, ada_w, ada_b.reshape(depth, 1, n))


def _mod_specs(layer, boff, d, which):
    return [pl.BlockSpec((None, None, 1, d), lambda b, i, w=w: (layer, boff + b, 0, w)) for w in which]


def _modulated(x_ref, shift_ref, scale_ref):
    return (x_ref[...] * (1.0 + scale_ref[...]) + shift_ref[...]).astype(BF16)


def _const_spec(shape):
    return pl.BlockSpec(shape, lambda b, i: (0,) * len(shape))


def _rows_spec(tm, n):
    return pl.BlockSpec((None, tm, n), lambda b, i: (b, i, 0))


def _gdn_proj_kernel(x_ref, shift_ref, scale_ref, wqkv_ref, wba_ref, wz_ref, cbuf_ref, cw_ref, avec_ref, dtvec_ref,
                     q_ref, k_ref, v_ref, gates_ref, z_ref, tail_ref, ext_ref, *, chunk):
    i = pl.program_id(1)
    tm, d = x_ref.shape
    dk = d // H_G
    u = _modulated(x_ref, shift_ref, scale_ref)

    @pl.when(i == 0)
    def _():
        ext_ref[0:SUBLANES, :] = cbuf_ref[...]

    for s in range(3):
        ext_ref[SUBLANES:SUBLANES + tm, s * d:(s + 1) * d] = _dot(u, wqkv_ref[:, s * d:(s + 1) * d])
    outs = (q_ref, k_ref, v_ref)
    for s in range(3):
        for h in range(H_G):
            c0 = s * d + h * dk
            e = ext_ref[:, c0:c0 + dk]
            acc = cw_ref[0:1, c0:c0 + dk] * e
            for j in range(1, CONV_W):
                acc = pltpu.roll(acc, 1, 0) + cw_ref[j:j + 1, c0:c0 + dk] * e
            hy = acc[SUBLANES:, :]
            y = hy + hy * jnp.tanh(hy)
            if s < 2:
                inv = lax.rsqrt(jnp.sum(y * y, axis=-1, keepdims=True) + NORM_EPS)
                y = y * (inv * (dk ** -0.5) if s == 0 else inv)
            outs[s][:, h * dk:(h + 1) * dk] = y.astype(BF16)
    tail = ext_ref[tm:tm + SUBLANES, :]
    tail_ref[...] = tail
    ext_ref[0:SUBLANES, :] = tail

    ba = _dot(u, wba_ref[...])
    g = -jnp.exp(avec_ref[...]) * _softplus(ba + dtvec_ref[...])
    r = _iota((tm, tm), 0)
    c = _iota((tm, tm), 1)
    tri = jnp.where(_div_pow2(r, chunk) == _div_pow2(c, chunk), jnp.where(r >= c, 1.0, 0.0), 0.0).astype(BF16)
    gcum = _dot_exact_l(tri, g)
    gates_ref[...] = jnp.where(_iota((tm, LANES), 1) < H_G, _sigmoid(ba), gcum)
    z_ref[...] = _dot(u, wz_ref[...]).astype(BF16)


def _gdn_proj(x, mod4, layer, boff, w_in, conv_buf, conv_w, a_log, dt_bias):
    b, t, d = x.shape
    tm = _row_tile(t, 256)
    chunk = min(GDN_CHUNK, t)
    assert t >= CONV_W - 1 and tm >= SUBLANES and tm % chunk == 0
    wqkv = w_in[:, :3 * d].astype(BF16)
    wba = jnp.pad(w_in[:, 3 * d:3 * d + 2 * H_G], ((0, 0), (0, LANES - 2 * H_G))).astype(BF16)
    wz = w_in[:, 3 * d + 2 * H_G:].astype(BF16)
    cbuf = jnp.pad(conv_buf.astype(F32), ((0, 0), (SUBLANES - (CONV_W - 1), 0), (0, 0)))
    cw = jnp.pad(0.5 * conv_w.astype(F32), ((0, SUBLANES - CONV_W), (0, 0)))
    avec = jnp.pad(a_log.astype(F32), (H_G, LANES - 2 * H_G)).reshape(1, LANES)
    dtvec = jnp.pad(dt_bias.astype(F32), (H_G, LANES - 2 * H_G)).reshape(1, LANES)
    bf16o = jax.ShapeDtypeStruct((b, t, d), BF16)
    tail_spec = pl.BlockSpec((None, SUBLANES, 3 * d), lambda bb, i: (bb, 0, 0))
    return pl.pallas_call(
        functools.partial(_gdn_proj_kernel, chunk=chunk),
        grid=(b, t // tm),
        in_specs=[_rows_spec(tm, d)] + _mod_specs(layer, boff, d, (0, 1))
                 + [_const_spec((d, 3 * d)), _const_spec((d, LANES)), _const_spec((d, d)), tail_spec,
                    _const_spec((SUBLANES, 3 * d)), _const_spec((1, LANES)), _const_spec((1, LANES))],
        out_specs=[_rows_spec(tm, d)] * 3 + [_rows_spec(tm, LANES), _rows_spec(tm, d), tail_spec],
        out_shape=[bf16o, bf16o, bf16o, jax.ShapeDtypeStruct((b, t, LANES), F32), bf16o,
                   jax.ShapeDtypeStruct((b, SUBLANES, 3 * d), F32)],
        scratch_shapes=[pltpu.VMEM((tm + SUBLANES, 3 * d), F32)],
        compiler_params=_cparams("arbitrary", "arbitrary"),
        name="gdn_in_proj",
    )(x, mod4, mod4, wqkv, wba, wz, cbuf, cw, avec, dtvec)


def _unit_lower_inverse_minus_identity(mats):
    n = mats[0].shape[0]
    r = _iota((n, n), 0)
    c = _iota((n, n), 1)

    def mm(xs, ys):
        return [_dot(x.astype(BF16), y.astype(BF16)) for x, y in zip(xs, ys)]

    base = 8
    diag = _div_pow2(r, base) == _div_pow2(c, base)
    d = [jnp.where(diag, a, 0.0) for a in mats]
    d2 = mm(d, d)
    d4 = mm(d2, d2)
    nn = [-x for x in d]
    nn = [x + y + z for x, y, z in zip(nn, d2, mm(nn, d2))]
    nn = [x + y + z for x, y, z in zip(nn, d4, mm(nn, d4))]
    m = base
    while m < n:
        pair = (_div_pow2(r, 2 * m) == _div_pow2(c, 2 * m)) & (_div_pow2(r, m) != _div_pow2(c, m))
        off = [jnp.where(pair, a, 0.0) for a in mats]
        y = [o + p for o, p in zip(off, mm(nn, off))]
        x = [p + q for p, q in zip(y, mm(y, nn))]
        nn = [p - q for p, q in zip(nn, x)]
        m *= 2
    return nn


def _gdn_kernel(q_ref, k_ref, v_ref, gates_ref, z_ref, s0_ref, nw_ref, o_ref, s_ref, *, chunk):
    ti = pl.program_id(1)
    grp, tb, d = z_ref.shape
    dk = d // H_G
    n_chunks = tb // chunk

    @pl.when(ti == 0)
    def _():
        s_ref[...] = s0_ref[...]

    ri = _iota((chunk, chunk), 0)
    ci = _iota((chunk, chunk), 1)
    eye_l = (_iota((LANES, LANES), 0) == _iota((LANES, LANES), 1)).astype(BF16)
    incl = ri >= ci
    strict = ri > ci
    nw = nw_ref[...]
    items = [(g, h) for g in range(grp) for h in range(H_G)]

    def chunk_body(cidx, carry):
        r0 = pl.multiple_of(cidx * chunk, chunk)
        rows = pl.ds(r0, chunk)
        gates = [gates_ref[g, rows, :] for g in range(grp)]
        gates_t = [_dot_exact_nt(eye_l, x) for x in gates]

        kbf = [k_ref[g, rows, h * dk:(h + 1) * dk] for g, h in items]
        qbf = [q_ref[g, rows, h * dk:(h + 1) * dk] for g, h in items]
        q = [x.astype(F32) for x in qbf]
        k = [x.astype(F32) for x in kbf]
        v = [v_ref[g, rows, h * dk:(h + 1) * dk].astype(F32) for g, h in items]
        beta = [gates[g][:, h:h + 1] for g, h in items]
        gcol = [gates[g][:, H_G + h:H_G + h + 1] for g, h in items]
        grow = [gates_t[g][H_G + h:H_G + h + 1, :] for g, h in items]
        dec_incl = [jnp.exp(jnp.where(incl, gc - gr, NEG_INF)) for gc, gr in zip(gcol, grow)]
        kb = [x * bt for x, bt in zip(k, beta)]
        a_mat = [_dot_nt(x.astype(BF16), y) for x, y in zip(kb, kbf)]
        qk = [_dot_nt(x, y) for x, y in zip(qbf, kbf)]
        a_mat = [jnp.where(strict, x * e, 0.0) for x, e in zip(a_mat, dec_incl)]
        qk = [x * e for x, e in zip(qk, dec_incl)]
        exp_g = [jnp.exp(gc) for gc in gcol]
        rhs = [jnp.concatenate([x * bt, y * e], axis=1) for x, bt, y, e in zip(v, beta, kb, exp_g)]
        nn = _unit_lower_inverse_minus_identity(a_mat)
        sol = [x + _dot(y.astype(BF16), x.astype(BF16)) for x, y in zip(rhs, nn)]
        s = [s_ref[g, h] for g, h in items]
        sb = [x.astype(BF16) for x in s]
        v_res = [x[:, :dk] - _dot(x[:, dk:].astype(BF16), y) for x, y in zip(sol, sb)]
        vrb = [x.astype(BF16) for x in v_res]
        o = [_dot((x * e).astype(BF16), y) for x, e, y in zip(q, exp_g, sb)]
        o = [x + _dot(y.astype(BF16), z) for x, y, z in zip(o, qk, vrb)]
        g_last = [gc[chunk - 1:chunk, :] for gc in gcol]
        k_dec = [(x * jnp.exp(gl - gc)).astype(BF16) for x, gl, gc in zip(k, g_last, gcol)]
        s_add = [_dot_tn(x, y) for x, y in zip(k_dec, vrb)]
        for i, (g, h) in enumerate(items):
            s_ref[g, h] = s[i] * jnp.exp(g_last[i]) + s_add[i]
            on = o[i] * lax.rsqrt(jnp.mean(o[i] * o[i], axis=-1, keepdims=True) + NORM_EPS) * nw
            zz = z_ref[g, rows, h * dk:(h + 1) * dk].astype(F32)
            o_ref[g, rows, h * dk:(h + 1) * dk] = (on * _silu(zz)).astype(BF16)
        return carry

    lax.fori_loop(0, n_chunks, chunk_body, 0)


def _gdn_mix(q, k, v, gates, z, s0, norm_w):
    b, t, d = q.shape
    dk = d // H_G
    chunk = min(GDN_CHUNK, t)
    tb = _row_tile(t, 4 * chunk)
    grp = 4 if b % 4 == 0 else (2 if b % 2 == 0 else 1)
    nw = norm_w.astype(F32).reshape(1, dk)

    def rows(n):
        return pl.BlockSpec((grp, tb, n), lambda bb, i: (bb, i, 0))

    state_spec = pl.BlockSpec((grp, H_G, dk, dk), lambda bb, i: (bb, 0, 0, 0))
    return pl.pallas_call(
        functools.partial(_gdn_kernel, chunk=chunk),
        grid=(b // grp, t // tb),
        in_specs=[rows(d)] * 3 + [rows(LANES), rows(d), state_spec, _const_spec((1, dk))],
        out_specs=[rows(d), state_spec],
        out_shape=[jax.ShapeDtypeStruct((b, t, d), BF16), jax.ShapeDtypeStruct((b, H_G, dk, dk), F32)],
        compiler_params=_cparams("arbitrary", "arbitrary"),
        name="gdn_mixer",
    )(q, k, v, gates, z, s0.astype(F32), nw)


def _out_proj_kernel(o_ref, x_ref, gate_ref, w_ref, g_ref, b_ref, y_ref):
    h = _dot(o_ref[...], w_ref[...])
    y = DEEPNORM_ALPHA * x_ref[...] + (1.0 + gate_ref[...]) * h
    mu = jnp.mean(y, axis=-1, keepdims=True)
    yc = y - mu
    var = jnp.mean(yc * yc, axis=-1, keepdims=True)
    y_ref[...] = yc * lax.rsqrt(var + LN_EPS) * g_ref[...] + b_ref[...]


def _out_proj(o, x, mod4, layer, boff, w_out, ln_g, ln_b):
    b, t, d = x.shape
    kdim = o.shape[-1]
    tm = _row_tile(t, 512)
    return pl.pallas_call(
        _out_proj_kernel,
        grid=(b, t // tm),
        in_specs=[_rows_spec(tm, kdim), _rows_spec(tm, d)] + _mod_specs(layer, boff, d, (2,))
                 + [_const_spec((kdim, d)), _const_spec((1, d)), _const_spec((1, d))],
        out_specs=_rows_spec(tm, d),
        out_shape=jax.ShapeDtypeStruct((b, t, d), F32),
        compiler_params=_cparams("arbitrary", "arbitrary"),
        name="out_proj_postnorm",
    )(o, x, mod4, w_out.astype(BF16), ln_g.reshape(1, d), ln_b.reshape(1, d))


def _qkvz_outputs(u, w_ref, q_ref, k32_ref, v32_ref, kb_ref, vb_ref, z_ref, q_scale, transposed):
    d = u.shape[-1]
    if transposed:
        q_ref[...] = (_dot_nt(w_ref[:, 0:d], u) * q_scale).astype(BF16)
    else:
        q_ref[...] = (_dot(u, w_ref[:, 0:d]) * q_scale).astype(BF16)
    k = _dot(u, w_ref[:, d:2 * d])
    k32_ref[...] = k
    kb_ref[...] = k.astype(BF16)
    v = _dot(u, w_ref[:, 2 * d:3 * d])
    v32_ref[...] = v
    vb_ref[...] = (v.T if transposed else v).astype(BF16)
    z_ref[...] = _dot(u, w_ref[:, 3 * d:4 * d]).astype(BF16)


def _qkvz_weights(w_in, d, transposed):
    w = w_in[:, :4 * d]
    if transposed:
        w = jnp.concatenate([w[:, :d].T, w[:, d:]], axis=1)
    return w.astype(BF16)


def _qkvz_specs(b, t, d, tm, transposed):
    cols_spec = pl.BlockSpec((None, d, tm), lambda bb, i: (bb, 0, i))
    rows = _rows_spec(tm, d)
    f32o = jax.ShapeDtypeStruct((b, t, d), F32)
    bf16o = jax.ShapeDtypeStruct((b, t, d), BF16)
    bf16t = jax.ShapeDtypeStruct((b, d, t), BF16)
    if transposed:
        return [cols_spec, rows, rows, rows, cols_spec, rows], [bf16t, f32o, f32o, bf16o, bf16t, bf16o]
    return [rows] * 6, [bf16o, f32o, f32o, bf16o, bf16o, bf16o]


def _fox_proj_kernel(x_ref, shift_ref, scale_ref, w_ref, wf_ref, bf_ref,
                     q_ref, k32_ref, v32_ref, kb_ref, vb_ref, z_ref, logf_ref, *, q_scale, transposed):
    u = _modulated(x_ref, shift_ref, scale_ref)
    _qkvz_outputs(u, w_ref, q_ref, k32_ref, v32_ref, kb_ref, vb_ref, z_ref, q_scale, transposed)
    f = _dot(u, wf_ref[...])[:, :H_F] + bf_ref[...]
    logf_ref[...] = -_softplus(-f)


def _fox_proj(x, mod4, layer, boff, w_in, b_f, transposed):
    b, t, d = x.shape
    tm = _row_tile(t, 256)
    w = _qkvz_weights(w_in, d, transposed)
    wf = jnp.pad(w_in[:, 4 * d:], ((0, 0), (0, LANES - H_F))).astype(BF16)
    out_specs, out_shape = _qkvz_specs(b, t, d, tm, transposed)
    return pl.pallas_call(
        functools.partial(_fox_proj_kernel, q_scale=(d // H_F) ** -0.5 * (LOG2E if transposed else 1.0),
                          transposed=transposed),
        grid=(b, t // tm),
        in_specs=[_rows_spec(tm, d)] + _mod_specs(layer, boff, d, (0, 1))
                 + [_const_spec((d, 4 * d)), _const_spec((d, LANES)), _const_spec((1, H_F))],
        out_specs=out_specs + [_rows_spec(tm, H_F)],
        out_shape=out_shape + [jax.ShapeDtypeStruct((b, t, H_F), F32)],
        compiler_params=_cparams("arbitrary", "arbitrary"),
        name="fox_in_proj",
    )(x, mod4, mod4, w, wf, b_f.astype(F32).reshape(1, H_F))


def _cumsum_kernel(x_ref, c0_ref, cn_ref, ct_ref, *, blk):
    s, h = x_ref.shape
    tri = (_iota((blk, blk), 0) >= _iota((blk, blk), 1)).astype(BF16)
    eye_h = (_iota((h, h), 0) == _iota((h, h), 1)).astype(BF16)
    carry = c0_ref[...]
    for i in range(s // blk):
        c = _dot_exact_l(tri, x_ref[i * blk:(i + 1) * blk, :]) + carry
        cn_ref[i * blk:(i + 1) * blk, :] = c
        ct_ref[:, i * blk:(i + 1) * blk] = _dot_exact_nt(eye_h, c)
        carry = c[blk - 1:blk, :]


def _cumsum_time(x, c0):
    b, s, h = x.shape
    blk = 256 if s % 256 == 0 else s
    return pl.pallas_call(
        functools.partial(_cumsum_kernel, blk=blk),
        grid=(b,),
        in_specs=[pl.BlockSpec((None, s, h), lambda bb: (bb, 0, 0)),
                  pl.BlockSpec((None, 1, h), lambda bb: (bb, 0, 0))],
        out_specs=[pl.BlockSpec((None, s, h), lambda bb: (bb, 0, 0)),
                   pl.BlockSpec((None, h, s), lambda bb: (bb, 0, 0))],
        out_shape=[jax.ShapeDtypeStruct((b, s, h), F32), jax.ShapeDtypeStruct((b, h, s), F32)],
        compiler_params=_cparams("arbitrary"),
        name="logf_cumsum",
    )(x, c0)


AUG = LANES // H_F


def _fox_aug_kernel(x_ref, kaug_ref, qaugt_ref, *, blk):
    s, h = x_ref.shape
    tri = (_iota((blk, blk), 0) >= _iota((blk, blk), 1)).astype(BF16)
    lane_h = _iota((h, LANES), 1)
    row_h = _iota((h, LANES), 0)
    ek = [jnp.where(lane_h == row_h * AUG + part, -1.0, 0.0).astype(BF16) for part in range(3)]
    row_q = _iota((LANES, h), 0)
    col_q = _iota((LANES, h), 1)
    eq = [jnp.where(row_q == col_q * AUG + 3 + part, 1.0, 0.0).astype(BF16) for part in range(3)]
    k_slot = _iota((blk, LANES), 1) & (AUG - 1)
    k_ones = jnp.where(k_slot >= 3, jnp.where(k_slot < 6, 1.0, 0.0), 0.0)
    q_ones = jnp.where((_iota((LANES, blk), 0) & (AUG - 1)) < 3, 1.0, 0.0)
    carry = jnp.zeros((1, h), F32)
    for i in range(s // blk):
        c = _dot_exact_l(tri, x_ref[i * blk:(i + 1) * blk, :]) + carry
        parts = _split3(c * LOG2E)
        kaug = k_ones
        qaugt = q_ones
        for part in range(3):
            kaug = kaug + _dot(parts[part], ek[part])
            qaugt = qaugt + _dot_nt(eq[part], parts[part])
        kaug_ref[i * blk:(i + 1) * blk, :] = kaug.astype(BF16)
        qaugt_ref[:, i * blk:(i + 1) * blk] = qaugt.astype(BF16)
        carry = c[blk - 1:blk, :]


def _fox_aug(logf):
    b, s, h = logf.shape
    assert h * AUG == LANES and AUG >= 6
    blk = 256 if s % 256 == 0 else s
    return pl.pallas_call(
        functools.partial(_fox_aug_kernel, blk=blk),
        grid=(b,),
        in_specs=[pl.BlockSpec((None, s, h), lambda bb: (bb, 0, 0))],
        out_specs=[pl.BlockSpec((None, s, LANES), lambda bb: (bb, 0, 0)),
                   pl.BlockSpec((None, LANES, s), lambda bb: (bb, 0, 0))],
        out_shape=[jax.ShapeDtypeStruct((b, s, LANES), BF16), jax.ShapeDtypeStruct((b, LANES, s), BF16)],
        compiler_params=_cparams("arbitrary"),
        name="fox_bias_operands",
    )(logf)


SUM_ROWS = 16


def _softmax_t_probs(s, m_ref):
    m_prev = m_ref[...]
    m_new = jnp.maximum(m_prev, jnp.max(s, axis=0, keepdims=True))
    m_ref[...] = m_new
    return jnp.exp2(s - m_new).astype(BF16), jnp.exp2(m_prev - m_new)


def _softmax_t_accumulate(vt, p, alpha, acc_ref):
    vt_ext = jnp.concatenate([vt, jnp.ones((SUM_ROWS, vt.shape[1]), BF16)], axis=0)
    acc_ref[...] = alpha * acc_ref[...] + _dot(vt_ext, p)


def _softmax_t_result(acc_ref, dv):
    return acc_ref[0:dv, :] / acc_ref[dv:dv + 1, :]


def _fox_attn_kernel(qt_ref, k_ref, kaug_ref, vt_ref, qaugt_ref, z_ref, o_ref,
                     qcat_ref, s_ref, m_ref, acc_ref, *, tk):
    hp = pl.program_id(1)
    qi = pl.program_id(2)
    tq = qt_ref.shape[1]
    assert tq == 2 * tk
    dh = LANES // 2
    row = _iota((LANES, tq), 0)
    qt = qt_ref[...].astype(F32)
    qa = qaugt_ref[...].astype(F32)
    for hh in range(2):
        qcat_ref[hh, 0:LANES, :] = jnp.where(_div_pow2(row, dh) == hh, qt, 0.0).astype(BF16)
        qcat_ref[hh, LANES:2 * LANES, :] = jnp.where(_div_pow2(row, AUG) == hp * 2 + hh, qa, 0.0).astype(BF16)
    m_ref[...] = jnp.full(m_ref.shape, NEG_INF, F32)
    acc_ref[...] = jnp.zeros(acc_ref.shape, F32)
    n_full = qi * 2
    key_in = _iota((tk, tk), 0)
    qry_in = _iota((tk, tk), 1)

    def scores(j, slot, q0=0):
        k0 = pl.multiple_of(j * tk, tk)
        kcat = jnp.concatenate([k_ref[pl.ds(k0, tk), :], kaug_ref[pl.ds(k0, tk), :]], axis=1)
        for hh in range(2):
            s_ref[slot, hh, :, q0:] = _dot(kcat, qcat_ref[hh, :, q0:])

    def consume(j, slot, masked, q0=0):
        k0 = pl.multiple_of(j * tk, tk)
        for hh in range(2):
            for half in range(q0 // tk, 2):
                s = s_ref[slot, hh, :, half * tk:(half + 1) * tk]
                if masked:
                    s = jnp.where(k0 + key_in <= qi * tq + half * tk + qry_in, s, NEG_INF)
                p, alpha = _softmax_t_probs(s, m_ref.at[hh, half])
                _softmax_t_accumulate(vt_ref[hh * dh:(hh + 1) * dh, pl.ds(k0, tk)], p, alpha, acc_ref.at[hh, half])

    def pair(cur, a):
        nxt = 1 - cur
        scores(a + 2, 2 * nxt)
        consume(a, 2 * cur, False)
        scores(a + 3, 2 * nxt + 1)
        consume(a + 1, 2 * cur + 1, False)

    def last_pair(cur):
        consume(n_full, 2 * cur, True)
        consume(n_full + 1, 2 * cur + 1, True, q0=tk)

    scores(0, 0)
    scores(1, 1)

    def body(i, carry):
        pair(0, 4 * i)
        pair(1, 4 * i + 2)
        return carry

    lax.fori_loop(0, lax.shift_right_logical(qi, 1), body, 0)

    @pl.when((qi & 1) == 1)
    def _():
        pair(0, n_full - 2)
        last_pair(1)

    @pl.when((qi & 1) == 0)
    def _():
        last_pair(0)
    o_t = jnp.concatenate(
        [jnp.concatenate([_softmax_t_result(acc_ref.at[hh, half], dh) for half in range(2)], axis=1)
         for hh in range(2)], axis=0)
    o_ref[...] = (o_t.T * _silu(z_ref[...].astype(F32))).astype(BF16)


def _fox_attn_prompt(qt, kb, kaug, vt, qaugt, z):
    b, d, t = qt.shape
    tq, tk = ATTN_TQ, ATTN_TK
    assert t % tq == 0
    hpairs = d // LANES
    dh = LANES // 2
    return pl.pallas_call(
        functools.partial(_fox_attn_kernel, tk=tk),
        grid=(b, hpairs, t // tq),
        in_specs=[pl.BlockSpec((None, LANES, tq), lambda bb, hp, i: (bb, hp, i)),
                  pl.BlockSpec((None, t, LANES), lambda bb, hp, i: (bb, 0, hp)),
                  pl.BlockSpec((None, t, LANES), lambda bb, hp, i: (bb, 0, 0)),
                  pl.BlockSpec((None, LANES, t), lambda bb, hp, i: (bb, hp, 0)),
                  pl.BlockSpec((None, LANES, tq), lambda bb, hp, i: (bb, 0, i)),
                  pl.BlockSpec((None, tq, LANES), lambda bb, hp, i: (bb, i, hp))],
        out_specs=pl.BlockSpec((None, tq, LANES), lambda bb, hp, i: (bb, i, hp)),
        out_shape=jax.ShapeDtypeStruct((b, t, d), BF16),
        scratch_shapes=[pltpu.VMEM((2, 2 * LANES, tq), BF16), pltpu.VMEM((4, 2, tk, tq), F32),
                        pltpu.VMEM((2, 2, 1, tk), F32), pltpu.VMEM((2, 2, dh + SUM_ROWS, tk), F32)],
        compiler_params=_cparams("arbitrary", "arbitrary", "arbitrary"),
        name="fox_attention_prompt",
    )(qt, kb, kaug, vt, qaugt, z)


def _fox_decode_kernel(q_ref, kp_ref, vp_ref, kn_ref, vn_ref, z_ref, cq_ref, ckt_ref, o_ref):
    hp = pl.program_id(1)
    t = q_ref.shape[0]
    p_len = kp_ref.shape[0]
    dh = LANES // 2
    lane = _iota((t, LANES), 1)
    q = q_ref[...]
    kp = kp_ref[...].astype(BF16)
    vp = vp_ref[...].astype(BF16)
    kn = kn_ref[...]
    vn = vn_ref[...]
    cq_all = cq_ref[...]
    hlane = _iota(cq_all.shape, 1)
    row2 = _iota((2 * t, t), 0)
    causal = _iota((2 * t, t), 1) <= jnp.where(row2 >= t, row2 - t, row2)
    qm, gate = [], []
    for hh in range(2):
        h = hp * 2 + hh
        qm.append(jnp.where(_div_pow2(lane, dh) == hh, q, jnp.zeros_like(q)))
        cq = jnp.sum(jnp.where(hlane == h, cq_all, 0.0), axis=-1, keepdims=True)
        gate.append(cq - ckt_ref[pl.ds(h, 1), :])
    qm = jnp.concatenate(qm, axis=0)
    gate = jnp.concatenate(gate, axis=0)
    s_p = _dot_nt(qm, kp) + gate[:, :p_len]
    s_n = jnp.where(causal, _dot_nt(qm, kn) + gate[:, p_len:], NEG_INF)
    m = jnp.maximum(jnp.max(s_p, axis=-1, keepdims=True), jnp.max(s_n, axis=-1, keepdims=True))
    e_p = jnp.exp(s_p - m)
    e_n = jnp.exp(s_n - m)
    den = jnp.sum(e_p, axis=-1, keepdims=True) + jnp.sum(e_n, axis=-1, keepdims=True)
    o2 = (_dot(e_p.astype(BF16), vp) + _dot(e_n.astype(BF16), vn)) / den
    o = jnp.where(lane < dh, o2[:t], o2[t:])
    o_ref[...] = (o * _silu(z_ref[...].astype(F32))).astype(BF16)


def _fox_attn_sample(q, k_past, v_past, kb, vb, z, cq_new, cum_t):
    b, t, d = q.shape
    p_len = k_past.shape[1]
    hpairs = d // LANES
    new_spec = pl.BlockSpec((None, t, LANES), lambda bb, hp: (bb, 0, hp))
    past_spec = pl.BlockSpec((None, p_len, LANES), lambda bb, hp: (bb, 0, hp))
    return pl.pallas_call(
        _fox_decode_kernel,
        grid=(b, hpairs),
        in_specs=[new_spec, past_spec, past_spec, new_spec, new_spec, new_spec,
                  pl.BlockSpec((None, t, H_F), lambda bb, hp: (bb, 0, 0)),
                  pl.BlockSpec((None, H_F, p_len + t), lambda bb, hp: (bb, 0, 0))],
        out_specs=new_spec,
        out_shape=jax.ShapeDtypeStruct((b, t, d), BF16),
        compiler_params=_cparams("arbitrary", "arbitrary"),
        name="fox_attention_sample",
    )(q, k_past, v_past, kb, vb, z, cq_new, cum_t)


def _diff_proj_kernel(x_ref, shift_ref, scale_ref, w_ref, q_ref, k32_ref, v32_ref, kb_ref, vb_ref, z_ref,
                      *, q_scale, transposed):
    u = _modulated(x_ref, shift_ref, scale_ref)
    _qkvz_outputs(u, w_ref, q_ref, k32_ref, v32_ref, kb_ref, vb_ref, z_ref, q_scale, transposed)


def _diff_proj(x, mod4, layer, boff, w_in, transposed):
    b, t, d = x.shape
    tm = _row_tile(t, 256)
    out_specs, out_shape = _qkvz_specs(b, t, d, tm, transposed)
    return pl.pallas_call(
        functools.partial(_diff_proj_kernel, q_scale=(d // (2 * H_D)) ** -0.5 * (LOG2E if transposed else 1.0),
                          transposed=transposed),
        grid=(b, t // tm),
        in_specs=[_rows_spec(tm, d)] + _mod_specs(layer, boff, d, (0, 1)) + [_const_spec((d, 4 * d))],
        out_specs=out_specs,
        out_shape=out_shape,
        compiler_params=_cparams("arbitrary", "arbitrary"),
        name="diff_in_proj",
    )(x, mod4, mod4, _qkvz_weights(w_in, d, transposed))


def _t5_thresholds():
    nb = N_BUCKETS // 2
    max_exact = nb // 2
    steps = nb - max_exact
    ratio = MAX_DISTANCE // max_exact
    out = []
    for kk in range(1, nb - max_exact):
        target = max_exact ** steps * ratio ** kk
        n = max_exact
        while n ** steps < target:
            n += 1
        out.append(n)
    return nb, max_exact, out


def _bias_kernel(tbl_ref, o_ref, *, q0, k0, keys_on_rows):
    h = pl.program_id(0)
    shape = o_ref.shape
    kdim, qdim = (0, 1) if keys_on_rows else (1, 0)
    rel = (k0 + _iota(shape, kdim)) - (q0 + _iota(shape, qdim))
    nb, max_exact, thr = _t5_thresholds()
    n = jnp.abs(rel)
    large = jnp.full(shape, max_exact, jnp.int32)
    for tval in thr:
        large = large + (n >= tval).astype(jnp.int32)
    bucket = jnp.where(rel > 0, nb, 0) + jnp.where(n < max_exact, n, large)
    acc = jnp.zeros(shape, F32)
    for bkt in range(N_BUCKETS):
        acc = jnp.where(bucket == bkt, tbl_ref[bkt * H_D + h], acc)
    if keys_on_rows:
        acc = (acc - tbl_ref[(nb - 1) * H_D + h]) * LOG2E
    o_ref[...] = acc


def _bias_tile(rel_table, q0, nq, k0, nk, keys_on_rows=False):
    shape = (nk, nq) if keys_on_rows else (nq, nk)
    return pl.pallas_call(
        functools.partial(_bias_kernel, q0=q0, k0=k0, keys_on_rows=keys_on_rows),
        grid=(H_D,),
        in_specs=[pl.BlockSpec(memory_space=pltpu.SMEM)],
        out_specs=pl.BlockSpec((None,) + shape, lambda h: (h, 0, 0)),
        out_shape=jax.ShapeDtypeStruct((H_D,) + shape, F32),
        compiler_params=_cparams("arbitrary"),
        name="t5_bias_tile",
    )(rel_table.astype(F32).reshape(N_BUCKETS * H_D))


def _diff_lambda(lam_ref):
    lam = lam_ref[...]
    s1 = jnp.sum(lam[0:1, :] * lam[1:2, :], axis=-1, keepdims=True)
    s2 = jnp.sum(lam[2:3, :] * lam[3:4, :], axis=-1, keepdims=True)
    return jnp.exp(s1) - jnp.exp(s2) + LAMBDA_INIT


def _diff_epilogue(o, z_ref, subln_ref, o_ref):
    on = o * lax.rsqrt(jnp.mean(o * o, axis=-1, keepdims=True) + NORM_EPS) * subln_ref[...]
    on = on * (1.0 - LAMBDA_INIT)
    o_ref[...] = (on * _silu(z_ref[...].astype(F32))).astype(BF16)


def _diff_attn_kernel(tbl_ref, qt_ref, k_ref, vt_ref, z_ref, biasm_ref, bias0_ref, bias1_ref, lam_ref, subln_ref,
                      o_ref, qcat_ref, s_ref, m_ref, acc_ref, *, tk):
    h = pl.program_id(1)
    qi = pl.program_id(2)
    tq = qt_ref.shape[1]
    assert tq == 2 * tk
    dh = LANES // 2
    row = _iota((LANES, tq), 0)
    qt = qt_ref[...].astype(F32)
    nb, _, _ = _t5_thresholds()
    far = _split3(jnp.full((LANES, tq), tbl_ref[(nb - 1) * H_D + h], F32) * LOG2E)
    far_rows = jnp.zeros((LANES, tq), F32)
    for part in range(3):
        far_rows = jnp.where(row == part, far[part].astype(F32), far_rows)
    for br in range(2):
        qcat_ref[br, 0:LANES, :] = jnp.where(_div_pow2(row, dh) == br, qt, 0.0).astype(BF16)
        qcat_ref[br, LANES:2 * LANES, :] = far_rows.astype(BF16)
    ones_aug = jnp.where(_iota((tk, LANES), 1) < 3, 1.0, 0.0).astype(BF16)
    m_ref[...] = jnp.full(m_ref.shape, NEG_INF, F32)
    acc_ref[...] = jnp.zeros(acc_ref.shape, F32)
    key_in = _iota((tk, tk), 0)
    qry_in = _iota((tk, tk), 1)

    def scores(j, slot, q0=0):
        k0 = pl.multiple_of(j * tk, tk)
        kcat = jnp.concatenate([k_ref[pl.ds(k0, tk), :], ones_aug], axis=1)
        for br in range(2):
            s_ref[slot, br, :, q0:] = _dot(kcat, qcat_ref[br, :, q0:])

    def consume(j, slot, bias_ref=None, key_off=None, q0=0):
        k0 = pl.multiple_of(j * tk, tk)
        for br in range(2):
            for half in range(q0 // tk, 2):
                lanes = slice(half * tk, (half + 1) * tk)
                s = s_ref[slot, br, :, lanes]
                if bias_ref is not None:
                    s = s + bias_ref[:, lanes]
                if key_off is not None:
                    s = jnp.where(_div_pow2(key_off + key_in, ATTN_CHUNK)
                                  <= _div_pow2(half * tk + qry_in, ATTN_CHUNK), s, NEG_INF)
                p, alpha = _softmax_t_probs(s, m_ref.at[br, half])
                _softmax_t_accumulate(vt_ref[:, pl.ds(k0, tk)], p, alpha, acc_ref.at[br, half])

    def pair(cur, a, second_bias_ref=None):
        nxt = 1 - cur
        scores(a + 2, 2 * nxt)
        consume(a, 2 * cur)
        scores(a + 3, 2 * nxt + 1)
        consume(a + 1, 2 * cur + 1, bias_ref=second_bias_ref)

    def last_pair(cur):
        consume(2 * qi, 2 * cur, bias_ref=bias0_ref, key_off=0)
        consume(2 * qi + 1, 2 * cur + 1, bias_ref=bias1_ref, key_off=tk, q0=tk)

    scores(0, 0)
    scores(1, 1)
    n_plain = jnp.maximum(qi - 1, 0)

    def body(i, carry):
        pair(0, 4 * i)
        pair(1, 4 * i + 2)
        return carry

    lax.fori_loop(0, lax.shift_right_logical(n_plain, 1), body, 0)

    @pl.when(qi == 0)
    def _():
        last_pair(0)

    @pl.when((qi & 1) == 1)
    def _():
        pair(0, 2 * qi - 2, second_bias_ref=biasm_ref)
        last_pair(1)

    @pl.when(jnp.logical_and(qi >= 2, (qi & 1) == 0))
    def _():
        pair(0, 2 * qi - 4)
        pair(1, 2 * qi - 2, second_bias_ref=biasm_ref)
        last_pair(0)
    branch = [jnp.concatenate([_softmax_t_result(acc_ref.at[br, half], LANES) for half in range(2)], axis=1)
              for br in range(2)]
    o_t = branch[0] - _diff_lambda(lam_ref) * branch[1]
    _diff_epilogue(o_t.T, z_ref, subln_ref, o_ref)


def _lam_pack(lam_q1, lam_k1, lam_q2, lam_k2):
    rows = jnp.stack([lam_q1, lam_k1, lam_q2, lam_k2]).astype(F32)
    return jnp.pad(rows, ((0, SUBLANES - 4), (0, LANES - rows.shape[1])))


def _diff_attn_prompt(qt, kb, vt, z, rel_table, lam, subln_w):
    b, d, t = qt.shape
    tq, tk = ATTN_TQ, ATTN_TK
    assert t % tq == 0 and tk % ATTN_CHUNK == 0 and tk >= MAX_DISTANCE
    biasm = _bias_tile(rel_table, tk, tq, 0, tk, keys_on_rows=True)
    bias0 = _bias_tile(rel_table, 0, tq, 0, tk, keys_on_rows=True)
    bias1 = _bias_tile(rel_table, 0, tq, tk, tk, keys_on_rows=True)
    rows_spec = pl.BlockSpec((None, tq, LANES), lambda bb, h, i: (bb, i, h))
    bias_spec = pl.BlockSpec((None, tk, tq), lambda bb, h, i: (h, 0, 0))
    return pl.pallas_call(
        functools.partial(_diff_attn_kernel, tk=tk),
        grid=(b, H_D, t // tq),
        in_specs=[pl.BlockSpec(memory_space=pltpu.SMEM),
                  pl.BlockSpec((None, LANES, tq), lambda bb, h, i: (bb, h, i)),
                  pl.BlockSpec((None, t, LANES), lambda bb, h, i: (bb, 0, h)),
                  pl.BlockSpec((None, LANES, t), lambda bb, h, i: (bb, h, 0)),
                  rows_spec, bias_spec, bias_spec, bias_spec,
                  pl.BlockSpec((SUBLANES, LANES), lambda bb, h, i: (0, 0)),
                  pl.BlockSpec((1, LANES), lambda bb, h, i: (0, 0))],
        out_specs=rows_spec,
        out_shape=jax.ShapeDtypeStruct((b, t, d), BF16),
        scratch_shapes=[pltpu.VMEM((2, 2 * LANES, tq), BF16), pltpu.VMEM((4, 2, tk, tq), F32),
                        pltpu.VMEM((2, 2, 1, tk), F32), pltpu.VMEM((2, 2, LANES + SUM_ROWS, tk), F32)],
        compiler_params=_cparams("arbitrary", "arbitrary", "arbitrary"),
        name="diff_attention_prompt",
    )(rel_table.astype(F32).reshape(N_BUCKETS * H_D), qt, kb, vt, z, biasm, bias0, bias1, lam,
      subln_w.astype(F32).reshape(1, LANES))


def _diff_decode_kernel(q_ref, kp_ref, vp_ref, kn_ref, vn_ref, z_ref, bias_ref, lam_ref, subln_ref, o_ref,
                        *, p_len):
    t = q_ref.shape[0]
    dh = LANES // 2
    lane = _iota((t, LANES), 1)
    q = q_ref[...]
    kp = kp_ref[...].astype(BF16)
    vp = vp_ref[...].astype(BF16)
    kn = kn_ref[...]
    vn = vn_ref[...]
    bias = jnp.concatenate([bias_ref[...], bias_ref[...]], axis=0)
    row_p = _iota((2 * t, p_len), 0)
    row_n = _iota((2 * t, t), 0)
    qp_chunk = _div_pow2(p_len + jnp.where(row_p >= t, row_p - t, row_p), ATTN_CHUNK)
    qn_chunk = _div_pow2(p_len + jnp.where(row_n >= t, row_n - t, row_n), ATTN_CHUNK)
    kp_chunk = _div_pow2(_iota((2 * t, p_len), 1), ATTN_CHUNK)
    kn_chunk = _div_pow2(p_len + _iota((2 * t, t), 1), ATTN_CHUNK)
    qm = jnp.concatenate([jnp.where(_div_pow2(lane, dh) == br, q, jnp.zeros_like(q)) for br in range(2)], axis=0)
    s_p = jnp.where(kp_chunk <= qp_chunk, _dot_nt(qm, kp) + bias[:, :p_len], NEG_INF)
    s_n = jnp.where(kn_chunk <= qn_chunk, _dot_nt(qm, kn) + bias[:, p_len:], NEG_INF)
    m = jnp.maximum(jnp.max(s_p, axis=-1, keepdims=True), jnp.max(s_n, axis=-1, keepdims=True))
    e_p = jnp.exp(s_p - m)
    e_n = jnp.exp(s_n - m)
    den = jnp.sum(e_p, axis=-1, keepdims=True) + jnp.sum(e_n, axis=-1, keepdims=True)
    o2 = (_dot(e_p.astype(BF16), vp) + _dot(e_n.astype(BF16), vn)) / den
    o = o2[:t] - _diff_lambda(lam_ref) * o2[t:]
    _diff_epilogue(o, z_ref, subln_ref, o_ref)


def _diff_attn_sample(q, k_past, v_past, kb, vb, z, rel_table, lam, subln_w):
    b, t, d = q.shape
    p_len = k_past.shape[1]
    bias = _bias_tile(rel_table, p_len, t, 0, p_len + t)
    new_spec = pl.BlockSpec((None, t, LANES), lambda bb, h: (bb, 0, h))
    past_spec = pl.BlockSpec((None, p_len, LANES), lambda bb, h: (bb, 0, h))
    return pl.pallas_call(
        functools.partial(_diff_decode_kernel, p_len=p_len),
        grid=(b, H_D),
        in_specs=[new_spec, past_spec, past_spec, new_spec, new_spec, new_spec,
                  pl.BlockSpec((None, t, p_len + t), lambda bb, h: (h, 0, 0)),
                  pl.BlockSpec((SUBLANES, LANES), lambda bb, h: (0, 0)),
                  pl.BlockSpec((1, LANES), lambda bb, h: (0, 0))],
        out_specs=new_spec,
        out_shape=jax.ShapeDtypeStruct((b, t, d), BF16),
        compiler_params=_cparams("arbitrary", "arbitrary"),
        name="diff_attention_sample",
    )(q, k_past, v_past, kb, vb, z, bias, lam, subln_w.astype(F32).reshape(1, LANES))


def _rope_kernel(inv_ref, cos_ref, sin_ref, *, start):
    t, w = cos_ref.shape
    pos = (start + pl.program_id(0) * t + _iota((t, w), 0)).astype(F32)
    ang = pos * inv_ref[...]
    even = (_iota((t, w), 1) & 1) == 0
    cos_ref[...] = jnp.cos(ang)
    sn = jnp.sin(ang)
    sin_ref[...] = jnp.where(even, -sn, sn)


def _rope_tables(t, start, dk):
    inv_half = np.power(np.float32(ROPE_BASE), -np.arange(0, dk, 2, dtype=np.float32) / np.float32(dk))
    inv = jnp.asarray(np.repeat(inv_half.astype(np.float32), 2).reshape(1, dk))
    tt = _row_tile(t, 512)
    return pl.pallas_call(
        functools.partial(_rope_kernel, start=start),
        grid=(t // tt,),
        in_specs=[pl.BlockSpec((1, dk), lambda i: (0, 0))],
        out_specs=[pl.BlockSpec((tt, dk), lambda i: (i, 0))] * 2,
        out_shape=[jax.ShapeDtypeStruct((t, dk), F32)] * 2,
        compiler_params=_cparams("arbitrary"),
        name="rope_tables",
    )(inv)


def _rotate_pairs(x, cos, sin_signed):
    slabs = []
    for c0 in range(0, x.shape[-1], LANES):
        xs = x[:, c0:c0 + LANES]
        even = (_iota(xs.shape, 1) & 1) == 0
        slabs.append(jnp.where(even, pltpu.roll(xs, LANES - 1, 1), pltpu.roll(xs, 1, 1)))
    return x * cos + jnp.concatenate(slabs, axis=1) * sin_signed


def _ret_proj_kernel(x_ref, shift_ref, scale_ref, w_ref, cos_ref, sin_ref, q_ref, k_ref, v_ref, z_ref,
                     *, q_scale):
    u = _modulated(x_ref, shift_ref, scale_ref)
    d = x_ref.shape[-1]
    dk = cos_ref.shape[-1]
    cos = cos_ref[...]
    sn = sin_ref[...]
    for h in range(d // dk):
        qh = _dot(u, w_ref[:, h * dk:(h + 1) * dk])
        q_ref[:, h * dk:(h + 1) * dk] = (_rotate_pairs(qh, cos, sn) * q_scale).astype(BF16)
        kh = _dot(u, w_ref[:, d + h * dk:d + (h + 1) * dk])
        k_ref[:, h * dk:(h + 1) * dk] = _rotate_pairs(kh, cos, sn).astype(BF16)
    for s in range(2):
        v_ref[:, s * d:(s + 1) * d] = _dot(u, w_ref[:, (2 + s) * d:(3 + s) * d]).astype(BF16)
        z_ref[:, s * d:(s + 1) * d] = _dot(u, w_ref[:, (4 + s) * d:(5 + s) * d]).astype(BF16)


def _ret_proj(x, mod4, layer, boff, w_in, cos, sin_signed):
    b, t, d = x.shape
    dk = d // H_R
    tm = _row_tile(t, 256)
    tab_spec = pl.BlockSpec((tm, dk), lambda bb, i: (i, 0))
    return pl.pallas_call(
        functools.partial(_ret_proj_kernel, q_scale=dk ** -0.5),
        grid=(b, t // tm),
        in_specs=[_rows_spec(tm, d)] + _mod_specs(layer, boff, d, (0, 1)) + [_const_spec((d, 6 * d)), tab_spec, tab_spec],
        out_specs=[_rows_spec(tm, d), _rows_spec(tm, d), _rows_spec(tm, 2 * d), _rows_spec(tm, 2 * d)],
        out_shape=[jax.ShapeDtypeStruct((b, t, d), BF16), jax.ShapeDtypeStruct((b, t, d), BF16),
                   jax.ShapeDtypeStruct((b, t, 2 * d), BF16), jax.ShapeDtypeStruct((b, t, 2 * d), BF16)],
        compiler_params=_cparams("arbitrary", "arbitrary"),
        name="ret_in_proj",
    )(x, mod4, mod4, w_in.astype(BF16), cos, sin_signed)


def _ret_kernel(q_ref, k_ref, v_ref, z_ref, s0_ref, gn_ref, o_ref, s_ref, intra_ref):
    ti = pl.program_id(1)
    lr = q_ref.shape[0]
    dk = q_ref.shape[-1] // H_R
    dv = v_ref.shape[-1] // H_R
    heads = range(H_R)
    log_gamma = [math.log1p(-(2.0 ** (-5.0 - h))) for h in heads]

    @pl.when(ti == 0)
    def _():
        s_ref[...] = s0_ref[...]
        rel = (_iota((lr, lr), 0) - _iota((lr, lr), 1)).astype(F32)
        for h in heads:
            intra_ref[h] = jnp.where(rel >= 0, jnp.exp(log_gamma[h] * jnp.maximum(rel, 0.0)), 0.0)

    idx = _iota((lr, 1), 0).astype(F32)
    for h in heads:
        q_dec = jnp.exp(log_gamma[h] * (idx + 1.0))
        k_dec = jnp.exp(log_gamma[h] * (lr - 1.0 - idx))
        qh = q_ref[:, h * dk:(h + 1) * dk]
        kh = k_ref[:, h * dk:(h + 1) * dk]
        vh = v_ref[:, h * dv:(h + 1) * dv]
        s = s_ref[h]
        att = _dot_nt(qh, kh) * intra_ref[h]
        o = _dot(att.astype(BF16), vh) + _dot(qh, s.astype(BF16)) * q_dec
        s_ref[h] = s * math.exp(log_gamma[h] * lr) + _dot_tn((kh.astype(F32) * k_dec).astype(BF16), vh)
        mu = jnp.mean(o, axis=-1, keepdims=True)
        oc = o - mu
        var = jnp.mean(oc * oc, axis=-1, keepdims=True)
        on = oc * lax.rsqrt(var + LN_EPS) * gn_ref[:, h * dv:(h + 1) * dv]
        zz = z_ref[:, h * dv:(h + 1) * dv].astype(F32)
        o_ref[:, h * dv:(h + 1) * dv] = (on * _silu(zz)).astype(BF16)


def _ret_mix(q, k, v, z, s0, gn_w):
    b, t, d = q.shape
    dk = d // H_R
    dv = v.shape[-1] // H_R
    lr = _row_tile(t, 256)
    state_spec = pl.BlockSpec((None, H_R, dk, dv), lambda bb, i: (bb, 0, 0, 0))
    return pl.pallas_call(
        _ret_kernel,
        grid=(b, t // lr),
        in_specs=[_rows_spec(lr, d), _rows_spec(lr, d), _rows_spec(lr, 2 * d), _rows_spec(lr, 2 * d), state_spec,
                  _const_spec((1, 2 * d))],
        out_specs=[_rows_spec(lr, 2 * d), state_spec],
        out_shape=[jax.ShapeDtypeStruct((b, t, 2 * d), BF16), jax.ShapeDtypeStruct((b, H_R, dk, dv), F32)],
        scratch_shapes=[pltpu.VMEM((H_R, lr, lr), F32)],
        compiler_params=_cparams("arbitrary", "arbitrary"),
        name="retention_mixer",
    )(q, k, v, z, s0.astype(F32), gn_w.astype(F32).reshape(1, 2 * d))


def _run_group(x, mod4, boff, state_gdn, state_gdn_conv, cache_fox_k, cache_fox_v, cache_fox_logf,
               cache_diff_k, cache_diff_v, state_ret, start, p):
    b, t, d = x.shape
    dk_g = d // H_G

    if state_gdn is None:
        state_gdn = jnp.zeros((b, H_G, dk_g, dk_g), F32)
        state_gdn_conv = jnp.zeros((b, CONV_W - 1, 3 * d), F32)
    q, k, v, gates, z, tail = _gdn_proj(x, mod4, 0, boff, p["gdn_w_in"], state_gdn_conv, p["gdn_conv_w"],
                                        p["gdn_a_log"], p["gdn_dt_bias"])
    o, gdn_state = _gdn_mix(q, k, v, gates, z, state_gdn, p["gdn_norm_w"])
    gdn_conv = tail[:, SUBLANES - (CONV_W - 1):, :]
    x = _out_proj(o, x, mod4, 0, boff, p["gdn_w_out"], p["ln_g"][0], p["ln_b"][0])

    prompt = cache_fox_k is None
    q, k32, v32, kb, vb, z, logf = _fox_proj(x, mod4, 1, boff, p["fox_w_in"], p["fox_b_f"], transposed=prompt)
    if prompt:
        kaug, qaugt = _fox_aug(logf)
        o = _fox_attn_prompt(q, kb, kaug, vb, qaugt, z)
    else:
        zero_c = jnp.zeros((b, 1, H_F), F32)
        p_len = cache_fox_k.shape[1]
        cum_pn, cum_pt = _cumsum_time(cache_fox_logf.astype(F32), zero_c)
        cum_n, cum_nt = _cumsum_time(logf, cum_pn[:, p_len - 1:, :])
        o = _fox_attn_sample(q, cache_fox_k.reshape(b, p_len, d), cache_fox_v.reshape(b, p_len, d), kb, vb, z,
                             cum_n, jnp.concatenate([cum_pt, cum_nt], axis=2))
    fox_k = k32.reshape(b, t, H_F, d // H_F)
    fox_v = v32.reshape(b, t, H_F, d // H_F)
    x = _out_proj(o, x, mod4, 1, boff, p["fox_w_out"], p["ln_g"][1], p["ln_b"][1])

    q, k32, v32, kb, vb, z = _diff_proj(x, mod4, 2, boff, p["diff_w_in"], transposed=prompt)
    lam = _lam_pack(p["diff_lam_q1"], p["diff_lam_k1"], p["diff_lam_q2"], p["diff_lam_k2"])
    if prompt:
        o = _diff_attn_prompt(q, kb, vb, z, p["rel_bias_table"], lam, p["diff_subln_w"])
    else:
        p_len = cache_diff_k.shape[1]
        o = _diff_attn_sample(q, cache_diff_k.reshape(b, p_len, d), cache_diff_v.reshape(b, p_len, d), kb, vb, z,
                              p["rel_bias_table"], lam, p["diff_subln_w"])
    diff_k = k32.reshape(b, t, H_D, 2, d // (2 * H_D))
    diff_v = v32.reshape(b, t, H_D, d // H_D)
    x = _out_proj(o, x, mod4, 2, boff, p["diff_w_out"], p["ln_g"][2], p["ln_b"][2])

    dk_r = d // H_R
    cos, sin_signed = _rope_tables(t, start, dk_r)
    q, k, v, z = _ret_proj(x, mod4, 3, boff, p["ret_w_in"], cos, sin_signed)
    if state_ret is None:
        state_ret = jnp.zeros((b, H_R, dk_r, 2 * d // H_R), F32)
    o, ret_state = _ret_mix(q, k, v, z, state_ret, p["ret_gn_w"])
    x = _out_proj(o, x, mod4, 3, boff, p["ret_w_out"], p["ln_g"][3], p["ln_b"][3])

    return x, gdn_state, gdn_conv, fox_k, fox_v, logf, diff_k, diff_v, ret_state


def kernel(x_prompt, x_sample, c_prompt, c_sample, state_gdn, state_gdn_conv, cache_fox_k, cache_fox_v, cache_fox_logf, cache_diff_k, cache_diff_v, state_ret, ada_w, ada_b, ln_g, ln_b, gdn_w_in, gdn_conv_w, gdn_a_log, gdn_dt_bias, gdn_norm_w, gdn_w_out, fox_w_in, fox_b_f, fox_w_out, rel_bias_table, diff_w_in, diff_lam_q1, diff_lam_k1, diff_lam_q2, diff_lam_k2, diff_subln_w, diff_w_out, ret_w_in, ret_gn_w, ret_w_out):
    p = dict(ln_g=ln_g, ln_b=ln_b, gdn_w_in=gdn_w_in, gdn_conv_w=gdn_conv_w, gdn_a_log=gdn_a_log,
             gdn_dt_bias=gdn_dt_bias, gdn_norm_w=gdn_norm_w, gdn_w_out=gdn_w_out, fox_w_in=fox_w_in,
             fox_b_f=fox_b_f, fox_w_out=fox_w_out, rel_bias_table=rel_bias_table, diff_w_in=diff_w_in,
             diff_lam_q1=diff_lam_q1, diff_lam_k1=diff_lam_k1, diff_lam_q2=diff_lam_q2, diff_lam_k2=diff_lam_k2,
             diff_subln_w=diff_subln_w, diff_w_out=diff_w_out, ret_w_in=ret_w_in, ret_gn_w=ret_gn_w,
             ret_w_out=ret_w_out)
    bp = x_prompt.shape[0]
    d = x_prompt.shape[-1]
    mod = _modulation(jnp.concatenate([c_prompt, c_sample], axis=0), ada_w, ada_b)
    mod4 = mod.reshape(mod.shape[0], mod.shape[1], 1, 3 * d)
    outs_p = _run_group(x_prompt, mod4, 0, None, None, None, None, None, None, None, None, 0, p)
    outs_s = _run_group(x_sample, mod4, bp, state_gdn, state_gdn_conv, cache_fox_k, cache_fox_v, cache_fox_logf,
                        cache_diff_k, cache_diff_v, state_ret, cache_fox_k.shape[1], p)
    return (outs_p[0], outs_s[0]) + tuple(outs_p[1:]) + tuple(outs_s[1:])
```

```python
import functools
import math

import numpy as np
import jax
import jax.numpy as jnp
from jax import lax
from jax.experimental import pallas as pl
from jax.experimental.pallas import tpu as pltpu

F32 = jnp.float32
BF16 = jnp.bfloat16

DEPTH = 4
ATTN_TQ = 512
ATTN_TK = 256
GDN_CHUNK = 64
ATTN_CHUNK = 64
DEEPNORM_ALPHA = (2.0 * DEPTH) ** 0.25
LN_EPS = 1e-5
NORM_EPS = 1e-6
NEG_INF = -1e30
LOG2E = math.log2(math.e)
H_G, H_F, H_D, H_R = 8, 16, 8, 4
CONV_W = 4
DIFF_LAYER = 2
LAMBDA_INIT = 0.8 - 0.6 * math.exp(-0.3 * DIFF_LAYER)
N_BUCKETS = 32
MAX_DISTANCE = 128
ROPE_BASE = 10000.0

LANES = 128
SUBLANES = 8
VMEM_LIMIT = 56 * 1024 * 1024


def _cparams(*sem):
    return pltpu.CompilerParams(dimension_semantics=sem, vmem_limit_bytes=VMEM_LIMIT)


def _sigmoid(x):
    return 1.0 / (1.0 + jnp.exp(-x))


def _silu(x):
    hx = 0.5 * x
    return hx + hx * jnp.tanh(hx)


def _softplus(x):
    return jnp.maximum(x, 0.0) + jnp.log(1.0 + jnp.exp(-jnp.abs(x)))


def _dot(a, b):
    return jnp.dot(a, b, preferred_element_type=F32)


def _dot_nt(a, b):
    return lax.dot_general(a, b, (((1,), (1,)), ((), ())), preferred_element_type=F32)


def _dot_tn(a, b):
    return lax.dot_general(a, b, (((0,), (0,)), ((), ())), preferred_element_type=F32)


def _split3(x):
    x1 = x.astype(BF16)
    r1 = x - x1.astype(F32)
    x2 = r1.astype(BF16)
    x3 = (r1 - x2.astype(F32)).astype(BF16)
    return x1, x2, x3


def _dot_exact_l(m01, x):
    x1, x2, x3 = _split3(x)
    return _dot(m01, x1) + _dot(m01, x2) + _dot(m01, x3)


def _dot_exact_nt(m01, x):
    x1, x2, x3 = _split3(x)
    return _dot_nt(m01, x1) + _dot_nt(m01, x2) + _dot_nt(m01, x3)


def _iota(shape, dim):
    return lax.broadcasted_iota(jnp.int32, shape, dim)


def _div_pow2(x, n):
    assert n & (n - 1) == 0
    return jnp.right_shift(x, n.bit_length() - 1)


def _row_tile(t, pref):
    return pref if t % pref == 0 else t


def _mod_kernel(c_ref, w_ref, b_ref, o_ref):
    s = _silu(c_ref[...])
    w = w_ref[...]
    s1 = s.astype(BF16)
    s2 = (s - s1.astype(F32)).astype(BF16)
    w1 = w.astype(BF16)
    w2 = (w - w1.astype(F32)).astype(BF16)
    o_ref[...] = _dot(s1, w1) + _dot(s1, w2) + _dot(s2, w1) + b_ref[...]


def _modulation(c_all, ada_w, ada_b):
    nb, d = c_all.shape
    depth, _, n = ada_w.shape
    tn = 1024
    return pl.pallas_call(
        _mod_kernel,
        grid=(depth, n // tn),
        in_specs=[pl.BlockSpec((nb, d), lambda l, j: (0, 0)),
                  pl.BlockSpec((None, d, tn), lambda l, j: (l, 0, j)),
                  pl.BlockSpec((None, 1, tn), lambda l, j: (l, 0, j))],
        out_specs=pl.BlockSpec((None, nb, tn), lambda l, j: (l, 0, j)),
        out_shape=jax.ShapeDtypeStruct((depth, nb, n), F32),
        compiler_params=_cparams("arbitrary", "arbitrary"),
        name="adaln_modulation",
    )(c_all, ada_w, ada_b.reshape(depth, 1, n))


def _mod_specs(layer, boff, d, which):
    return [pl.BlockSpec((None, None, 1, d), lambda b, i, w=w: (layer, boff + b, 0, w)) for w in which]


def _modulated(x_ref, shift_ref, scale_ref):
    return (x_ref[...] * (1.0 + scale_ref[...]) + shift_ref[...]).astype(BF16)


def _const_spec(shape):
    return pl.BlockSpec(shape, lambda b, i: (0,) * len(shape))


def _rows_spec(tm, n):
    return pl.BlockSpec((None, tm, n), lambda b, i: (b, i, 0))


def _gdn_proj_kernel(x_ref, shift_ref, scale_ref, wqkv_ref, wba_ref, wz_ref, cbuf_ref, cw_ref, avec_ref, dtvec_ref,
                     q_ref, k_ref, v_ref, gates_ref, z_ref, tail_ref, ext_ref, *, chunk):
    i = pl.program_id(1)
    tm, d = x_ref.shape
    dk = d // H_G
    u = _modulated(x_ref, shift_ref, scale_ref)

    @pl.when(i == 0)
    def _():
        ext_ref[0:SUBLANES, :] = cbuf_ref[...]

    for s in range(3):
        ext_ref[SUBLANES:SUBLANES + tm, s * d:(s + 1) * d] = _dot(u, wqkv_ref[:, s * d:(s + 1) * d])
    outs = (q_ref, k_ref, v_ref)
    for s in range(3):
        for h in range(H_G):
            c0 = s * d + h * dk
            e = ext_ref[:, c0:c0 + dk]
            acc = cw_ref[0:1, c0:c0 + dk] * e
            for j in range(1, CONV_W):
                acc = pltpu.roll(acc, 1, 0) + cw_ref[j:j + 1, c0:c0 + dk] * e
            hy = acc[SUBLANES:, :]
            y = hy + hy * jnp.tanh(hy)
            if s < 2:
                inv = lax.rsqrt(jnp.sum(y * y, axis=-1, keepdims=True) + NORM_EPS)
                y = y * (inv * (dk ** -0.5) if s == 0 else inv)
            outs[s][:, h * dk:(h + 1) * dk] = y.astype(BF16)
    tail = ext_ref[tm:tm + SUBLANES, :]
    tail_ref[...] = tail
    ext_ref[0:SUBLANES, :] = tail

    ba = _dot(u, wba_ref[...])
    g = -jnp.exp(avec_ref[...]) * _softplus(ba + dtvec_ref[...])
    r = _iota((tm, tm), 0)
    c = _iota((tm, tm), 1)
    tri = jnp.where(_div_pow2(r, chunk) == _div_pow2(c, chunk), jnp.where(r >= c, 1.0, 0.0), 0.0).astype(BF16)
    gcum = _dot_exact_l(tri, g)
    gates_ref[...] = jnp.where(_iota((tm, LANES), 1) < H_G, _sigmoid(ba), gcum)
    z_ref[...] = _dot(u, wz_ref[...]).astype(BF16)


def _gdn_proj(x, mod4, layer, boff, w_in, conv_buf, conv_w, a_log, dt_bias):
    b, t, d = x.shape
    tm = _row_tile(t, 256)
    chunk = min(GDN_CHUNK, t)
    assert t >= CONV_W - 1 and tm >= SUBLANES and tm % chunk == 0
    wqkv = w_in[:, :3 * d].astype(BF16)
    wba = jnp.pad(w_in[:, 3 * d:3 * d + 2 * H_G], ((0, 0), (0, LANES - 2 * H_G))).astype(BF16)
    wz = w_in[:, 3 * d + 2 * H_G:].astype(BF16)
    cbuf = jnp.pad(conv_buf.astype(F32), ((0, 0), (SUBLANES - (CONV_W - 1), 0), (0, 0)))
    cw = jnp.pad(0.5 * conv_w.astype(F32), ((0, SUBLANES - CONV_W), (0, 0)))
    avec = jnp.pad(a_log.astype(F32), (H_G, LANES - 2 * H_G)).reshape(1, LANES)
    dtvec = jnp.pad(dt_bias.astype(F32), (H_G, LANES - 2 * H_G)).reshape(1, LANES)
    bf16o = jax.ShapeDtypeStruct((b, t, d), BF16)
    tail_spec = pl.BlockSpec((None, SUBLANES, 3 * d), lambda bb, i: (bb, 0, 0))
    return pl.pallas_call(
        functools.partial(_gdn_proj_kernel, chunk=chunk),
        grid=(b, t // tm),
        in_specs=[_rows_spec(tm, d)] + _mod_specs(layer, boff, d, (0, 1))
                 + [_const_spec((d, 3 * d)), _const_spec((d, LANES)), _const_spec((d, d)), tail_spec,
                    _const_spec((SUBLANES, 3 * d)), _const_spec((1, LANES)), _const_spec((1, LANES))],
        out_specs=[_rows_spec(tm, d)] * 3 + [_rows_spec(tm, LANES), _rows_spec(tm, d), tail_spec],
        out_shape=[bf16o, bf16o, bf16o, jax.ShapeDtypeStruct((b, t, LANES), F32), bf16o,
                   jax.ShapeDtypeStruct((b, SUBLANES, 3 * d), F32)],
        scratch_shapes=[pltpu.VMEM((tm + SUBLANES, 3 * d), F32)],
        compiler_params=_cparams("arbitrary", "arbitrary"),
        name="gdn_in_proj",
    )(x, mod4, mod4, wqkv, wba, wz, cbuf, cw, avec, dtvec)


def _unit_lower_inverse_minus_identity(mats):
    n = mats[0].shape[0]
    r = _iota((n, n), 0)
    c = _iota((n, n), 1)

    def mm(xs, ys):
        return [_dot(x.astype(BF16), y.astype(BF16)) for x, y in zip(xs, ys)]

    base = 8
    diag = _div_pow2(r, base) == _div_pow2(c, base)
    d = [jnp.where(diag, a, 0.0) for a in mats]
    d2 = mm(d, d)
    d4 = mm(d2, d2)
    nn = [-x for x in d]
    nn = [x + y + z for x, y, z in zip(nn, d2, mm(nn, d2))]
    nn = [x + y + z for x, y, z in zip(nn, d4, mm(nn, d4))]
    m = base
    while m < n:
        pair = (_div_pow2(r, 2 * m) == _div_pow2(c, 2 * m)) & (_div_pow2(r, m) != _div_pow2(c, m))
        off = [jnp.where(pair, a, 0.0) for a in mats]
        y = [o + p for o, p in zip(off, mm(nn, off))]
        x = [p + q for p, q in zip(y, mm(y, nn))]
        nn = [p - q for p, q in zip(nn, x)]
        m *= 2
    return nn


def _gdn_kernel(q_ref, k_ref, v_ref, gates_ref, z_ref, s0_ref, nw_ref, o_ref, s_ref, *, chunk):
    ti = pl.program_id(1)
    grp, tb, d = z_ref.shape
    dk = d // H_G
    n_chunks = tb // chunk

    @pl.when(ti == 0)
    def _():
        s_ref[...] = s0_ref[...]

    ri = _iota((chunk, chunk), 0)
    ci = _iota((chunk, chunk), 1)
    eye_l = (_iota((LANES, LANES), 0) == _iota((LANES, LANES), 1)).astype(BF16)
    incl = ri >= ci
    strict = ri > ci
    nw = nw_ref[...]
    items = [(g, h) for g in range(grp) for h in range(H_G)]

    def chunk_body(cidx, carry):
        r0 = pl.multiple_of(cidx * chunk, chunk)
        rows = pl.ds(r0, chunk)
        gates = [gates_ref[g, rows, :] for g in range(grp)]
        gates_t = [_dot_exact_nt(eye_l, x) for x in gates]

        kbf = [k_ref[g, rows, h * dk:(h + 1) * dk] for g, h in items]
        qbf = [q_ref[g, rows, h * dk:(h + 1) * dk] for g, h in items]
        q = [x.astype(F32) for x in qbf]
        k = [x.astype(F32) for x in kbf]
        v = [v_ref[g, rows, h * dk:(h + 1) * dk].astype(F32) for g, h in items]
        beta = [gates[g][:, h:h + 1] for g, h in items]
        gcol = [gates[g][:, H_G + h:H_G + h + 1] for g, h in items]
        grow = [gates_t[g][H_G + h:H_G + h + 1, :] for g, h in items]
        dec_incl = [jnp.exp(jnp.where(incl, gc - gr, NEG_INF)) for gc, gr in zip(gcol, grow)]
        kb = [x * bt for x, bt in zip(k, beta)]
        a_mat = [_dot_nt(x.astype(BF16), y) for x, y in zip(kb, kbf)]
        qk = [_dot_nt(x, y) for x, y in zip(qbf, kbf)]
        a_mat = [jnp.where(strict, x * e, 0.0) for x, e in zip(a_mat, dec_incl)]
        qk = [x * e for x, e in zip(qk, dec_incl)]
        exp_g = [jnp.exp(gc) for gc in gcol]
        rhs = [jnp.concatenate([x * bt, y * e], axis=1) for x, bt, y, e in zip(v, beta, kb, exp_g)]
        nn = _unit_lower_inverse_minus_identity(a_mat)
        sol = [x + _dot(y.astype(BF16), x.astype(BF16)) for x, y in zip(rhs, nn)]
        s = [s_ref[g, h] for g, h in items]
        sb = [x.astype(BF16) for x in s]
        v_res = [x[:, :dk] - _dot(x[:, dk:].astype(BF16), y) for x, y in zip(sol, sb)]
        vrb = [x.astype(BF16) for x in v_res]
        o = [_dot((x * e).astype(BF16), y) for x, e, y in zip(q, exp_g, sb)]
        o = [x + _dot(y.astype(BF16), z) for x, y, z in zip(o, qk, vrb)]
        g_last = [gc[chunk - 1:chunk, :] for gc in gcol]
        k_dec = [(x * jnp.exp(gl - gc)).astype(BF16) for x, gl, gc in zip(k, g_last, gcol)]
        s_add = [_dot_tn(x, y) for x, y in zip(k_dec, vrb)]
        for i, (g, h) in enumerate(items):
            s_ref[g, h] = s[i] * jnp.exp(g_last[i]) + s_add[i]
            on = o[i] * lax.rsqrt(jnp.mean(o[i] * o[i], axis=-1, keepdims=True) + NORM_EPS) * nw
            zz = z_ref[g, rows, h * dk:(h + 1) * dk].astype(F32)
            o_ref[g, rows, h * dk:(h + 1) * dk] = (on * _silu(zz)).astype(BF16)
        return carry

    lax.fori_loop(0, n_chunks, chunk_body, 0)


def _gdn_mix(q, k, v, gates, z, s0, norm_w):
    b, t, d = q.shape
    dk = d // H_G
    chunk = min(GDN_CHUNK, t)
    tb = _row_tile(t, 4 * chunk)
    grp = 4 if b % 4 == 0 else (2 if b % 2 == 0 else 1)
    nw = norm_w.astype(F32).reshape(1, dk)

    def rows(n):
        return pl.BlockSpec((grp, tb, n), lambda bb, i: (bb, i, 0))

    state_spec = pl.BlockSpec((grp, H_G, dk, dk), lambda bb, i: (bb, 0, 0, 0))
    return pl.pallas_call(
        functools.partial(_gdn_kernel, chunk=chunk),
        grid=(b // grp, t // tb),
        in_specs=[rows(d)] * 3 + [rows(LANES), rows(d), state_spec, _const_spec((1, dk))],
        out_specs=[rows(d), state_spec],
        out_shape=[jax.ShapeDtypeStruct((b, t, d), BF16), jax.ShapeDtypeStruct((b, H_G, dk, dk), F32)],
        compiler_params=_cparams("arbitrary", "arbitrary"),
        name="gdn_mixer",
    )(q, k, v, gates, z, s0.astype(F32), nw)


def _out_proj_kernel(o_ref, x_ref, gate_ref, w_ref, g_ref, b_ref, y_ref):
    h = _dot(o_ref[...], w_ref[...])
    y = DEEPNORM_ALPHA * x_ref[...] + (1.0 + gate_ref[...]) * h
    mu = jnp.mean(y, axis=-1, keepdims=True)
    yc = y - mu
    var = jnp.mean(yc * yc, axis=-1, keepdims=True)
    y_ref[...] = yc * lax.rsqrt(var + LN_EPS) * g_ref[...] + b_ref[...]


def _out_proj(o, x, mod4, layer, boff, w_out, ln_g, ln_b):
    b, t, d = x.shape
    kdim = o.shape[-1]
    tm = _row_tile(t, 1024)
    return pl.pallas_call(
        _out_proj_kernel,
        grid=(b, t // tm),
        in_specs=[_rows_spec(tm, kdim), _rows_spec(tm, d)] + _mod_specs(layer, boff, d, (2,))
                 + [_const_spec((kdim, d)), _const_spec((1, d)), _const_spec((1, d))],
        out_specs=_rows_spec(tm, d),
        out_shape=jax.ShapeDtypeStruct((b, t, d), F32),
        compiler_params=_cparams("arbitrary", "arbitrary"),
        name="out_proj_postnorm",
    )(o, x, mod4, w_out.astype(BF16), ln_g.reshape(1, d), ln_b.reshape(1, d))


def _qkvz_outputs(u, w_ref, q_ref, k32_ref, v32_ref, kb_ref, vb_ref, z_ref, q_scale, transposed):
    d = u.shape[-1]
    if transposed:
        q_ref[...] = (_dot_nt(w_ref[:, 0:d], u) * q_scale).astype(BF16)
    else:
        q_ref[...] = (_dot(u, w_ref[:, 0:d]) * q_scale).astype(BF16)
    k = _dot(u, w_ref[:, d:2 * d])
    k32_ref[...] = k
    kb_ref[...] = k.astype(BF16)
    v = _dot(u, w_ref[:, 2 * d:3 * d])
    v32_ref[...] = v
    vb_ref[...] = (v.T if transposed else v).astype(BF16)
    z_ref[...] = _dot(u, w_ref[:, 3 * d:4 * d]).astype(BF16)


def _qkvz_weights(w_in, d, transposed):
    w = w_in[:, :4 * d]
    if transposed:
        w = jnp.concatenate([w[:, :d].T, w[:, d:]], axis=1)
    return w.astype(BF16)


def _qkvz_specs(b, t, d, tm, transposed):
    cols_spec = pl.BlockSpec((None, d, tm), lambda bb, i: (bb, 0, i))
    rows = _rows_spec(tm, d)
    f32o = jax.ShapeDtypeStruct((b, t, d), F32)
    bf16o = jax.ShapeDtypeStruct((b, t, d), BF16)
    bf16t = jax.ShapeDtypeStruct((b, d, t), BF16)
    if transposed:
        return [cols_spec, rows, rows, rows, cols_spec, rows], [bf16t, f32o, f32o, bf16o, bf16t, bf16o]
    return [rows] * 6, [bf16o, f32o, f32o, bf16o, bf16o, bf16o]


def _fox_proj_kernel(x_ref, shift_ref, scale_ref, w_ref, wf_ref, bf_ref,
                     q_ref, k32_ref, v32_ref, kb_ref, vb_ref, z_ref, logf_ref, *, q_scale, transposed):
    u = _modulated(x_ref, shift_ref, scale_ref)
    _qkvz_outputs(u, w_ref, q_ref, k32_ref, v32_ref, kb_ref, vb_ref, z_ref, q_scale, transposed)
    f = _dot(u, wf_ref[...])[:, :H_F] + bf_ref[...]
    logf_ref[...] = -_softplus(-f)


def _fox_proj(x, mod4, layer, boff, w_in, b_f, transposed):
    b, t, d = x.shape
    tm = _row_tile(t, 512)
    w = _qkvz_weights(w_in, d, transposed)
    wf = jnp.pad(w_in[:, 4 * d:], ((0, 0), (0, LANES - H_F))).astype(BF16)
    out_specs, out_shape = _qkvz_specs(b, t, d, tm, transposed)
    return pl.pallas_call(
        functools.partial(_fox_proj_kernel, q_scale=(d // H_F) ** -0.5 * (LOG2E if transposed else 1.0),
                          transposed=transposed),
        grid=(b, t // tm),
        in_specs=[_rows_spec(tm, d)] + _mod_specs(layer, boff, d, (0, 1))
                 + [_const_spec((d, 4 * d)), _const_spec((d, LANES)), _const_spec((1, H_F))],
        out_specs=out_specs + [_rows_spec(tm, H_F)],
        out_shape=out_shape + [jax.ShapeDtypeStruct((b, t, H_F), F32)],
        compiler_params=_cparams("arbitrary", "arbitrary"),
        name="fox_in_proj",
    )(x, mod4, mod4, w, wf, b_f.astype(F32).reshape(1, H_F))


def _cumsum_kernel(x_ref, c0_ref, cn_ref, ct_ref, *, blk):
    s, h = x_ref.shape
    tri = (_iota((blk, blk), 0) >= _iota((blk, blk), 1)).astype(BF16)
    eye_h = (_iota((h, h), 0) == _iota((h, h), 1)).astype(BF16)
    carry = c0_ref[...]
    for i in range(s // blk):
        c = _dot_exact_l(tri, x_ref[i * blk:(i + 1) * blk, :]) + carry
        cn_ref[i * blk:(i + 1) * blk, :] = c
        ct_ref[:, i * blk:(i + 1) * blk] = _dot_exact_nt(eye_h, c)
        carry = c[blk - 1:blk, :]


def _cumsum_time(x, c0):
    b, s, h = x.shape
    blk = 256 if s % 256 == 0 else s
    return pl.pallas_call(
        functools.partial(_cumsum_kernel, blk=blk),
        grid=(b,),
        in_specs=[pl.BlockSpec((None, s, h), lambda bb: (bb, 0, 0)),
                  pl.BlockSpec((None, 1, h), lambda bb: (bb, 0, 0))],
        out_specs=[pl.BlockSpec((None, s, h), lambda bb: (bb, 0, 0)),
                   pl.BlockSpec((None, h, s), lambda bb: (bb, 0, 0))],
        out_shape=[jax.ShapeDtypeStruct((b, s, h), F32), jax.ShapeDtypeStruct((b, h, s), F32)],
        compiler_params=_cparams("arbitrary"),
        name="logf_cumsum",
    )(x, c0)


AUG = LANES // H_F


def _fox_aug_kernel(x_ref, kaug_ref, qaugt_ref, *, blk):
    s, h = x_ref.shape
    tri = (_iota((blk, blk), 0) >= _iota((blk, blk), 1)).astype(BF16)
    lane_h = _iota((h, LANES), 1)
    row_h = _iota((h, LANES), 0)
    ek = [jnp.where(lane_h == row_h * AUG + part, -1.0, 0.0).astype(BF16) for part in range(3)]
    row_q = _iota((LANES, h), 0)
    col_q = _iota((LANES, h), 1)
    eq = [jnp.where(row_q == col_q * AUG + 3 + part, 1.0, 0.0).astype(BF16) for part in range(3)]
    k_slot = _iota((blk, LANES), 1) & (AUG - 1)
    k_ones = jnp.where(k_slot >= 3, jnp.where(k_slot < 6, 1.0, 0.0), 0.0)
    q_ones = jnp.where((_iota((LANES, blk), 0) & (AUG - 1)) < 3, 1.0, 0.0)
    carry = jnp.zeros((1, h), F32)
    for i in range(s // blk):
        c = _dot_exact_l(tri, x_ref[i * blk:(i + 1) * blk, :]) + carry
        parts = _split3(c * LOG2E)
        kaug = k_ones
        qaugt = q_ones
        for part in range(3):
            kaug = kaug + _dot(parts[part], ek[part])
            qaugt = qaugt + _dot_nt(eq[part], parts[part])
        kaug_ref[i * blk:(i + 1) * blk, :] = kaug.astype(BF16)
        qaugt_ref[:, i * blk:(i + 1) * blk] = qaugt.astype(BF16)
        carry = c[blk - 1:blk, :]


def _fox_aug(logf):
    b, s, h = logf.shape
    assert h * AUG == LANES and AUG >= 6
    blk = 256 if s % 256 == 0 else s
    return pl.pallas_call(
        functools.partial(_fox_aug_kernel, blk=blk),
        grid=(b,),
        in_specs=[pl.BlockSpec((None, s, h), lambda bb: (bb, 0, 0))],
        out_specs=[pl.BlockSpec((None, s, LANES), lambda bb: (bb, 0, 0)),
                   pl.BlockSpec((None, LANES, s), lambda bb: (bb, 0, 0))],
        out_shape=[jax.ShapeDtypeStruct((b, s, LANES), BF16), jax.ShapeDtypeStruct((b, LANES, s), BF16)],
        compiler_params=_cparams("arbitrary"),
        name="fox_bias_operands",
    )(logf)


SUM_ROWS = 16


def _softmax_t_probs(s, m_ref):
    m_prev = m_ref[...]
    m_new = jnp.maximum(m_prev, jnp.max(s, axis=0, keepdims=True))
    m_ref[...] = m_new
    return jnp.exp2(s - m_new).astype(BF16), jnp.exp2(m_prev - m_new)


def _softmax_t_accumulate(vt, p, alpha, acc_ref):
    vt_ext = jnp.concatenate([vt, jnp.ones((SUM_ROWS, vt.shape[1]), BF16)], axis=0)
    acc_ref[...] = alpha * acc_ref[...] + _dot(vt_ext, p)


def _softmax_t_result(acc_ref, dv):
    return acc_ref[0:dv, :] / acc_ref[dv:dv + 1, :]


def _fox_attn_kernel(qt_ref, k_ref, kaug_ref, vt_ref, qaugt_ref, z_ref, o_ref,
                     qcat_ref, s_ref, m_ref, acc_ref, *, tk):
    hp = pl.program_id(1)
    qi = pl.program_id(2)
    tq = qt_ref.shape[1]
    assert tq == 2 * tk
    dh = LANES // 2
    row = _iota((LANES, tq), 0)
    qt = qt_ref[...].astype(F32)
    qa = qaugt_ref[...].astype(F32)
    for hh in range(2):
        qcat_ref[hh, 0:LANES, :] = jnp.where(_div_pow2(row, dh) == hh, qt, 0.0).astype(BF16)
        qcat_ref[hh, LANES:2 * LANES, :] = jnp.where(_div_pow2(row, AUG) == hp * 2 + hh, qa, 0.0).astype(BF16)
    m_ref[...] = jnp.full(m_ref.shape, NEG_INF, F32)
    acc_ref[...] = jnp.zeros(acc_ref.shape, F32)
    n_full = qi * 2
    key_in = _iota((tk, tk), 0)
    qry_in = _iota((tk, tk), 1)

    def scores(j, slot):
        k0 = pl.multiple_of(j * tk, tk)
        kcat = jnp.concatenate([k_ref[pl.ds(k0, tk), :], kaug_ref[pl.ds(k0, tk), :]], axis=1)
        for hh in range(2):
            s_ref[slot, hh] = _dot(kcat, qcat_ref[hh])

    def consume(j, slot, masked, q0=0):
        k0 = pl.multiple_of(j * tk, tk)
        for hh in range(2):
            for half in range(q0 // tk, 2):
                s = s_ref[slot, hh, :, half * tk:(half + 1) * tk]
                if masked:
                    s = jnp.where(k0 + key_in <= qi * tq + half * tk + qry_in, s, NEG_INF)
                p, alpha = _softmax_t_probs(s, m_ref.at[hh, half])
                _softmax_t_accumulate(vt_ref[hh * dh:(hh + 1) * dh, pl.ds(k0, tk)], p, alpha, acc_ref.at[hh, half])

    def pair(cur, a):
        nxt = 1 - cur
        scores(a + 2, 2 * nxt)
        consume(a, 2 * cur, False)
        scores(a + 3, 2 * nxt + 1)
        consume(a + 1, 2 * cur + 1, False)

    def last_pair(cur):
        consume(n_full, 2 * cur, True)
        consume(n_full + 1, 2 * cur + 1, True, q0=tk)

    scores(0, 0)
    scores(1, 1)

    def body(i, carry):
        pair(0, 4 * i)
        pair(1, 4 * i + 2)
        return carry

    lax.fori_loop(0, lax.shift_right_logical(qi, 1), body, 0)

    @pl.when((qi & 1) == 1)
    def _():
        pair(0, n_full - 2)
        last_pair(1)

    @pl.when((qi & 1) == 0)
    def _():
        last_pair(0)
    o_t = jnp.concatenate(
        [jnp.concatenate([_softmax_t_result(acc_ref.at[hh, half], dh) for half in range(2)], axis=1)
         for hh in range(2)], axis=0)
    o_ref[...] = (o_t.T * _silu(z_ref[...].astype(F32))).astype(BF16)


def _fox_attn_prompt(qt, kb, kaug, vt, qaugt, z):
    b, d, t = qt.shape
    tq, tk = ATTN_TQ, ATTN_TK
    assert t % tq == 0
    hpairs = d // LANES
    dh = LANES // 2
    return pl.pallas_call(
        functools.partial(_fox_attn_kernel, tk=tk),
        grid=(b, hpairs, t // tq),
        in_specs=[pl.BlockSpec((None, LANES, tq), lambda bb, hp, i: (bb, hp, i)),
                  pl.BlockSpec((None, t, LANES), lambda bb, hp, i: (bb, 0, hp)),
                  pl.BlockSpec((None, t, LANES), lambda bb, hp, i: (bb, 0, 0)),
                  pl.BlockSpec((None, LANES, t), lambda bb, hp, i: (bb, hp, 0)),
                  pl.BlockSpec((None, LANES, tq), lambda bb, hp, i: (bb, 0, i)),
                  pl.BlockSpec((None, tq, LANES), lambda bb, hp, i: (bb, i, hp))],
        out_specs=pl.BlockSpec((None, tq, LANES), lambda bb, hp, i: (bb, i, hp)),
        out_shape=jax.ShapeDtypeStruct((b, t, d), BF16),
        scratch_shapes=[pltpu.VMEM((2, 2 * LANES, tq), BF16), pltpu.VMEM((4, 2, tk, tq), F32),
                        pltpu.VMEM((2, 2, 1, tk), F32), pltpu.VMEM((2, 2, dh + SUM_ROWS, tk), F32)],
        compiler_params=_cparams("arbitrary", "arbitrary", "arbitrary"),
        name="fox_attention_prompt",
    )(qt, kb, kaug, vt, qaugt, z)


def _fox_decode_kernel(q_ref, kp_ref, vp_ref, kn_ref, vn_ref, z_ref, cq_ref, ckt_ref, o_ref):
    hp = pl.program_id(1)
    t = q_ref.shape[0]
    p_len = kp_ref.shape[0]
    dh = LANES // 2
    lane = _iota((t, LANES), 1)
    q = q_ref[...]
    kp = kp_ref[...].astype(BF16)
    vp = vp_ref[...].astype(BF16)
    kn = kn_ref[...]
    vn = vn_ref[...]
    cq_all = cq_ref[...]
    hlane = _iota(cq_all.shape, 1)
    row2 = _iota((2 * t, t), 0)
    causal = _iota((2 * t, t), 1) <= jnp.where(row2 >= t, row2 - t, row2)
    qm, gate = [], []
    for hh in range(2):
        h = hp * 2 + hh
        qm.append(jnp.where(_div_pow2(lane, dh) == hh, q, jnp.zeros_like(q)))
        cq = jnp.sum(jnp.where(hlane == h, cq_all, 0.0), axis=-1, keepdims=True)
        gate.append(cq - ckt_ref[pl.ds(h, 1), :])
    qm = jnp.concatenate(qm, axis=0)
    gate = jnp.concatenate(gate, axis=0)
    s_p = _dot_nt(qm, kp) + gate[:, :p_len]
    s_n = jnp.where(causal, _dot_nt(qm, kn) + gate[:, p_len:], NEG_INF)
    m = jnp.maximum(jnp.max(s_p, axis=-1, keepdims=True), jnp.max(s_n, axis=-1, keepdims=True))
    e_p = jnp.exp(s_p - m)
    e_n = jnp.exp(s_n - m)
    den = jnp.sum(e_p, axis=-1, keepdims=True) + jnp.sum(e_n, axis=-1, keepdims=True)
    o2 = (_dot(e_p.astype(BF16), vp) + _dot(e_n.astype(BF16), vn)) / den
    o = jnp.where(lane < dh, o2[:t], o2[t:])
    o_ref[...] = (o * _silu(z_ref[...].astype(F32))).astype(BF16)


def _fox_attn_sample(q, k_past, v_past, kb, vb, z, cq_new, cum_t):
    b, t, d = q.shape
    p_len = k_past.shape[1]
    hpairs = d // LANES
    new_spec = pl.BlockSpec((None, t, LANES), lambda bb, hp: (bb, 0, hp))
    past_spec = pl.BlockSpec((None, p_len, LANES), lambda bb, hp: (bb, 0, hp))
    return pl.pallas_call(
        _fox_decode_kernel,
        grid=(b, hpairs),
        in_specs=[new_spec, past_spec, past_spec, new_spec, new_spec, new_spec,
                  pl.BlockSpec((None, t, H_F), lambda bb, hp: (bb, 0, 0)),
                  pl.BlockSpec((None, H_F, p_len + t), lambda bb, hp: (bb, 0, 0))],
        out_specs=new_spec,
        out_shape=jax.ShapeDtypeStruct((b, t, d), BF16),
        compiler_params=_cparams("arbitrary", "arbitrary"),
        name="fox_attention_sample",
    )(q, k_past, v_past, kb, vb, z, cq_new, cum_t)


def _diff_proj_kernel(x_ref, shift_ref, scale_ref, w_ref, q_ref, k32_ref, v32_ref, kb_ref, vb_ref, z_ref,
                      *, q_scale, transposed):
    u = _modulated(x_ref, shift_ref, scale_ref)
    _qkvz_outputs(u, w_ref, q_ref, k32_ref, v32_ref, kb_ref, vb_ref, z_ref, q_scale, transposed)


def _diff_proj(x, mod4, layer, boff, w_in, transposed):
    b, t, d = x.shape
    tm = _row_tile(t, 512)
    out_specs, out_shape = _qkvz_specs(b, t, d, tm, transposed)
    return pl.pallas_call(
        functools.partial(_diff_proj_kernel, q_scale=(d // (2 * H_D)) ** -0.5 * (LOG2E if transposed else 1.0),
                          transposed=transposed),
        grid=(b, t // tm),
        in_specs=[_rows_spec(tm, d)] + _mod_specs(layer, boff, d, (0, 1)) + [_const_spec((d, 4 * d))],
        out_specs=out_specs,
        out_shape=out_shape,
        compiler_params=_cparams("arbitrary", "arbitrary"),
        name="diff_in_proj",
    )(x, mod4, mod4, _qkvz_weights(w_in, d, transposed))


def _t5_thresholds():
    nb = N_BUCKETS // 2
    max_exact = nb // 2
    steps = nb - max_exact
    ratio = MAX_DISTANCE // max_exact
    out = []
    for kk in range(1, nb - max_exact):
        target = max_exact ** steps * ratio ** kk
        n = max_exact
        while n ** steps < target:
            n += 1
        out.append(n)
    return nb, max_exact, out


def _bias_kernel(tbl_ref, o_ref, *, q0, k0, keys_on_rows):
    h = pl.program_id(0)
    shape = o_ref.shape
    kdim, qdim = (0, 1) if keys_on_rows else (1, 0)
    rel = (k0 + _iota(shape, kdim)) - (q0 + _iota(shape, qdim))
    nb, max_exact, thr = _t5_thresholds()
    n = jnp.abs(rel)
    large = jnp.full(shape, max_exact, jnp.int32)
    for tval in thr:
        large = large + (n >= tval).astype(jnp.int32)
    bucket = jnp.where(rel > 0, nb, 0) + jnp.where(n < max_exact, n, large)
    acc = jnp.zeros(shape, F32)
    for bkt in range(N_BUCKETS):
        acc = jnp.where(bucket == bkt, tbl_ref[bkt * H_D + h], acc)
    if keys_on_rows:
        acc = (acc - tbl_ref[(nb - 1) * H_D + h]) * LOG2E
    o_ref[...] = acc


def _bias_tile(rel_table, q0, nq, k0, nk, keys_on_rows=False):
    shape = (nk, nq) if keys_on_rows else (nq, nk)
    return pl.pallas_call(
        functools.partial(_bias_kernel, q0=q0, k0=k0, keys_on_rows=keys_on_rows),
        grid=(H_D,),
        in_specs=[pl.BlockSpec(memory_space=pltpu.SMEM)],
        out_specs=pl.BlockSpec((None,) + shape, lambda h: (h, 0, 0)),
        out_shape=jax.ShapeDtypeStruct((H_D,) + shape, F32),
        compiler_params=_cparams("arbitrary"),
        name="t5_bias_tile",
    )(rel_table.astype(F32).reshape(N_BUCKETS * H_D))


def _diff_lambda(lam_ref):
    lam = lam_ref[...]
    s1 = jnp.sum(lam[0:1, :] * lam[1:2, :], axis=-1, keepdims=True)
    s2 = jnp.sum(lam[2:3, :] * lam[3:4, :], axis=-1, keepdims=True)
    return jnp.exp(s1) - jnp.exp(s2) + LAMBDA_INIT


def _diff_epilogue(o, z, subln_ref):
    on = o * lax.rsqrt(jnp.mean(o * o, axis=-1, keepdims=True) + NORM_EPS) * subln_ref[...]
    on = on * (1.0 - LAMBDA_INIT)
    return (on * _silu(z.astype(F32))).astype(BF16)


def _diff_attn_kernel(tbl_ref, qt_ref, k_ref, vt_ref, z_ref, biasm_ref, bias0_ref, bias1_ref, lam_ref, subln_ref,
                      o_ref, qcat_ref, s_ref, m_ref, acc_ref, *, tk):
    h = pl.program_id(1)
    qi = pl.program_id(2)
    tq = qt_ref.shape[1]
    assert tq == 2 * tk
    dh = LANES // 2
    row = _iota((LANES, tq), 0)
    qt = qt_ref[...].astype(F32)
    nb, _, _ = _t5_thresholds()
    far = _split3(jnp.full((LANES, tq), tbl_ref[(nb - 1) * H_D + h], F32) * LOG2E)
    far_rows = jnp.zeros((LANES, tq), F32)
    for part in range(3):
        far_rows = jnp.where(row == part, far[part].astype(F32), far_rows)
    for br in range(2):
        qcat_ref[br, 0:LANES, :] = jnp.where(_div_pow2(row, dh) == br, qt, 0.0).astype(BF16)
        qcat_ref[br, LANES:2 * LANES, :] = far_rows.astype(BF16)
    ones_aug = jnp.where(_iota((tk, LANES), 1) < 3, 1.0, 0.0).astype(BF16)
    m_ref[...] = jnp.full(m_ref.shape, NEG_INF, F32)
    acc_ref[...] = jnp.zeros(acc_ref.shape, F32)
    key_in = _iota((tk, tk), 0)
    qry_in = _iota((tk, tk), 1)

    def scores(j, slot):
        k0 = pl.multiple_of(j * tk, tk)
        kcat = jnp.concatenate([k_ref[pl.ds(k0, tk), :], ones_aug], axis=1)
        for br in range(2):
            s_ref[slot, br] = _dot(kcat, qcat_ref[br])

    def consume(j, slot, bias_ref=None, key_off=None, q0=0):
        k0 = pl.multiple_of(j * tk, tk)
        for br in range(2):
            for half in range(q0 // tk, 2):
                lanes = slice(half * tk, (half + 1) * tk)
                s = s_ref[slot, br, :, lanes]
                if bias_ref is not None:
                    s = s + bias_ref[:, lanes]
                if key_off is not None:
                    s = jnp.where(_div_pow2(key_off + key_in, ATTN_CHUNK)
                                  <= _div_pow2(half * tk + qry_in, ATTN_CHUNK), s, NEG_INF)
                p, alpha = _softmax_t_probs(s, m_ref.at[br, half])
                _softmax_t_accumulate(vt_ref[:, pl.ds(k0, tk)], p, alpha, acc_ref.at[br, half])

    def pair(cur, a, second_bias_ref=None):
        nxt = 1 - cur
        scores(a + 2, 2 * nxt)
        consume(a, 2 * cur)
        scores(a + 3, 2 * nxt + 1)
        consume(a + 1, 2 * cur + 1, bias_ref=second_bias_ref)

    def last_pair(cur):
        consume(2 * qi, 2 * cur, bias_ref=bias0_ref, key_off=0)
        consume(2 * qi + 1, 2 * cur + 1, bias_ref=bias1_ref, key_off=tk, q0=tk)

    scores(0, 0)
    scores(1, 1)
    n_plain = jnp.maximum(qi - 1, 0)

    def body(i, carry):
        pair(0, 4 * i)
        pair(1, 4 * i + 2)
        return carry

    lax.fori_loop(0, lax.shift_right_logical(n_plain, 1), body, 0)

    @pl.when(qi == 0)
    def _():
        last_pair(0)

    @pl.when((qi & 1) == 1)
    def _():
        pair(0, 2 * qi - 2, second_bias_ref=biasm_ref)
        last_pair(1)

    @pl.when(jnp.logical_and(qi >= 2, (qi & 1) == 0))
    def _():
        pair(0, 2 * qi - 4)
        pair(1, 2 * qi - 2, second_bias_ref=biasm_ref)
        last_pair(0)
    branch = [jnp.concatenate([_softmax_t_result(acc_ref.at[br, half], LANES) for half in range(2)], axis=1)
              for br in range(2)]
    o_t = branch[0] - _diff_lambda(lam_ref) * branch[1]
    o_ref[...] = _diff_epilogue(o_t.T, z_ref[...], subln_ref)


def _lam_pack(lam_q1, lam_k1, lam_q2, lam_k2):
    rows = jnp.stack([lam_q1, lam_k1, lam_q2, lam_k2]).astype(F32)
    return jnp.pad(rows, ((0, SUBLANES - 4), (0, LANES - rows.shape[1])))


def _diff_attn_prompt(qt, kb, vt, z, rel_table, lam, subln_w):
    b, d, t = qt.shape
    tq, tk = ATTN_TQ, ATTN_TK
    assert t % tq == 0 and tk % ATTN_CHUNK == 0 and tk >= MAX_DISTANCE
    biasm = _bias_tile(rel_table, tk, tq, 0, tk, keys_on_rows=True)
    bias0 = _bias_tile(rel_table, 0, tq, 0, tk, keys_on_rows=True)
    bias1 = _bias_tile(rel_table, 0, tq, tk, tk, keys_on_rows=True)
    rows_spec = pl.BlockSpec((None, tq, LANES), lambda bb, h, i: (bb, i, h))
    bias_spec = pl.BlockSpec((None, tk, tq), lambda bb, h, i: (h, 0, 0))
    return pl.pallas_call(
        functools.partial(_diff_attn_kernel, tk=tk),
        grid=(b, H_D, t // tq),
        in_specs=[pl.BlockSpec(memory_space=pltpu.SMEM),
                  pl.BlockSpec((None, LANES, tq), lambda bb, h, i: (bb, h, i)),
                  pl.BlockSpec((None, t, LANES), lambda bb, h, i: (bb, 0, h)),
                  pl.BlockSpec((None, LANES, t), lambda bb, h, i: (bb, h, 0)),
                  rows_spec, bias_spec, bias_spec, bias_spec,
                  pl.BlockSpec((SUBLANES, LANES), lambda bb, h, i: (0, 0)),
                  pl.BlockSpec((1, LANES), lambda bb, h, i: (0, 0))],
        out_specs=rows_spec,
        out_shape=jax.ShapeDtypeStruct((b, t, d), BF16),
        scratch_shapes=[pltpu.VMEM((2, 2 * LANES, tq), BF16), pltpu.VMEM((4, 2, tk, tq), F32),
                        pltpu.VMEM((2, 2, 1, tk), F32), pltpu.VMEM((2, 2, LANES + SUM_ROWS, tk), F32)],
        compiler_params=_cparams("arbitrary", "arbitrary", "arbitrary"),
        name="diff_attention_prompt",
    )(rel_table.astype(F32).reshape(N_BUCKETS * H_D), qt, kb, vt, z, biasm, bias0, bias1, lam,
      subln_w.astype(F32).reshape(1, LANES))


def _diff_decode_kernel(q_ref, kp_ref, vp_ref, kn_ref, vn_ref, z_ref, bias_ref, lam_ref, subln_ref, o_ref,
                        *, p_len):
    t = q_ref.shape[0]
    dh = LANES // 2
    lane = _iota((t, LANES), 1)
    q = q_ref[...]
    kp = kp_ref[...].astype(BF16)
    vp = vp_ref[...].astype(BF16)
    kn = kn_ref[...]
    vn = vn_ref[...]
    bias = jnp.concatenate([bias_ref[...], bias_ref[...]], axis=0)
    row_p = _iota((2 * t, p_len), 0)
    row_n = _iota((2 * t, t), 0)
    qp_chunk = _div_pow2(p_len + jnp.where(row_p >= t, row_p - t, row_p), ATTN_CHUNK)
    qn_chunk = _div_pow2(p_len + jnp.where(row_n >= t, row_n - t, row_n), ATTN_CHUNK)
    kp_chunk = _div_pow2(_iota((2 * t, p_len), 1), ATTN_CHUNK)
    kn_chunk = _div_pow2(p_len + _iota((2 * t, t), 1), ATTN_CHUNK)
    qm = jnp.concatenate([jnp.where(_div_pow2(lane, dh) == br, q, jnp.zeros_like(q)) for br in range(2)], axis=0)
    s_p = jnp.where(kp_chunk <= qp_chunk, _dot_nt(qm, kp) + bias[:, :p_len], NEG_INF)
    s_n = jnp.where(kn_chunk <= qn_chunk, _dot_nt(qm, kn) + bias[:, p_len:], NEG_INF)
    m = jnp.maximum(jnp.max(s_p, axis=-1, keepdims=True), jnp.max(s_n, axis=-1, keepdims=True))
    e_p = jnp.exp(s_p - m)
    e_n = jnp.exp(s_n - m)
    den = jnp.sum(e_p, axis=-1, keepdims=True) + jnp.sum(e_n, axis=-1, keepdims=True)
    o2 = (_dot(e_p.astype(BF16), vp) + _dot(e_n.astype(BF16), vn)) / den
    o = o2[:t] - _diff_lambda(lam_ref) * o2[t:]
    o_ref[...] = _diff_epilogue(o, z_ref[...], subln_ref)


def _diff_attn_sample(q, k_past, v_past, kb, vb, z, rel_table, lam, subln_w):
    b, t, d = q.shape
    p_len = k_past.shape[1]
    bias = _bias_tile(rel_table, p_len, t, 0, p_len + t)
    new_spec = pl.BlockSpec((None, t, LANES), lambda bb, h: (bb, 0, h))
    past_spec = pl.BlockSpec((None, p_len, LANES), lambda bb, h: (bb, 0, h))
    return pl.pallas_call(
        functools.partial(_diff_decode_kernel, p_len=p_len),
        grid=(b, H_D),
        in_specs=[new_spec, past_spec, past_spec, new_spec, new_spec, new_spec,
                  pl.BlockSpec((None, t, p_len + t), lambda bb, h: (h, 0, 0)),
                  pl.BlockSpec((SUBLANES, LANES), lambda bb, h: (0, 0)),
                  pl.BlockSpec((1, LANES), lambda bb, h: (0, 0))],
        out_specs=new_spec,
        out_shape=jax.ShapeDtypeStruct((b, t, d), BF16),
        compiler_params=_cparams("arbitrary", "arbitrary"),
        name="diff_attention_sample",
    )(q, k_past, v_past, kb, vb, z, bias, lam, subln_w.astype(F32).reshape(1, LANES))


def _rope_kernel(inv_ref, cos_ref, sin_ref, *, start):
    t, w = cos_ref.shape
    pos = (start + pl.program_id(0) * t + _iota((t, w), 0)).astype(F32)
    ang = pos * inv_ref[...]
    even = (_iota((t, w), 1) & 1) == 0
    cos_ref[...] = jnp.cos(ang)
    sn = jnp.sin(ang)
    sin_ref[...] = jnp.where(even, -sn, sn)


def _rope_tables(t, start, dk):
    inv_half = np.power(np.float32(ROPE_BASE), -np.arange(0, dk, 2, dtype=np.float32) / np.float32(dk))
    inv = jnp.asarray(np.repeat(inv_half.astype(np.float32), 2).reshape(1, dk))
    tt = _row_tile(t, 512)
    return pl.pallas_call(
        functools.partial(_rope_kernel, start=start),
        grid=(t // tt,),
        in_specs=[pl.BlockSpec((1, dk), lambda i: (0, 0))],
        out_specs=[pl.BlockSpec((tt, dk), lambda i: (i, 0))] * 2,
        out_shape=[jax.ShapeDtypeStruct((t, dk), F32)] * 2,
        compiler_params=_cparams("arbitrary"),
        name="rope_tables",
    )(inv)


def _rotate_pairs(x, cos, sin_signed):
    slabs = []
    for c0 in range(0, x.shape[-1], LANES):
        xs = x[:, c0:c0 + LANES]
        even = (_iota(xs.shape, 1) & 1) == 0
        slabs.append(jnp.where(even, pltpu.roll(xs, LANES - 1, 1), pltpu.roll(xs, 1, 1)))
    return x * cos + jnp.concatenate(slabs, axis=1) * sin_signed


def _ret_proj_kernel(x_ref, shift_ref, scale_ref, w_ref, cos_ref, sin_ref, q_ref, k_ref, v_ref, z_ref,
                     *, q_scale):
    u = _modulated(x_ref, shift_ref, scale_ref)
    d = x_ref.shape[-1]
    dk = cos_ref.shape[-1]
    cos = cos_ref[...]
    sn = sin_ref[...]
    for h in range(d // dk):
        qh = _dot(u, w_ref[:, h * dk:(h + 1) * dk])
        q_ref[:, h * dk:(h + 1) * dk] = (_rotate_pairs(qh, cos, sn) * q_scale).astype(BF16)
        kh = _dot(u, w_ref[:, d + h * dk:d + (h + 1) * dk])
        k_ref[:, h * dk:(h + 1) * dk] = _rotate_pairs(kh, cos, sn).astype(BF16)
    for s in range(2):
        v_ref[:, s * d:(s + 1) * d] = _dot(u, w_ref[:, (2 + s) * d:(3 + s) * d]).astype(BF16)
        z_ref[:, s * d:(s + 1) * d] = _dot(u, w_ref[:, (4 + s) * d:(5 + s) * d]).astype(BF16)


def _ret_proj(x, mod4, layer, boff, w_in, cos, sin_signed):
    b, t, d = x.shape
    dk = d // H_R
    tm = _row_tile(t, 512)
    tab_spec = pl.BlockSpec((tm, dk), lambda bb, i: (i, 0))
    return pl.pallas_call(
        functools.partial(_ret_proj_kernel, q_scale=dk ** -0.5),
        grid=(b, t // tm),
        in_specs=[_rows_spec(tm, d)] + _mod_specs(layer, boff, d, (0, 1)) + [_const_spec((d, 6 * d)), tab_spec, tab_spec],
        out_specs=[_rows_spec(tm, d), _rows_spec(tm, d), _rows_spec(tm, 2 * d), _rows_spec(tm, 2 * d)],
        out_shape=[jax.ShapeDtypeStruct((b, t, d), BF16), jax.ShapeDtypeStruct((b, t, d), BF16),
                   jax.ShapeDtypeStruct((b, t, 2 * d), BF16), jax.ShapeDtypeStruct((b, t, 2 * d), BF16)],
        compiler_params=_cparams("arbitrary", "arbitrary"),
        name="ret_in_proj",
    )(x, mod4, mod4, w_in.astype(BF16), cos, sin_signed)


def _ret_kernel(q_ref, k_ref, v_ref, z_ref, s0_ref, gn_ref, o_ref, s_ref, intra_ref):
    ti = pl.program_id(1)
    lr = q_ref.shape[0]
    dk = q_ref.shape[-1] // H_R
    dv = v_ref.shape[-1] // H_R
    heads = range(H_R)
    log_gamma = [math.log1p(-(2.0 ** (-5.0 - h))) for h in heads]

    @pl.when(ti == 0)
    def _():
        s_ref[...] = s0_ref[...]
        rel = (_iota((lr, lr), 0) - _iota((lr, lr), 1)).astype(F32)
        for h in heads:
            intra_ref[h] = jnp.where(rel >= 0, jnp.exp(log_gamma[h] * jnp.maximum(rel, 0.0)), 0.0)

    idx = _iota((lr, 1), 0).astype(F32)
    for h in heads:
        q_dec = jnp.exp(log_gamma[h] * (idx + 1.0))
        k_dec = jnp.exp(log_gamma[h] * (lr - 1.0 - idx))
        qh = q_ref[:, h * dk:(h + 1) * dk]
        kh = k_ref[:, h * dk:(h + 1) * dk]
        vh = v_ref[:, h * dv:(h + 1) * dv]
        s = s_ref[h]
        att = _dot_nt(qh, kh) * intra_ref[h]
        o = _dot(att.astype(BF16), vh) + _dot(qh, s.astype(BF16)) * q_dec
        s_ref[h] = s * math.exp(log_gamma[h] * lr) + _dot_tn((kh.astype(F32) * k_dec).astype(BF16), vh)
        mu = jnp.mean(o, axis=-1, keepdims=True)
        oc = o - mu
        var = jnp.mean(oc * oc, axis=-1, keepdims=True)
        on = oc * lax.rsqrt(var + LN_EPS) * gn_ref[:, h * dv:(h + 1) * dv]
        zz = z_ref[:, h * dv:(h + 1) * dv].astype(F32)
        o_ref[:, h * dv:(h + 1) * dv] = (on * _silu(zz)).astype(BF16)


def _ret_mix(q, k, v, z, s0, gn_w):
    b, t, d = q.shape
    dk = d // H_R
    dv = v.shape[-1] // H_R
    lr = _row_tile(t, 256)
    state_spec = pl.BlockSpec((None, H_R, dk, dv), lambda bb, i: (bb, 0, 0, 0))
    return pl.pallas_call(
        _ret_kernel,
        grid=(b, t // lr),
        in_specs=[_rows_spec(lr, d), _rows_spec(lr, d), _rows_spec(lr, 2 * d), _rows_spec(lr, 2 * d), state_spec,
                  _const_spec((1, 2 * d))],
        out_specs=[_rows_spec(lr, 2 * d), state_spec],
        out_shape=[jax.ShapeDtypeStruct((b, t, 2 * d), BF16), jax.ShapeDtypeStruct((b, H_R, dk, dv), F32)],
        scratch_shapes=[pltpu.VMEM((H_R, lr, lr), F32)],
        compiler_params=_cparams("arbitrary", "arbitrary"),
        name="retention_mixer",
    )(q, k, v, z, s0.astype(F32), gn_w.astype(F32).reshape(1, 2 * d))


def _run_group(x, mod4, boff, state_gdn, state_gdn_conv, cache_fox_k, cache_fox_v, cache_fox_logf,
               cache_diff_k, cache_diff_v, state_ret, start, p):
    b, t, d = x.shape
    dk_g = d // H_G

    if state_gdn is None:
        state_gdn = jnp.zeros((b, H_G, dk_g, dk_g), F32)
        state_gdn_conv = jnp.zeros((b, CONV_W - 1, 3 * d), F32)
    q, k, v, gates, z, tail = _gdn_proj(x, mod4, 0, boff, p["gdn_w_in"], state_gdn_conv, p["gdn_conv_w"],
                                        p["gdn_a_log"], p["gdn_dt_bias"])
    o, gdn_state = _gdn_mix(q, k, v, gates, z, state_gdn, p["gdn_norm_w"])
    gdn_conv = tail[:, SUBLANES - (CONV_W - 1):, :]
    x = _out_proj(o, x, mod4, 0, boff, p["gdn_w_out"], p["ln_g"][0], p["ln_b"][0])

    prompt = cache_fox_k is None
    q, k32, v32, kb, vb, z, logf = _fox_proj(x, mod4, 1, boff, p["fox_w_in"], p["fox_b_f"], transposed=prompt)
    if prompt:
        kaug, qaugt = _fox_aug(logf)
        o = _fox_attn_prompt(q, kb, kaug, vb, qaugt, z)
    else:
        zero_c = jnp.zeros((b, 1, H_F), F32)
        p_len = cache_fox_k.shape[1]
        cum_pn, cum_pt = _cumsum_time(cache_fox_logf.astype(F32), zero_c)
        cum_n, cum_nt = _cumsum_time(logf, cum_pn[:, p_len - 1:, :])
        o = _fox_attn_sample(q, cache_fox_k.reshape(b, p_len, d), cache_fox_v.reshape(b, p_len, d), kb, vb, z,
                             cum_n, jnp.concatenate([cum_pt, cum_nt], axis=2))
    fox_k = k32.reshape(b, t, H_F, d // H_F)
    fox_v = v32.reshape(b, t, H_F, d // H_F)
    x = _out_proj(o, x, mod4, 1, boff, p["fox_w_out"], p["ln_g"][1], p["ln_b"][1])

    q, k32, v32, kb, vb, z = _diff_proj(x, mod4, 2, boff, p["diff_w_in"], transposed=prompt)
    lam = _lam_pack(p["diff_lam_q1"], p["diff_lam_k1"], p["diff_lam_q2"], p["diff_lam_k2"])
    if prompt:
        o = _diff_attn_prompt(q, kb, vb, z, p["rel_bias_table"], lam, p["diff_subln_w"])
    else:
        p_len = cache_diff_k.shape[1]
        o = _diff_attn_sample(q, cache_diff_k.reshape(b, p_len, d), cache_diff_v.reshape(b, p_len, d), kb, vb, z,
                              p["rel_bias_table"], lam, p["diff_subln_w"])
    diff_k = k32.reshape(b, t, H_D, 2, d // (2 * H_D))
    diff_v = v32.reshape(b, t, H_D, d // H_D)
    x = _out_proj(o, x, mod4, 2, boff, p["diff_w_out"], p["ln_g"][2], p["ln_b"][2])

    dk_r = d // H_R
    cos, sin_signed = _rope_tables(t, start, dk_r)
    q, k, v, z = _ret_proj(x, mod4, 3, boff, p["ret_w_in"], cos, sin_signed)
    if state_ret is None:
        state_ret = jnp.zeros((b, H_R, dk_r, 2 * d // H_R), F32)
    o, ret_state = _ret_mix(q, k, v, z, state_ret, p["ret_gn_w"])
    x = _out_proj(o, x, mod4, 3, boff, p["ret_w_out"], p["ln_g"][3], p["ln_b"][3])

    return x, gdn_state, gdn_conv, fox_k, fox_v, logf, diff_k, diff_v, ret_state


def kernel(x_prompt, x_sample, c_prompt, c_sample, state_gdn, state_gdn_conv, cache_fox_k, cache_fox_v, cache_fox_logf, cache_diff_k, cache_diff_v, state_ret, ada_w, ada_b, ln_g, ln_b, gdn_w_in, gdn_conv_w, gdn_a_log, gdn_dt_bias, gdn_norm_w, gdn_w_out, fox_w_in, fox_b_f, fox_w_out, rel_bias_table, diff_w_in, diff_lam_q1, diff_lam_k1, diff_lam_q2, diff_lam_k2, diff_subln_w, diff_w_out, ret_w_in, ret_gn_w, ret_w_out):
    p = dict(ln_g=ln_g, ln_b=ln_b, gdn_w_in=gdn_w_in, gdn_conv_w=gdn_conv_w, gdn_a_log=gdn_a_log,
             gdn_dt_bias=gdn_dt_bias, gdn_norm_w=gdn_norm_w, gdn_w_out=gdn_w_out, fox_w_in=fox_w_in,
             fox_b_f=fox_b_f, fox_w_out=fox_w_out, rel_bias_table=rel_bias_table, diff_w_in=diff_w_in,
             diff_lam_q1=diff_lam_q1, diff_lam_k1=diff_lam_k1, diff_lam_q2=diff_lam_q2, diff_lam_k2=diff_lam_k2,
             diff_subln_w=diff_subln_w, diff_w_out=diff_w_out, ret_w_in=ret_w_in, ret_gn_w=ret_gn_w,
             ret_w_out=ret_w_out)
    bp = x_prompt.shape[0]
    d = x_prompt.shape[-1]
    mod = _modulation(jnp.concatenate([c_prompt, c_sample], axis=0), ada_w, ada_b)
    mod4 = mod.reshape(mod.shape[0], mod.shape[1], 1, 3 * d)
    outs_p = _run_group(x_prompt, mod4, 0, None, None, None, None, None, None, None, None, 0, p)
    outs_s = _run_group(x_sample, mod4, bp, state_gdn, state_gdn_conv, cache_fox_k, cache_fox_v, cache_fox_logf,
                        cache_diff_k, cache_diff_v, state_ret, cache_fox_k.shape[1], p)
    return (outs_p[0], outs_s[0]) + tuple(outs_p[1:]) + tuple(outs_s[1:])
```

```python
import functools
import math

import numpy as np
import jax
import jax.numpy as jnp
from jax import lax
from jax.experimental import pallas as pl
from jax.experimental.pallas import tpu as pltpu

F32 = jnp.float32
BF16 = jnp.bfloat16

DEPTH = 4
ATTN_TQ = 512
ATTN_TK = 256
GDN_CHUNK = 64
ATTN_CHUNK = 64
DEEPNORM_ALPHA = (2.0 * DEPTH) ** 0.25
LN_EPS = 1e-5
NORM_EPS = 1e-6
NEG_INF = -1e30
LOG2E = math.log2(math.e)
H_G, H_F, H_D, H_R = 8, 16, 8, 4
CONV_W = 4
DIFF_LAYER = 2
LAMBDA_INIT = 0.8 - 0.6 * math.exp(-0.3 * DIFF_LAYER)
N_BUCKETS = 32
MAX_DISTANCE = 128
ROPE_BASE = 10000.0

LANES = 128
SUBLANES = 8
VMEM_LIMIT = 56 * 1024 * 1024


def _cparams(*sem):
    return pltpu.CompilerParams(dimension_semantics=sem, vmem_limit_bytes=VMEM_LIMIT)


def _sigmoid(x):
    return 1.0 / (1.0 + jnp.exp(-x))


def _silu(x):
    hx = 0.5 * x
    return hx + hx * jnp.tanh(hx)


def _softplus(x):
    return jnp.maximum(x, 0.0) + jnp.log(1.0 + jnp.exp(-jnp.abs(x)))


def _dot(a, b):
    return jnp.dot(a, b, preferred_element_type=F32)


def _dot_nt(a, b):
    return lax.dot_general(a, b, (((1,), (1,)), ((), ())), preferred_element_type=F32)


def _dot_tn(a, b):
    return lax.dot_general(a, b, (((0,), (0,)), ((), ())), preferred_element_type=F32)


def _split3(x):
    x1 = x.astype(BF16)
    r1 = x - x1.astype(F32)
    x2 = r1.astype(BF16)
    x3 = (r1 - x2.astype(F32)).astype(BF16)
    return x1, x2, x3


def _dot_exact_l(m01, x):
    x1, x2, x3 = _split3(x)
    return _dot(m01, x1) + _dot(m01, x2) + _dot(m01, x3)


def _dot_exact_nt(m01, x):
    x1, x2, x3 = _split3(x)
    return _dot_nt(m01, x1) + _dot_nt(m01, x2) + _dot_nt(m01, x3)


def _iota(shape, dim):
    return lax.broadcasted_iota(jnp.int32, shape, dim)


def _div_pow2(x, n):
    assert n & (n - 1) == 0
    return jnp.right_shift(x, n.bit_length() - 1)


def _row_tile(t, pref):
    return pref if t % pref == 0 else t


def _mod_kernel(c_ref, w_ref, b_ref, o_ref):
    s = _silu(c_ref[...])
    w = w_ref[...]
    s1 = s.astype(BF16)
    s2 = (s - s1.astype(F32)).astype(BF16)
    w1 = w.astype(BF16)
    w2 = (w - w1.astype(F32)).astype(BF16)
    o_ref[...] = _dot(s1, w1) + _dot(s1, w2) + _dot(s2, w1) + b_ref[...]


def _modulation(c_all, ada_w, ada_b):
    nb, d = c_all.shape
    depth, _, n = ada_w.shape
    tn = 1024
    return pl.pallas_call(
        _mod_kernel,
        grid=(depth, n // tn),
        in_specs=[pl.BlockSpec((nb, d), lambda l, j: (0, 0)),
                  pl.BlockSpec((None, d, tn), lambda l, j: (l, 0, j)),
                  pl.BlockSpec((None, 1, tn), lambda l, j: (l, 0, j))],
        out_specs=pl.BlockSpec((None, nb, tn), lambda l, j: (l, 0, j)),
        out_shape=jax.ShapeDtypeStruct((depth, nb, n), F32),
        compiler_params=_cparams("arbitrary", "arbitrary"),
        name="adaln_modulation",
    )(c_all, ada_w, ada_b.reshape(depth, 1, n))


def _mod_specs(layer, boff, d, which):
    return [pl.BlockSpec((None, None, 1, d), lambda b, i, w=w: (layer, boff + b, 0, w)) for w in which]


def _modulated(x_ref, shift_ref, scale_ref):
    return (x_ref[...] * (1.0 + scale_ref[...]) + shift_ref[...]).astype(BF16)


def _const_spec(shape):
    return pl.BlockSpec(shape, lambda b, i: (0,) * len(shape))


def _rows_spec(tm, n):
    return pl.BlockSpec((None, tm, n), lambda b, i: (b, i, 0))


def _gdn_proj_kernel(x_ref, shift_ref, scale_ref, wqkv_ref, wba_ref, wz_ref, cbuf_ref, cw_ref, avec_ref, dtvec_ref,
                     q_ref, k_ref, v_ref, gates_ref, z_ref, tail_ref, ext_ref, *, chunk):
    i = pl.program_id(1)
    tm, d = x_ref.shape
    dk = d // H_G
    u = _modulated(x_ref, shift_ref, scale_ref)

    @pl.when(i == 0)
    def _():
        ext_ref[0:SUBLANES, :] = cbuf_ref[...]

    for s in range(3):
        ext_ref[SUBLANES:SUBLANES + tm, s * d:(s + 1) * d] = _dot(u, wqkv_ref[:, s * d:(s + 1) * d])
    outs = (q_ref, k_ref, v_ref)
    for s in range(3):
        for h in range(H_G):
            c0 = s * d + h * dk
            e = ext_ref[:, c0:c0 + dk]
            acc = cw_ref[0:1, c0:c0 + dk] * e
            for j in range(1, CONV_W):
                acc = pltpu.roll(acc, 1, 0) + cw_ref[j:j + 1, c0:c0 + dk] * e
            hy = acc[SUBLANES:, :]
            y = hy + hy * jnp.tanh(hy)
            if s < 2:
                inv = lax.rsqrt(jnp.sum(y * y, axis=-1, keepdims=True) + NORM_EPS)
                y = y * (inv * (dk ** -0.5) if s == 0 else inv)
            outs[s][:, h * dk:(h + 1) * dk] = y.astype(BF16)
    tail = ext_ref[tm:tm + SUBLANES, :]
    tail_ref[...] = tail
    ext_ref[0:SUBLANES, :] = tail

    ba = _dot(u, wba_ref[...])
    g = -jnp.exp(avec_ref[...]) * _softplus(ba + dtvec_ref[...])
    r = _iota((tm, tm), 0)
    c = _iota((tm, tm), 1)
    tri = jnp.where(_div_pow2(r, chunk) == _div_pow2(c, chunk), jnp.where(r >= c, 1.0, 0.0), 0.0).astype(BF16)
    gcum = _dot_exact_l(tri, g)
    gates_ref[...] = jnp.where(_iota((tm, LANES), 1) < H_G, _sigmoid(ba), gcum)
    z_ref[...] = _dot(u, wz_ref[...]).astype(BF16)


def _gdn_proj(x, mod4, layer, boff, w_in, conv_buf, conv_w, a_log, dt_bias):
    b, t, d = x.shape
    tm = _row_tile(t, 256)
    chunk = min(GDN_CHUNK, t)
    assert t >= CONV_W - 1 and tm >= SUBLANES and tm % chunk == 0
    wqkv = w_in[:, :3 * d].astype(BF16)
    wba = jnp.pad(w_in[:, 3 * d:3 * d + 2 * H_G], ((0, 0), (0, LANES - 2 * H_G))).astype(BF16)
    wz = w_in[:, 3 * d + 2 * H_G:].astype(BF16)
    cbuf = jnp.pad(conv_buf.astype(F32), ((0, 0), (SUBLANES - (CONV_W - 1), 0), (0, 0)))
    cw = jnp.pad(0.5 * conv_w.astype(F32), ((0, SUBLANES - CONV_W), (0, 0)))
    avec = jnp.pad(a_log.astype(F32), (H_G, LANES - 2 * H_G)).reshape(1, LANES)
    dtvec = jnp.pad(dt_bias.astype(F32), (H_G, LANES - 2 * H_G)).reshape(1, LANES)
    bf16o = jax.ShapeDtypeStruct((b, t, d), BF16)
    tail_spec = pl.BlockSpec((None, SUBLANES, 3 * d), lambda bb, i: (bb, 0, 0))
    return pl.pallas_call(
        functools.partial(_gdn_proj_kernel, chunk=chunk),
        grid=(b, t // tm),
        in_specs=[_rows_spec(tm, d)] + _mod_specs(layer, boff, d, (0, 1))
                 + [_const_spec((d, 3 * d)), _const_spec((d, LANES)), _const_spec((d, d)), tail_spec,
                    _const_spec((SUBLANES, 3 * d)), _const_spec((1, LANES)), _const_spec((1, LANES))],
        out_specs=[_rows_spec(tm, d)] * 3 + [_rows_spec(tm, LANES), _rows_spec(tm, d), tail_spec],
        out_shape=[bf16o, bf16o, bf16o, jax.ShapeDtypeStruct((b, t, LANES), F32), bf16o,
                   jax.ShapeDtypeStruct((b, SUBLANES, 3 * d), F32)],
        scratch_shapes=[pltpu.VMEM((tm + SUBLANES, 3 * d), F32)],
        compiler_params=_cparams("arbitrary", "arbitrary"),
        name="gdn_in_proj",
    )(x, mod4, mod4, wqkv, wba, wz, cbuf, cw, avec, dtvec)


def _unit_lower_inverse_minus_identity(mats):
    n = mats[0].shape[0]
    r = _iota((n, n), 0)
    c = _iota((n, n), 1)

    def mm(xs, ys):
        return [_dot(x.astype(BF16), y.astype(BF16)) for x, y in zip(xs, ys)]

    base = 8
    diag = _div_pow2(r, base) == _div_pow2(c, base)
    d = [jnp.where(diag, a, 0.0) for a in mats]
    d2 = mm(d, d)
    d4 = mm(d2, d2)
    nn = [-x for x in d]
    nn = [x + y + z for x, y, z in zip(nn, d2, mm(nn, d2))]
    nn = [x + y + z for x, y, z in zip(nn, d4, mm(nn, d4))]
    m = base
    while m < n:
        pair = (_div_pow2(r, 2 * m) == _div_pow2(c, 2 * m)) & (_div_pow2(r, m) != _div_pow2(c, m))
        off = [jnp.where(pair, a, 0.0) for a in mats]
        y = [o + p for o, p in zip(off, mm(nn, off))]
        x = [p + q for p, q in zip(y, mm(y, nn))]
        nn = [p - q for p, q in zip(nn, x)]
        m *= 2
    return nn


def _gdn_kernel(q_ref, k_ref, v_ref, gates_ref, z_ref, s0_ref, nw_ref, o_ref, s_ref, *, chunk):
    ti = pl.program_id(1)
    grp, tb, d = z_ref.shape
    dk = d // H_G
    n_chunks = tb // chunk

    @pl.when(ti == 0)
    def _():
        s_ref[...] = s0_ref[...]

    ri = _iota((chunk, chunk), 0)
    ci = _iota((chunk, chunk), 1)
    eye_l = (_iota((LANES, LANES), 0) == _iota((LANES, LANES), 1)).astype(BF16)
    incl = ri >= ci
    strict = ri > ci
    nw = nw_ref[...]
    items = [(g, h) for g in range(grp) for h in range(H_G)]

    def chunk_body(cidx, carry):
        r0 = pl.multiple_of(cidx * chunk, chunk)
        rows = pl.ds(r0, chunk)
        gates = [gates_ref[g, rows, :] for g in range(grp)]
        gates_t = [_dot_exact_nt(eye_l, x) for x in gates]

        kbf = [k_ref[g, rows, h * dk:(h + 1) * dk] for g, h in items]
        qbf = [q_ref[g, rows, h * dk:(h + 1) * dk] for g, h in items]
        q = [x.astype(F32) for x in qbf]
        k = [x.astype(F32) for x in kbf]
        v = [v_ref[g, rows, h * dk:(h + 1) * dk].astype(F32) for g, h in items]
        beta = [gates[g][:, h:h + 1] for g, h in items]
        gcol = [gates[g][:, H_G + h:H_G + h + 1] for g, h in items]
        grow = [gates_t[g][H_G + h:H_G + h + 1, :] for g, h in items]
        dec_incl = [jnp.exp(jnp.where(incl, gc - gr, NEG_INF)) for gc, gr in zip(gcol, grow)]
        kb = [x * bt for x, bt in zip(k, beta)]
        a_mat = [_dot_nt(x.astype(BF16), y) for x, y in zip(kb, kbf)]
        qk = [_dot_nt(x, y) for x, y in zip(qbf, kbf)]
        a_mat = [jnp.where(strict, x * e, 0.0) for x, e in zip(a_mat, dec_incl)]
        qk = [x * e for x, e in zip(qk, dec_incl)]
        exp_g = [jnp.exp(gc) for gc in gcol]
        rhs = [jnp.concatenate([x * bt, y * e], axis=1) for x, bt, y, e in zip(v, beta, kb, exp_g)]
        nn = _unit_lower_inverse_minus_identity(a_mat)
        sol = [x + _dot(y.astype(BF16), x.astype(BF16)) for x, y in zip(rhs, nn)]
        s = [s_ref[g, h] for g, h in items]
        sb = [x.astype(BF16) for x in s]
        v_res = [x[:, :dk] - _dot(x[:, dk:].astype(BF16), y) for x, y in zip(sol, sb)]
        vrb = [x.astype(BF16) for x in v_res]
        o = [_dot((x * e).astype(BF16), y) for x, e, y in zip(q, exp_g, sb)]
        o = [x + _dot(y.astype(BF16), z) for x, y, z in zip(o, qk, vrb)]
        g_last = [gc[chunk - 1:chunk, :] for gc in gcol]
        k_dec = [(x * jnp.exp(gl - gc)).astype(BF16) for x, gl, gc in zip(k, g_last, gcol)]
        s_add = [_dot_tn(x, y) for x, y in zip(k_dec, vrb)]
        for i, (g, h) in enumerate(items):
            s_ref[g, h] = s[i] * jnp.exp(g_last[i]) + s_add[i]
            on = o[i] * lax.rsqrt(jnp.mean(o[i] * o[i], axis=-1, keepdims=True) + NORM_EPS) * nw
            zz = z_ref[g, rows, h * dk:(h + 1) * dk].astype(F32)
            o_ref[g, rows, h * dk:(h + 1) * dk] = (on * _silu(zz)).astype(BF16)
        return carry

    lax.fori_loop(0, n_chunks, chunk_body, 0)


def _gdn_mix(q, k, v, gates, z, s0, norm_w):
    b, t, d = q.shape
    dk = d // H_G
    chunk = min(GDN_CHUNK, t)
    tb = _row_tile(t, 4 * chunk)
    grp = 4 if b % 4 == 0 else (2 if b % 2 == 0 else 1)
    nw = norm_w.astype(F32).reshape(1, dk)

    def rows(n):
        return pl.BlockSpec((grp, tb, n), lambda bb, i: (bb, i, 0))

    state_spec = pl.BlockSpec((grp, H_G, dk, dk), lambda bb, i: (bb, 0, 0, 0))
    return pl.pallas_call(
        functools.partial(_gdn_kernel, chunk=chunk),
        grid=(b // grp, t // tb),
        in_specs=[rows(d)] * 3 + [rows(LANES), rows(d), state_spec, _const_spec((1, dk))],
        out_specs=[rows(d), state_spec],
        out_shape=[jax.ShapeDtypeStruct((b, t, d), BF16), jax.ShapeDtypeStruct((b, H_G, dk, dk), F32)],
        compiler_params=_cparams("arbitrary", "arbitrary"),
        name="gdn_mixer",
    )(q, k, v, gates, z, s0.astype(F32), nw)


def _out_proj_kernel(o_ref, x_ref, gate_ref, w_ref, g_ref, b_ref, y_ref):
    h = _dot(o_ref[...], w_ref[...])
    y = DEEPNORM_ALPHA * x_ref[...] + (1.0 + gate_ref[...]) * h
    mu = jnp.mean(y, axis=-1, keepdims=True)
    yc = y - mu
    var = jnp.mean(yc * yc, axis=-1, keepdims=True)
    y_ref[...] = yc * lax.rsqrt(var + LN_EPS) * g_ref[...] + b_ref[...]


def _out_proj(o, x, mod4, layer, boff, w_out, ln_g, ln_b):
    b, t, d = x.shape
    kdim = o.shape[-1]
    tm = _row_tile(t, 1024)
    return pl.pallas_call(
        _out_proj_kernel,
        grid=(b, t // tm),
        in_specs=[_rows_spec(tm, kdim), _rows_spec(tm, d)] + _mod_specs(layer, boff, d, (2,))
                 + [_const_spec((kdim, d)), _const_spec((1, d)), _const_spec((1, d))],
        out_specs=_rows_spec(tm, d),
        out_shape=jax.ShapeDtypeStruct((b, t, d), F32),
        compiler_params=_cparams("arbitrary", "arbitrary"),
        name="out_proj_postnorm",
    )(o, x, mod4, w_out.astype(BF16), ln_g.reshape(1, d), ln_b.reshape(1, d))


def _qkvz_outputs(u, w_ref, q_ref, k32_ref, v32_ref, kb_ref, vb_ref, z_ref, q_scale, transposed):
    d = u.shape[-1]
    if transposed:
        q_ref[...] = (_dot_nt(w_ref[:, 0:d], u) * q_scale).astype(BF16)
    else:
        q_ref[...] = (_dot(u, w_ref[:, 0:d]) * q_scale).astype(BF16)
    k = _dot(u, w_ref[:, d:2 * d])
    k32_ref[...] = k
    kb_ref[...] = k.astype(BF16)
    v = _dot(u, w_ref[:, 2 * d:3 * d])
    v32_ref[...] = v
    vb_ref[...] = (v.T if transposed else v).astype(BF16)
    z_ref[...] = _dot(u, w_ref[:, 3 * d:4 * d]).astype(BF16)


def _qkvz_weights(w_in, d, transposed):
    w = w_in[:, :4 * d]
    if transposed:
        w = jnp.concatenate([w[:, :d].T, w[:, d:]], axis=1)
    return w.astype(BF16)


def _qkvz_specs(b, t, d, tm, transposed):
    cols_spec = pl.BlockSpec((None, d, tm), lambda bb, i: (bb, 0, i))
    rows = _rows_spec(tm, d)
    f32o = jax.ShapeDtypeStruct((b, t, d), F32)
    bf16o = jax.ShapeDtypeStruct((b, t, d), BF16)
    bf16t = jax.ShapeDtypeStruct((b, d, t), BF16)
    if transposed:
        return [cols_spec, rows, rows, rows, cols_spec, rows], [bf16t, f32o, f32o, bf16o, bf16t, bf16o]
    return [rows] * 6, [bf16o, f32o, f32o, bf16o, bf16o, bf16o]


def _fox_proj_kernel(x_ref, shift_ref, scale_ref, w_ref, wf_ref, bf_ref,
                     q_ref, k32_ref, v32_ref, kb_ref, vb_ref, z_ref, logf_ref, *, q_scale, transposed):
    u = _modulated(x_ref, shift_ref, scale_ref)
    _qkvz_outputs(u, w_ref, q_ref, k32_ref, v32_ref, kb_ref, vb_ref, z_ref, q_scale, transposed)
    f = _dot(u, wf_ref[...])[:, :H_F] + bf_ref[...]
    logf_ref[...] = -_softplus(-f)


def _fox_proj(x, mod4, layer, boff, w_in, b_f, transposed):
    b, t, d = x.shape
    tm = _row_tile(t, 512)
    w = _qkvz_weights(w_in, d, transposed)
    wf = jnp.pad(w_in[:, 4 * d:], ((0, 0), (0, LANES - H_F))).astype(BF16)
    out_specs, out_shape = _qkvz_specs(b, t, d, tm, transposed)
    return pl.pallas_call(
        functools.partial(_fox_proj_kernel, q_scale=(d // H_F) ** -0.5 * (LOG2E if transposed else 1.0),
                          transposed=transposed),
        grid=(b, t // tm),
        in_specs=[_rows_spec(tm, d)] + _mod_specs(layer, boff, d, (0, 1))
                 + [_const_spec((d, 4 * d)), _const_spec((d, LANES)), _const_spec((1, H_F))],
        out_specs=out_specs + [_rows_spec(tm, H_F)],
        out_shape=out_shape + [jax.ShapeDtypeStruct((b, t, H_F), F32)],
        compiler_params=_cparams("arbitrary", "arbitrary"),
        name="fox_in_proj",
    )(x, mod4, mod4, w, wf, b_f.astype(F32).reshape(1, H_F))


def _cumsum_kernel(x_ref, c0_ref, cn_ref, ct_ref, *, blk):
    s, h = x_ref.shape
    tri = (_iota((blk, blk), 0) >= _iota((blk, blk), 1)).astype(BF16)
    eye_h = (_iota((h, h), 0) == _iota((h, h), 1)).astype(BF16)
    carry = c0_ref[...]
    for i in range(s // blk):
        c = _dot_exact_l(tri, x_ref[i * blk:(i + 1) * blk, :]) + carry
        cn_ref[i * blk:(i + 1) * blk, :] = c
        ct_ref[:, i * blk:(i + 1) * blk] = _dot_exact_nt(eye_h, c)
        carry = c[blk - 1:blk, :]


def _cumsum_time(x, c0):
    b, s, h = x.shape
    blk = 256 if s % 256 == 0 else s
    return pl.pallas_call(
        functools.partial(_cumsum_kernel, blk=blk),
        grid=(b,),
        in_specs=[pl.BlockSpec((None, s, h), lambda bb: (bb, 0, 0)),
                  pl.BlockSpec((None, 1, h), lambda bb: (bb, 0, 0))],
        out_specs=[pl.BlockSpec((None, s, h), lambda bb: (bb, 0, 0)),
                   pl.BlockSpec((None, h, s), lambda bb: (bb, 0, 0))],
        out_shape=[jax.ShapeDtypeStruct((b, s, h), F32), jax.ShapeDtypeStruct((b, h, s), F32)],
        compiler_params=_cparams("arbitrary"),
        name="logf_cumsum",
    )(x, c0)


AUG = LANES // H_F


def _fox_aug_kernel(x_ref, kaug_ref, qaugt_ref, *, blk):
    s, h = x_ref.shape
    tri = (_iota((blk, blk), 0) >= _iota((blk, blk), 1)).astype(BF16)
    lane_h = _iota((h, LANES), 1)
    row_h = _iota((h, LANES), 0)
    ek = [jnp.where(lane_h == row_h * AUG + part, -1.0, 0.0).astype(BF16) for part in range(3)]
    row_q = _iota((LANES, h), 0)
    col_q = _iota((LANES, h), 1)
    eq = [jnp.where(row_q == col_q * AUG + 3 + part, 1.0, 0.0).astype(BF16) for part in range(3)]
    k_slot = _iota((blk, LANES), 1) & (AUG - 1)
    k_ones = jnp.where(k_slot >= 3, jnp.where(k_slot < 6, 1.0, 0.0), 0.0)
    q_ones = jnp.where((_iota((LANES, blk), 0) & (AUG - 1)) < 3, 1.0, 0.0)
    carry = jnp.zeros((1, h), F32)
    for i in range(s // blk):
        c = _dot_exact_l(tri, x_ref[i * blk:(i + 1) * blk, :]) + carry
        parts = _split3(c * LOG2E)
        kaug = k_ones
        qaugt = q_ones
        for part in range(3):
            kaug = kaug + _dot(parts[part], ek[part])
            qaugt = qaugt + _dot_nt(eq[part], parts[part])
        kaug_ref[i * blk:(i + 1) * blk, :] = kaug.astype(BF16)
        qaugt_ref[:, i * blk:(i + 1) * blk] = qaugt.astype(BF16)
        carry = c[blk - 1:blk, :]


def _fox_aug(logf):
    b, s, h = logf.shape
    assert h * AUG == LANES and AUG >= 6
    blk = 256 if s % 256 == 0 else s
    return pl.pallas_call(
        functools.partial(_fox_aug_kernel, blk=blk),
        grid=(b,),
        in_specs=[pl.BlockSpec((None, s, h), lambda bb: (bb, 0, 0))],
        out_specs=[pl.BlockSpec((None, s, LANES), lambda bb: (bb, 0, 0)),
                   pl.BlockSpec((None, LANES, s), lambda bb: (bb, 0, 0))],
        out_shape=[jax.ShapeDtypeStruct((b, s, LANES), BF16), jax.ShapeDtypeStruct((b, LANES, s), BF16)],
        compiler_params=_cparams("arbitrary"),
        name="fox_bias_operands",
    )(logf)


SUM_ROWS = 16


def _softmax_t_probs(s, m_ref):
    m_prev = m_ref[...]
    m_new = jnp.maximum(m_prev, jnp.max(s, axis=0, keepdims=True))
    m_ref[...] = m_new
    return jnp.exp2(s - m_new).astype(BF16), jnp.exp2(m_prev - m_new)


def _softmax_t_accumulate(vt, p, alpha, acc_ref):
    vt_ext = jnp.concatenate([vt, jnp.ones((SUM_ROWS, vt.shape[1]), BF16)], axis=0)
    acc_ref[...] = alpha * acc_ref[...] + _dot(vt_ext, p)


def _softmax_t_result(acc_ref, dv):
    return acc_ref[0:dv, :] / acc_ref[dv:dv + 1, :]


def _fox_attn_kernel(qt_ref, k_ref, kaug_ref, vt_ref, qaugt_ref, z_ref, o_ref,
                     qcat_ref, s_ref, m_ref, acc_ref, *, tk):
    hp = pl.program_id(1)
    qi = pl.program_id(2)
    tq = qt_ref.shape[1]
    assert tq == 2 * tk
    dh = LANES // 2
    row = _iota((LANES, tq), 0)
    qa = qaugt_ref[...].astype(F32)
    for hh in range(2):
        qcat_ref[hh, hh * dh:(hh + 1) * dh, :] = qt_ref[hh * dh:(hh + 1) * dh, :]
        qcat_ref[hh, (1 - hh) * dh:(2 - hh) * dh, :] = jnp.zeros((dh, tq), BF16)
        qcat_ref[hh, LANES:2 * LANES, :] = jnp.where(_div_pow2(row, AUG) == hp * 2 + hh, qa, 0.0).astype(BF16)
    m_ref[...] = jnp.full(m_ref.shape, NEG_INF, F32)
    acc_ref[...] = jnp.zeros(acc_ref.shape, F32)
    n_full = qi * 2
    causal = _iota((tk, tk), 0) <= _iota((tk, tk), 1)

    def scores(j, slot):
        k0 = pl.multiple_of(j * tk, tk)
        kcat = jnp.concatenate([k_ref[pl.ds(k0, tk), :], kaug_ref[pl.ds(k0, tk), :]], axis=1)
        for hh in range(2):
            s_ref[slot, hh] = _dot(kcat, qcat_ref[hh])

    def consume(j, slot, diagonal=(False, False), q0=0):
        k0 = pl.multiple_of(j * tk, tk)
        for hh in range(2):
            for half in range(q0 // tk, 2):
                s = s_ref[slot, hh, :, half * tk:(half + 1) * tk]
                if diagonal[half]:
                    s = jnp.where(causal, s, NEG_INF)
                p, alpha = _softmax_t_probs(s, m_ref.at[hh, half])
                _softmax_t_accumulate(vt_ref[hh * dh:(hh + 1) * dh, pl.ds(k0, tk)], p, alpha, acc_ref.at[hh, half])

    def pair(cur, a):
        nxt = 1 - cur
        scores(a + 2, 2 * nxt)
        consume(a, 2 * cur)
        scores(a + 3, 2 * nxt + 1)
        consume(a + 1, 2 * cur + 1)

    def last_pair(cur):
        consume(n_full, 2 * cur, diagonal=(True, False))
        consume(n_full + 1, 2 * cur + 1, diagonal=(False, True), q0=tk)

    scores(0, 0)
    scores(1, 1)

    def body(i, carry):
        pair(0, 4 * i)
        pair(1, 4 * i + 2)
        return carry

    lax.fori_loop(0, lax.shift_right_logical(qi, 1), body, 0)

    @pl.when((qi & 1) == 1)
    def _():
        pair(0, n_full - 2)
        last_pair(1)

    @pl.when((qi & 1) == 0)
    def _():
        last_pair(0)
    o_t = jnp.concatenate(
        [jnp.concatenate([_softmax_t_result(acc_ref.at[hh, half], dh) for half in range(2)], axis=1)
         for hh in range(2)], axis=0)
    o_ref[...] = (o_t.T * _silu(z_ref[...].astype(F32))).astype(BF16)


def _fox_attn_prompt(qt, kb, kaug, vt, qaugt, z):
    b, d, t = qt.shape
    tq, tk = ATTN_TQ, ATTN_TK
    assert t % tq == 0
    hpairs = d // LANES
    dh = LANES // 2
    return pl.pallas_call(
        functools.partial(_fox_attn_kernel, tk=tk),
        grid=(b, hpairs, t // tq),
        in_specs=[pl.BlockSpec((None, LANES, tq), lambda bb, hp, i: (bb, hp, i)),
                  pl.BlockSpec((None, t, LANES), lambda bb, hp, i: (bb, 0, hp)),
                  pl.BlockSpec((None, t, LANES), lambda bb, hp, i: (bb, 0, 0)),
                  pl.BlockSpec((None, LANES, t), lambda bb, hp, i: (bb, hp, 0)),
                  pl.BlockSpec((None, LANES, tq), lambda bb, hp, i: (bb, 0, i)),
                  pl.BlockSpec((None, tq, LANES), lambda bb, hp, i: (bb, i, hp))],
        out_specs=pl.BlockSpec((None, tq, LANES), lambda bb, hp, i: (bb, i, hp)),
        out_shape=jax.ShapeDtypeStruct((b, t, d), BF16),
        scratch_shapes=[pltpu.VMEM((2, 2 * LANES, tq), BF16), pltpu.VMEM((4, 2, tk, tq), F32),
                        pltpu.VMEM((2, 2, 1, tk), F32), pltpu.VMEM((2, 2, dh + SUM_ROWS, tk), F32)],
        compiler_params=_cparams("arbitrary", "arbitrary", "arbitrary"),
        name="fox_attention_prompt",
    )(qt, kb, kaug, vt, qaugt, z)


def _fox_decode_kernel(q_ref, kp_ref, vp_ref, kn_ref, vn_ref, z_ref, cq_ref, ckt_ref, o_ref):
    hp = pl.program_id(1)
    t = q_ref.shape[0]
    p_len = kp_ref.shape[0]
    dh = LANES // 2
    lane = _iota((t, LANES), 1)
    q = q_ref[...]
    kp = kp_ref[...].astype(BF16)
    vp = vp_ref[...].astype(BF16)
    kn = kn_ref[...]
    vn = vn_ref[...]
    cq_all = cq_ref[...]
    hlane = _iota(cq_all.shape, 1)
    row2 = _iota((2 * t, t), 0)
    causal = _iota((2 * t, t), 1) <= jnp.where(row2 >= t, row2 - t, row2)
    qm, gate = [], []
    for hh in range(2):
        h = hp * 2 + hh
        qm.append(jnp.where(_div_pow2(lane, dh) == hh, q, jnp.zeros_like(q)))
        cq = jnp.sum(jnp.where(hlane == h, cq_all, 0.0), axis=-1, keepdims=True)
        gate.append(cq - ckt_ref[pl.ds(h, 1), :])
    qm = jnp.concatenate(qm, axis=0)
    gate = jnp.concatenate(gate, axis=0)
    s_p = _dot_nt(qm, kp) + gate[:, :p_len]
    s_n = jnp.where(causal, _dot_nt(qm, kn) + gate[:, p_len:], NEG_INF)
    m = jnp.maximum(jnp.max(s_p, axis=-1, keepdims=True), jnp.max(s_n, axis=-1, keepdims=True))
    e_p = jnp.exp(s_p - m)
    e_n = jnp.exp(s_n - m)
    den = jnp.sum(e_p, axis=-1, keepdims=True) + jnp.sum(e_n, axis=-1, keepdims=True)
    o2 = (_dot(e_p.astype(BF16), vp) + _dot(e_n.astype(BF16), vn)) / den
    o = jnp.where(lane < dh, o2[:t], o2[t:])
    o_ref[...] = (o * _silu(z_ref[...].astype(F32))).astype(BF16)


def _fox_attn_sample(q, k_past, v_past, kb, vb, z, cq_new, cum_t):
    b, t, d = q.shape
    p_len = k_past.shape[1]
    hpairs = d // LANES
    new_spec = pl.BlockSpec((None, t, LANES), lambda bb, hp: (bb, 0, hp))
    past_spec = pl.BlockSpec((None, p_len, LANES), lambda bb, hp: (bb, 0, hp))
    return pl.pallas_call(
        _fox_decode_kernel,
        grid=(b, hpairs),
        in_specs=[new_spec, past_spec, past_spec, new_spec, new_spec, new_spec,
                  pl.BlockSpec((None, t, H_F), lambda bb, hp: (bb, 0, 0)),
                  pl.BlockSpec((None, H_F, p_len + t), lambda bb, hp: (bb, 0, 0))],
        out_specs=new_spec,
        out_shape=jax.ShapeDtypeStruct((b, t, d), BF16),
        compiler_params=_cparams("arbitrary", "arbitrary"),
        name="fox_attention_sample",
    )(q, k_past, v_past, kb, vb, z, cq_new, cum_t)


def _diff_proj_kernel(x_ref, shift_ref, scale_ref, w_ref, q_ref, k32_ref, v32_ref, kb_ref, vb_ref, z_ref,
                      *, q_scale, transposed):
    u = _modulated(x_ref, shift_ref, scale_ref)
    _qkvz_outputs(u, w_ref, q_ref, k32_ref, v32_ref, kb_ref, vb_ref, z_ref, q_scale, transposed)


def _diff_proj(x, mod4, layer, boff, w_in, transposed):
    b, t, d = x.shape
    tm = _row_tile(t, 512)
    out_specs, out_shape = _qkvz_specs(b, t, d, tm, transposed)
    return pl.pallas_call(
        functools.partial(_diff_proj_kernel, q_scale=(d // (2 * H_D)) ** -0.5 * (LOG2E if transposed else 1.0),
                          transposed=transposed),
        grid=(b, t // tm),
        in_specs=[_rows_spec(tm, d)] + _mod_specs(layer, boff, d, (0, 1)) + [_const_spec((d, 4 * d))],
        out_specs=out_specs,
        out_shape=out_shape,
        compiler_params=_cparams("arbitrary", "arbitrary"),
        name="diff_in_proj",
    )(x, mod4, mod4, _qkvz_weights(w_in, d, transposed))


def _t5_thresholds():
    nb = N_BUCKETS // 2
    max_exact = nb // 2
    steps = nb - max_exact
    ratio = MAX_DISTANCE // max_exact
    out = []
    for kk in range(1, nb - max_exact):
        target = max_exact ** steps * ratio ** kk
        n = max_exact
        while n ** steps < target:
            n += 1
        out.append(n)
    return nb, max_exact, out


def _bias_kernel(tbl_ref, o_ref, *, q0, k0, keys_on_rows):
    h = pl.program_id(0)
    shape = o_ref.shape
    kdim, qdim = (0, 1) if keys_on_rows else (1, 0)
    rel = (k0 + _iota(shape, kdim)) - (q0 + _iota(shape, qdim))
    nb, max_exact, thr = _t5_thresholds()
    n = jnp.abs(rel)
    large = jnp.full(shape, max_exact, jnp.int32)
    for tval in thr:
        large = large + (n >= tval).astype(jnp.int32)
    bucket = jnp.where(rel > 0, nb, 0) + jnp.where(n < max_exact, n, large)
    acc = jnp.zeros(shape, F32)
    for bkt in range(N_BUCKETS):
        acc = jnp.where(bucket == bkt, tbl_ref[bkt * H_D + h], acc)
    if keys_on_rows:
        acc = (acc - tbl_ref[(nb - 1) * H_D + h]) * LOG2E
    o_ref[...] = acc


def _bias_tile(rel_table, q0, nq, k0, nk, keys_on_rows=False):
    shape = (nk, nq) if keys_on_rows else (nq, nk)
    return pl.pallas_call(
        functools.partial(_bias_kernel, q0=q0, k0=k0, keys_on_rows=keys_on_rows),
        grid=(H_D,),
        in_specs=[pl.BlockSpec(memory_space=pltpu.SMEM)],
        out_specs=pl.BlockSpec((None,) + shape, lambda h: (h, 0, 0)),
        out_shape=jax.ShapeDtypeStruct((H_D,) + shape, F32),
        compiler_params=_cparams("arbitrary"),
        name="t5_bias_tile",
    )(rel_table.astype(F32).reshape(N_BUCKETS * H_D))


def _diff_lambda(lam_ref):
    lam = lam_ref[...]
    s1 = jnp.sum(lam[0:1, :] * lam[1:2, :], axis=-1, keepdims=True)
    s2 = jnp.sum(lam[2:3, :] * lam[3:4, :], axis=-1, keepdims=True)
    return jnp.exp(s1) - jnp.exp(s2) + LAMBDA_INIT


def _diff_epilogue(o, z, subln_ref):
    on = o * lax.rsqrt(jnp.mean(o * o, axis=-1, keepdims=True) + NORM_EPS) * subln_ref[...]
    on = on * (1.0 - LAMBDA_INIT)
    return (on * _silu(z.astype(F32))).astype(BF16)


def _diff_attn_kernel(tbl_ref, qt_ref, k_ref, vt_ref, z_ref, biasm_ref, bias0_ref, bias1_ref, lam_ref, subln_ref,
                      o_ref, qcat_ref, s_ref, m_ref, acc_ref, *, tk):
    h = pl.program_id(1)
    qi = pl.program_id(2)
    tq = qt_ref.shape[1]
    assert tq == 2 * tk
    dh = LANES // 2
    row = _iota((LANES, tq), 0)
    nb, _, _ = _t5_thresholds()
    far = _split3(jnp.full((LANES, tq), tbl_ref[(nb - 1) * H_D + h], F32) * LOG2E)
    far_rows = jnp.zeros((LANES, tq), F32)
    for part in range(3):
        far_rows = jnp.where(row == part, far[part].astype(F32), far_rows)
    for br in range(2):
        qcat_ref[br, br * dh:(br + 1) * dh, :] = qt_ref[br * dh:(br + 1) * dh, :]
        qcat_ref[br, (1 - br) * dh:(2 - br) * dh, :] = jnp.zeros((dh, tq), BF16)
        qcat_ref[br, LANES:2 * LANES, :] = far_rows.astype(BF16)
    ones_aug = jnp.where(_iota((tk, LANES), 1) < 3, 1.0, 0.0).astype(BF16)
    m_ref[...] = jnp.full(m_ref.shape, NEG_INF, F32)
    acc_ref[...] = jnp.zeros(acc_ref.shape, F32)
    block_causal = _div_pow2(_iota((tk, tk), 0), ATTN_CHUNK) <= _div_pow2(_iota((tk, tk), 1), ATTN_CHUNK)

    def scores(j, slot):
        k0 = pl.multiple_of(j * tk, tk)
        kcat = jnp.concatenate([k_ref[pl.ds(k0, tk), :], ones_aug], axis=1)
        for br in range(2):
            s_ref[slot, br] = _dot(kcat, qcat_ref[br])

    def consume(j, slot, bias_ref=None, diagonal=(False, False), q0=0):
        k0 = pl.multiple_of(j * tk, tk)
        for br in range(2):
            for half in range(q0 // tk, 2):
                lanes = slice(half * tk, (half + 1) * tk)
                s = s_ref[slot, br, :, lanes]
                if bias_ref is not None:
                    s = s + bias_ref[:, lanes]
                if diagonal[half]:
                    s = jnp.where(block_causal, s, NEG_INF)
                p, alpha = _softmax_t_probs(s, m_ref.at[br, half])
                _softmax_t_accumulate(vt_ref[:, pl.ds(k0, tk)], p, alpha, acc_ref.at[br, half])

    def pair(cur, a, second_bias_ref=None):
        nxt = 1 - cur
        scores(a + 2, 2 * nxt)
        consume(a, 2 * cur)
        scores(a + 3, 2 * nxt + 1)
        consume(a + 1, 2 * cur + 1, bias_ref=second_bias_ref)

    def last_pair(cur):
        consume(2 * qi, 2 * cur, bias_ref=bias0_ref, diagonal=(True, False))
        consume(2 * qi + 1, 2 * cur + 1, bias_ref=bias1_ref, diagonal=(False, True), q0=tk)

    scores(0, 0)
    scores(1, 1)
    n_plain = jnp.maximum(qi - 1, 0)

    def body(i, carry):
        pair(0, 4 * i)
        pair(1, 4 * i + 2)
        return carry

    lax.fori_loop(0, lax.shift_right_logical(n_plain, 1), body, 0)

    @pl.when(qi == 0)
    def _():
        last_pair(0)

    @pl.when((qi & 1) == 1)
    def _():
        pair(0, 2 * qi - 2, second_bias_ref=biasm_ref)
        last_pair(1)

    @pl.when(jnp.logical_and(qi >= 2, (qi & 1) == 0))
    def _():
        pair(0, 2 * qi - 4)
        pair(1, 2 * qi - 2, second_bias_ref=biasm_ref)
        last_pair(0)
    branch = [jnp.concatenate([_softmax_t_result(acc_ref.at[br, half], LANES) for half in range(2)], axis=1)
              for br in range(2)]
    o_t = branch[0] - _diff_lambda(lam_ref) * branch[1]
    o_ref[...] = _diff_epilogue(o_t.T, z_ref[...], subln_ref)


def _lam_pack(lam_q1, lam_k1, lam_q2, lam_k2):
    rows = jnp.stack([lam_q1, lam_k1, lam_q2, lam_k2]).astype(F32)
    return jnp.pad(rows, ((0, SUBLANES - 4), (0, LANES - rows.shape[1])))


def _diff_attn_prompt(qt, kb, vt, z, rel_table, lam, subln_w):
    b, d, t = qt.shape
    tq, tk = ATTN_TQ, ATTN_TK
    assert t % tq == 0 and tk % ATTN_CHUNK == 0 and tk >= MAX_DISTANCE
    biasm = _bias_tile(rel_table, tk, tq, 0, tk, keys_on_rows=True)
    bias0 = _bias_tile(rel_table, 0, tq, 0, tk, keys_on_rows=True)
    bias1 = _bias_tile(rel_table, 0, tq, tk, tk, keys_on_rows=True)
    rows_spec = pl.BlockSpec((None, tq, LANES), lambda bb, h, i: (bb, i, h))
    bias_spec = pl.BlockSpec((None, tk, tq), lambda bb, h, i: (h, 0, 0))
    return pl.pallas_call(
        functools.partial(_diff_attn_kernel, tk=tk),
        grid=(b, H_D, t // tq),
        in_specs=[pl.BlockSpec(memory_space=pltpu.SMEM),
                  pl.BlockSpec((None, LANES, tq), lambda bb, h, i: (bb, h, i)),
                  pl.BlockSpec((None, t, LANES), lambda bb, h, i: (bb, 0, h)),
                  pl.BlockSpec((None, LANES, t), lambda bb, h, i: (bb, h, 0)),
                  rows_spec, bias_spec, bias_spec, bias_spec,
                  pl.BlockSpec((SUBLANES, LANES), lambda bb, h, i: (0, 0)),
                  pl.BlockSpec((1, LANES), lambda bb, h, i: (0, 0))],
        out_specs=rows_spec,
        out_shape=jax.ShapeDtypeStruct((b, t, d), BF16),
        scratch_shapes=[pltpu.VMEM((2, 2 * LANES, tq), BF16), pltpu.VMEM((4, 2, tk, tq), F32),
                        pltpu.VMEM((2, 2, 1, tk), F32), pltpu.VMEM((2, 2, LANES + SUM_ROWS, tk), F32)],
        compiler_params=_cparams("arbitrary", "arbitrary", "arbitrary"),
        name="diff_attention_prompt",
    )(rel_table.astype(F32).reshape(N_BUCKETS * H_D), qt, kb, vt, z, biasm, bias0, bias1, lam,
      subln_w.astype(F32).reshape(1, LANES))


def _diff_decode_kernel(q_ref, kp_ref, vp_ref, kn_ref, vn_ref, z_ref, bias_ref, lam_ref, subln_ref, o_ref,
                        *, p_len):
    t = q_ref.shape[0]
    dh = LANES // 2
    lane = _iota((t, LANES), 1)
    q = q_ref[...]
    kp = kp_ref[...].astype(BF16)
    vp = vp_ref[...].astype(BF16)
    kn = kn_ref[...]
    vn = vn_ref[...]
    bias = jnp.concatenate([bias_ref[...], bias_ref[...]], axis=0)
    row_p = _iota((2 * t, p_len), 0)
    row_n = _iota((2 * t, t), 0)
    qp_chunk = _div_pow2(p_len + jnp.where(row_p >= t, row_p - t, row_p), ATTN_CHUNK)
    qn_chunk = _div_pow2(p_len + jnp.where(row_n >= t, row_n - t, row_n), ATTN_CHUNK)
    kp_chunk = _div_pow2(_iota((2 * t, p_len), 1), ATTN_CHUNK)
    kn_chunk = _div_pow2(p_len + _iota((2 * t, t), 1), ATTN_CHUNK)
    qm = jnp.concatenate([jnp.where(_div_pow2(lane, dh) == br, q, jnp.zeros_like(q)) for br in range(2)], axis=0)
    s_p = jnp.where(kp_chunk <= qp_chunk, _dot_nt(qm, kp) + bias[:, :p_len], NEG_INF)
    s_n = jnp.where(kn_chunk <= qn_chunk, _dot_nt(qm, kn) + bias[:, p_len:], NEG_INF)
    m = jnp.maximum(jnp.max(s_p, axis=-1, keepdims=True), jnp.max(s_n, axis=-1, keepdims=True))
    e_p = jnp.exp(s_p - m)
    e_n = jnp.exp(s_n - m)
    den = jnp.sum(e_p, axis=-1, keepdims=True) + jnp.sum(e_n, axis=-1, keepdims=True)
    o2 = (_dot(e_p.astype(BF16), vp) + _dot(e_n.astype(BF16), vn)) / den
    o = o2[:t] - _diff_lambda(lam_ref) * o2[t:]
    o_ref[...] = _diff_epilogue(o, z_ref[...], subln_ref)


def _diff_attn_sample(q, k_past, v_past, kb, vb, z, rel_table, lam, subln_w):
    b, t, d = q.shape
    p_len = k_past.shape[1]
    bias = _bias_tile(rel_table, p_len, t, 0, p_len + t)
    new_spec = pl.BlockSpec((None, t, LANES), lambda bb, h: (bb, 0, h))
    past_spec = pl.BlockSpec((None, p_len, LANES), lambda bb, h: (bb, 0, h))
    return pl.pallas_call(
        functools.partial(_diff_decode_kernel, p_len=p_len),
        grid=(b, H_D),
        in_specs=[new_spec, past_spec, past_spec, new_spec, new_spec, new_spec,
                  pl.BlockSpec((None, t, p_len + t), lambda bb, h: (h, 0, 0)),
                  pl.BlockSpec((SUBLANES, LANES), lambda bb, h: (0, 0)),
                  pl.BlockSpec((1, LANES), lambda bb, h: (0, 0))],
        out_specs=new_spec,
        out_shape=jax.ShapeDtypeStruct((b, t, d), BF16),
        compiler_params=_cparams("arbitrary", "arbitrary"),
        name="diff_attention_sample",
    )(q, k_past, v_past, kb, vb, z, bias, lam, subln_w.astype(F32).reshape(1, LANES))


def _rope_kernel(inv_ref, cos_ref, sin_ref, *, start):
    t, w = cos_ref.shape
    pos = (start + pl.program_id(0) * t + _iota((t, w), 0)).astype(F32)
    ang = pos * inv_ref[...]
    even = (_iota((t, w), 1) & 1) == 0
    cos_ref[...] = jnp.cos(ang)
    sn = jnp.sin(ang)
    sin_ref[...] = jnp.where(even, -sn, sn)


def _rope_tables(t, start, dk):
    inv_half = np.power(np.float32(ROPE_BASE), -np.arange(0, dk, 2, dtype=np.float32) / np.float32(dk))
    inv = jnp.asarray(np.repeat(inv_half.astype(np.float32), 2).reshape(1, dk))
    tt = _row_tile(t, 512)
    return pl.pallas_call(
        functools.partial(_rope_kernel, start=start),
        grid=(t // tt,),
        in_specs=[pl.BlockSpec((1, dk), lambda i: (0, 0))],
        out_specs=[pl.BlockSpec((tt, dk), lambda i: (i, 0))] * 2,
        out_shape=[jax.ShapeDtypeStruct((t, dk), F32)] * 2,
        compiler_params=_cparams("arbitrary"),
        name="rope_tables",
    )(inv)


def _rotate_pairs(x, cos, sin_signed):
    slabs = []
    for c0 in range(0, x.shape[-1], LANES):
        xs = x[:, c0:c0 + LANES]
        even = (_iota(xs.shape, 1) & 1) == 0
        slabs.append(jnp.where(even, pltpu.roll(xs, LANES - 1, 1), pltpu.roll(xs, 1, 1)))
    return x * cos + jnp.concatenate(slabs, axis=1) * sin_signed


def _ret_proj_kernel(x_ref, shift_ref, scale_ref, w_ref, cos_ref, sin_ref, q_ref, k_ref, v_ref, z_ref,
                     *, q_scale):
    u = _modulated(x_ref, shift_ref, scale_ref)
    d = x_ref.shape[-1]
    dk = cos_ref.shape[-1]
    cos = cos_ref[...]
    sn = sin_ref[...]
    for h in range(d // dk):
        qh = _dot(u, w_ref[:, h * dk:(h + 1) * dk])
        q_ref[:, h * dk:(h + 1) * dk] = (_rotate_pairs(qh, cos, sn) * q_scale).astype(BF16)
        kh = _dot(u, w_ref[:, d + h * dk:d + (h + 1) * dk])
        k_ref[:, h * dk:(h + 1) * dk] = _rotate_pairs(kh, cos, sn).astype(BF16)
    for s in range(2):
        v_ref[:, s * d:(s + 1) * d] = _dot(u, w_ref[:, (2 + s) * d:(3 + s) * d]).astype(BF16)
        z_ref[:, s * d:(s + 1) * d] = _dot(u, w_ref[:, (4 + s) * d:(5 + s) * d]).astype(BF16)


def _ret_proj(x, mod4, layer, boff, w_in, cos, sin_signed):
    b, t, d = x.shape
    dk = d // H_R
    tm = _row_tile(t, 512)
    tab_spec = pl.BlockSpec((tm, dk), lambda bb, i: (i, 0))
    return pl.pallas_call(
        functools.partial(_ret_proj_kernel, q_scale=dk ** -0.5),
        grid=(b, t // tm),
        in_specs=[_rows_spec(tm, d)] + _mod_specs(layer, boff, d, (0, 1)) + [_const_spec((d, 6 * d)), tab_spec, tab_spec],
        out_specs=[_rows_spec(tm, d), _rows_spec(tm, d), _rows_spec(tm, 2 * d), _rows_spec(tm, 2 * d)],
        out_shape=[jax.ShapeDtypeStruct((b, t, d), BF16), jax.ShapeDtypeStruct((b, t, d), BF16),
                   jax.ShapeDtypeStruct((b, t, 2 * d), BF16), jax.ShapeDtypeStruct((b, t, 2 * d), BF16)],
        compiler_params=_cparams("arbitrary", "arbitrary"),
        name="ret_in_proj",
    )(x, mod4, mod4, w_in.astype(BF16), cos, sin_signed)


def _ret_kernel(q_ref, k_ref, v_ref, z_ref, s0_ref, gn_ref, o_ref, s_ref, intra_ref):
    ti = pl.program_id(1)
    lr = q_ref.shape[0]
    dk = q_ref.shape[-1] // H_R
    dv = v_ref.shape[-1] // H_R
    heads = range(H_R)
    log_gamma = [math.log1p(-(2.0 ** (-5.0 - h))) for h in heads]

    @pl.when(ti == 0)
    def _():
        s_ref[...] = s0_ref[...]
        rel = (_iota((lr, lr), 0) - _iota((lr, lr), 1)).astype(F32)
        for h in heads:
            intra_ref[h] = jnp.where(rel >= 0, jnp.exp(log_gamma[h] * jnp.maximum(rel, 0.0)), 0.0)

    idx = _iota((lr, 1), 0).astype(F32)
    for h in heads:
        q_dec = jnp.exp(log_gamma[h] * (idx + 1.0))
        k_dec = jnp.exp(log_gamma[h] * (lr - 1.0 - idx))
        qh = q_ref[:, h * dk:(h + 1) * dk]
        kh = k_ref[:, h * dk:(h + 1) * dk]
        vh = v_ref[:, h * dv:(h + 1) * dv]
        s = s_ref[h]
        att = _dot_nt(qh, kh) * intra_ref[h]
        o = _dot(att.astype(BF16), vh) + _dot(qh, s.astype(BF16)) * q_dec
        s_ref[h] = s * math.exp(log_gamma[h] * lr) + _dot_tn((kh.astype(F32) * k_dec).astype(BF16), vh)
        mu = jnp.mean(o, axis=-1, keepdims=True)
        oc = o - mu
        var = jnp.mean(oc * oc, axis=-1, keepdims=True)
        on = oc * lax.rsqrt(var + LN_EPS) * gn_ref[:, h * dv:(h + 1) * dv]
        zz = z_ref[:, h * dv:(h + 1) * dv].astype(F32)
        o_ref[:, h * dv:(h + 1) * dv] = (on * _silu(zz)).astype(BF16)


def _ret_mix(q, k, v, z, s0, gn_w):
    b, t, d = q.shape
    dk = d // H_R
    dv = v.shape[-1] // H_R
    lr = _row_tile(t, 256)
    state_spec = pl.BlockSpec((None, H_R, dk, dv), lambda bb, i: (bb, 0, 0, 0))
    return pl.pallas_call(
        _ret_kernel,
        grid=(b, t // lr),
        in_specs=[_rows_spec(lr, d), _rows_spec(lr, d), _rows_spec(lr, 2 * d), _rows_spec(lr, 2 * d), state_spec,
                  _const_spec((1, 2 * d))],
        out_specs=[_rows_spec(lr, 2 * d), state_spec],
        out_shape=[jax.ShapeDtypeStruct((b, t, 2 * d), BF16), jax.ShapeDtypeStruct((b, H_R, dk, dv), F32)],
        scratch_shapes=[pltpu.VMEM((H_R, lr, lr), F32)],
        compiler_params=_cparams("arbitrary", "arbitrary"),
        name="retention_mixer",
    )(q, k, v, z, s0.astype(F32), gn_w.astype(F32).reshape(1, 2 * d))


def _run_group(x, mod4, boff, state_gdn, state_gdn_conv, cache_fox_k, cache_fox_v, cache_fox_logf,
               cache_diff_k, cache_diff_v, state_ret, start, p):
    b, t, d = x.shape
    dk_g = d // H_G

    if state_gdn is None:
        state_gdn = jnp.zeros((b, H_G, dk_g, dk_g), F32)
        state_gdn_conv = jnp.zeros((b, CONV_W - 1, 3 * d), F32)
    q, k, v, gates, z, tail = _gdn_proj(x, mod4, 0, boff, p["gdn_w_in"], state_gdn_conv, p["gdn_conv_w"],
                                        p["gdn_a_log"], p["gdn_dt_bias"])
    o, gdn_state = _gdn_mix(q, k, v, gates, z, state_gdn, p["gdn_norm_w"])
    gdn_conv = tail[:, SUBLANES - (CONV_W - 1):, :]
    x = _out_proj(o, x, mod4, 0, boff, p["gdn_w_out"], p["ln_g"][0], p["ln_b"][0])

    prompt = cache_fox_k is None
    q, k32, v32, kb, vb, z, logf = _fox_proj(x, mod4, 1, boff, p["fox_w_in"], p["fox_b_f"], transposed=prompt)
    if prompt:
        kaug, qaugt = _fox_aug(logf)
        o = _fox_attn_prompt(q, kb, kaug, vb, qaugt, z)
    else:
        zero_c = jnp.zeros((b, 1, H_F), F32)
        p_len = cache_fox_k.shape[1]
        cum_pn, cum_pt = _cumsum_time(cache_fox_logf.astype(F32), zero_c)
        cum_n, cum_nt = _cumsum_time(logf, cum_pn[:, p_len - 1:, :])
        o = _fox_attn_sample(q, cache_fox_k.reshape(b, p_len, d), cache_fox_v.reshape(b, p_len, d), kb, vb, z,
                             cum_n, jnp.concatenate([cum_pt, cum_nt], axis=2))
    fox_k = k32.reshape(b, t, H_F, d // H_F)
    fox_v = v32.reshape(b, t, H_F, d // H_F)
    x = _out_proj(o, x, mod4, 1, boff, p["fox_w_out"], p["ln_g"][1], p["ln_b"][1])

    q, k32, v32, kb, vb, z = _diff_proj(x, mod4, 2, boff, p["diff_w_in"], transposed=prompt)
    lam = _lam_pack(p["diff_lam_q1"], p["diff_lam_k1"], p["diff_lam_q2"], p["diff_lam_k2"])
    if prompt:
        o = _diff_attn_prompt(q, kb, vb, z, p["rel_bias_table"], lam, p["diff_subln_w"])
    else:
        p_len = cache_diff_k.shape[1]
        o = _diff_attn_sample(q, cache_diff_k.reshape(b, p_len, d), cache_diff_v.reshape(b, p_len, d), kb, vb, z,
                              p["rel_bias_table"], lam, p["diff_subln_w"])
    diff_k = k32.reshape(b, t, H_D, 2, d // (2 * H_D))
    diff_v = v32.reshape(b, t, H_D, d // H_D)
    x = _out_proj(o, x, mod4, 2, boff, p["diff_w_out"], p["ln_g"][2], p["ln_b"][2])

    dk_r = d // H_R
    cos, sin_signed = _rope_tables(t, start, dk_r)
    q, k, v, z = _ret_proj(x, mod4, 3, boff, p["ret_w_in"], cos, sin_signed)
    if state_ret is None:
        state_ret = jnp.zeros((b, H_R, dk_r, 2 * d // H_R), F32)
    o, ret_state = _ret_mix(q, k, v, z, state_ret, p["ret_gn_w"])
    x = _out_proj(o, x, mod4, 3, boff, p["ret_w_out"], p["ln_g"][3], p["ln_b"][3])

    return x, gdn_state, gdn_conv, fox_k, fox_v, logf, diff_k, diff_v, ret_state


def kernel(x_prompt, x_sample, c_prompt, c_sample, state_gdn, state_gdn_conv, cache_fox_k, cache_fox_v, cache_fox_logf, cache_diff_k, cache_diff_v, state_ret, ada_w, ada_b, ln_g, ln_b, gdn_w_in, gdn_conv_w, gdn_a_log, gdn_dt_bias, gdn_norm_w, gdn_w_out, fox_w_in, fox_b_f, fox_w_out, rel_bias_table, diff_w_in, diff_lam_q1, diff_lam_k1, diff_lam_q2, diff_lam_k2, diff_subln_w, diff_w_out, ret_w_in, ret_gn_w, ret_w_out):
    p = dict(ln_g=ln_g, ln_b=ln_b, gdn_w_in=gdn_w_in, gdn_conv_w=gdn_conv_w, gdn_a_log=gdn_a_log,
             gdn_dt_bias=gdn_dt_bias, gdn_norm_w=gdn_norm_w, gdn_w_out=gdn_w_out, fox_w_in=fox_w_in,
             fox_b_f=fox_b_f, fox_w_out=fox_w_out, rel_bias_table=rel_bias_table, diff_w_in=diff_w_in,
             diff_lam_q1=diff_lam_q1, diff_lam_k1=diff_lam_k1, diff_lam_q2=diff_lam_q2, diff_lam_k2=diff_lam_k2,
             diff_subln_w=diff_subln_w, diff_w_out=diff_w_out, ret_w_in=ret_w_in, ret_gn_w=ret_gn_w,
             ret_w_out=ret_w_out)
    bp = x_prompt.shape[0]
    d = x_prompt.shape[-1]
    mod = _modulation(jnp.concatenate([c_prompt, c_sample], axis=0), ada_w, ada_b)
    mod4 = mod.reshape(mod.shape[0], mod.shape[1], 1, 3 * d)
    outs_p = _run_group(x_prompt, mod4, 0, None, None, None, None, None, None, None, None, 0, p)
    outs_s = _run_group(x_sample, mod4, bp, state_gdn, state_gdn_conv, cache_fox_k, cache_fox_v, cache_fox_logf,
                        cache_diff_k, cache_diff_v, state_ret, cache_fox_k.shape[1], p)
    return (outs_p[0], outs_s[0]) + tuple(outs_p[1:]) + tuple(outs_s[1:])
```

```python
import functools
import math

import numpy as np
import jax
import jax.numpy as jnp
from jax import lax
from jax.experimental import pallas as pl
from jax.experimental.pallas import tpu as pltpu

F32 = jnp.float32
BF16 = jnp.bfloat16

DEPTH = 4
ATTN_TQ = 1024
ATTN_TK = 512
GDN_CHUNK = 64
ATTN_CHUNK = 64
DEEPNORM_ALPHA = (2.0 * DEPTH) ** 0.25
LN_EPS = 1e-5
NORM_EPS = 1e-6
NEG_INF = -1e30
LOG2E = math.log2(math.e)
H_G, H_F, H_D, H_R = 8, 16, 8, 4
CONV_W = 4
DIFF_LAYER = 2
LAMBDA_INIT = 0.8 - 0.6 * math.exp(-0.3 * DIFF_LAYER)
N_BUCKETS = 32
MAX_DISTANCE = 128
ROPE_BASE = 10000.0

LANES = 128
SUBLANES = 8
VMEM_LIMIT = 56 * 1024 * 1024


def _cparams(*sem):
    return pltpu.CompilerParams(dimension_semantics=sem, vmem_limit_bytes=VMEM_LIMIT)


def _sigmoid(x):
    return 1.0 / (1.0 + jnp.exp(-x))


def _silu(x):
    hx = 0.5 * x
    return hx + hx * jnp.tanh(hx)


def _softplus(x):
    return jnp.maximum(x, 0.0) + jnp.log(1.0 + jnp.exp(-jnp.abs(x)))


def _dot(a, b):
    return jnp.dot(a, b, preferred_element_type=F32)


def _dot_nt(a, b):
    return lax.dot_general(a, b, (((1,), (1,)), ((), ())), preferred_element_type=F32)


def _dot_tn(a, b):
    return lax.dot_general(a, b, (((0,), (0,)), ((), ())), preferred_element_type=F32)


def _split3(x):
    x1 = x.astype(BF16)
    r1 = x - x1.astype(F32)
    x2 = r1.astype(BF16)
    x3 = (r1 - x2.astype(F32)).astype(BF16)
    return x1, x2, x3


def _dot_exact_l(m01, x):
    x1, x2, x3 = _split3(x)
    return _dot(m01, x1) + _dot(m01, x2) + _dot(m01, x3)


def _dot_exact_nt(m01, x):
    x1, x2, x3 = _split3(x)
    return _dot_nt(m01, x1) + _dot_nt(m01, x2) + _dot_nt(m01, x3)


def _iota(shape, dim):
    return lax.broadcasted_iota(jnp.int32, shape, dim)


def _div_pow2(x, n):
    assert n & (n - 1) == 0
    return jnp.right_shift(x, n.bit_length() - 1)


def _row_tile(t, pref):
    return pref if t % pref == 0 else t


def _mod_kernel(c_ref, w_ref, b_ref, o_ref):
    s = _silu(c_ref[...])
    w = w_ref[...]
    s1 = s.astype(BF16)
    s2 = (s - s1.astype(F32)).astype(BF16)
    w1 = w.astype(BF16)
    w2 = (w - w1.astype(F32)).astype(BF16)
    o_ref[...] = _dot(s1, w1) + _dot(s1, w2) + _dot(s2, w1) + b_ref[...]


def _modulation(c_all, ada_w, ada_b):
    nb, d = c_all.shape
    depth, _, n = ada_w.shape
    tn = 1024
    return pl.pallas_call(
        _mod_kernel,
        grid=(depth, n // tn),
        in_specs=[pl.BlockSpec((nb, d), lambda l, j: (0, 0)),
                  pl.BlockSpec((None, d, tn), lambda l, j: (l, 0, j)),
                  pl.BlockSpec((None, 1, tn), lambda l, j: (l, 0, j))],
        out_specs=pl.BlockSpec((None, nb, tn), lambda l, j: (l, 0, j)),
        out_shape=jax.ShapeDtypeStruct((depth, nb, n), F32),
        compiler_params=_cparams("arbitrary", "arbitrary"),
        name="adaln_modulation",
    )(c_all, ada_w, ada_b.reshape(depth, 1, n))


def _mod_specs(layer, boff, d, which):
    return [pl.BlockSpec((None, None, 1, d), lambda b, i, w=w: (layer, boff + b, 0, w)) for w in which]


def _modulated(x_ref, shift_ref, scale_ref):
    return (x_ref[...] * (1.0 + scale_ref[...]) + shift_ref[...]).astype(BF16)


def _const_spec(shape):
    return pl.BlockSpec(shape, lambda b, i: (0,) * len(shape))


def _rows_spec(tm, n):
    return pl.BlockSpec((None, tm, n), lambda b, i: (b, i, 0))


def _gdn_proj_kernel(x_ref, shift_ref, scale_ref, wqkv_ref, wba_ref, wz_ref, cbuf_ref, cw_ref, avec_ref, dtvec_ref,
                     q_ref, k_ref, v_ref, gates_ref, z_ref, tail_ref, ext_ref, *, chunk):
    i = pl.program_id(1)
    tm, d = x_ref.shape
    dk = d // H_G
    u = _modulated(x_ref, shift_ref, scale_ref)

    @pl.when(i == 0)
    def _():
        ext_ref[0:SUBLANES, :] = cbuf_ref[...]

    for s in range(3):
        ext_ref[SUBLANES:SUBLANES + tm, s * d:(s + 1) * d] = _dot(u, wqkv_ref[:, s * d:(s + 1) * d])
    outs = (q_ref, k_ref, v_ref)
    for s in range(3):
        for h in range(H_G):
            c0 = s * d + h * dk
            e = ext_ref[:, c0:c0 + dk]
            acc = cw_ref[0:1, c0:c0 + dk] * e
            for j in range(1, CONV_W):
                acc = pltpu.roll(acc, 1, 0) + cw_ref[j:j + 1, c0:c0 + dk] * e
            hy = acc[SUBLANES:, :]
            y = hy + hy * jnp.tanh(hy)
            if s < 2:
                inv = lax.rsqrt(jnp.sum(y * y, axis=-1, keepdims=True) + NORM_EPS)
                y = y * (inv * (dk ** -0.5) if s == 0 else inv)
            outs[s][:, h * dk:(h + 1) * dk] = y.astype(BF16)
    tail = ext_ref[tm:tm + SUBLANES, :]
    tail_ref[...] = tail
    ext_ref[0:SUBLANES, :] = tail

    ba = _dot(u, wba_ref[...])
    g = -jnp.exp(avec_ref[...]) * _softplus(ba + dtvec_ref[...])
    r = _iota((tm, tm), 0)
    c = _iota((tm, tm), 1)
    tri = jnp.where(_div_pow2(r, chunk) == _div_pow2(c, chunk), jnp.where(r >= c, 1.0, 0.0), 0.0).astype(BF16)
    gcum = _dot_exact_l(tri, g)
    gates_ref[...] = jnp.where(_iota((tm, LANES), 1) < H_G, _sigmoid(ba), gcum)
    z_ref[...] = _dot(u, wz_ref[...]).astype(BF16)


def _gdn_proj(x, mod4, layer, boff, w_in, conv_buf, conv_w, a_log, dt_bias):
    b, t, d = x.shape
    tm = _row_tile(t, 256)
    chunk = min(GDN_CHUNK, t)
    assert t >= CONV_W - 1 and tm >= SUBLANES and tm % chunk == 0
    wqkv = w_in[:, :3 * d].astype(BF16)
    wba = jnp.pad(w_in[:, 3 * d:3 * d + 2 * H_G], ((0, 0), (0, LANES - 2 * H_G))).astype(BF16)
    wz = w_in[:, 3 * d + 2 * H_G:].astype(BF16)
    cbuf = jnp.pad(conv_buf.astype(F32), ((0, 0), (SUBLANES - (CONV_W - 1), 0), (0, 0)))
    cw = jnp.pad(0.5 * conv_w.astype(F32), ((0, SUBLANES - CONV_W), (0, 0)))
    avec = jnp.pad(a_log.astype(F32), (H_G, LANES - 2 * H_G)).reshape(1, LANES)
    dtvec = jnp.pad(dt_bias.astype(F32), (H_G, LANES - 2 * H_G)).reshape(1, LANES)
    bf16o = jax.ShapeDtypeStruct((b, t, d), BF16)
    tail_spec = pl.BlockSpec((None, SUBLANES, 3 * d), lambda bb, i: (bb, 0, 0))
    return pl.pallas_call(
        functools.partial(_gdn_proj_kernel, chunk=chunk),
        grid=(b, t // tm),
        in_specs=[_rows_spec(tm, d)] + _mod_specs(layer, boff, d, (0, 1))
                 + [_const_spec((d, 3 * d)), _const_spec((d, LANES)), _const_spec((d, d)), tail_spec,
                    _const_spec((SUBLANES, 3 * d)), _const_spec((1, LANES)), _const_spec((1, LANES))],
        out_specs=[_rows_spec(tm, d)] * 3 + [_rows_spec(tm, LANES), _rows_spec(tm, d), tail_spec],
        out_shape=[bf16o, bf16o, bf16o, jax.ShapeDtypeStruct((b, t, LANES), F32), bf16o,
                   jax.ShapeDtypeStruct((b, SUBLANES, 3 * d), F32)],
        scratch_shapes=[pltpu.VMEM((tm + SUBLANES, 3 * d), F32)],
        compiler_params=_cparams("arbitrary", "arbitrary"),
        name="gdn_in_proj",
    )(x, mod4, mod4, wqkv, wba, wz, cbuf, cw, avec, dtvec)


def _unit_lower_inverse_minus_identity(mats):
    n = mats[0].shape[0]
    r = _iota((n, n), 0)
    c = _iota((n, n), 1)

    def mm(xs, ys):
        return [_dot(x.astype(BF16), y.astype(BF16)) for x, y in zip(xs, ys)]

    base = 8
    diag = _div_pow2(r, base) == _div_pow2(c, base)
    d = [jnp.where(diag, a, 0.0) for a in mats]
    d2 = mm(d, d)
    d4 = mm(d2, d2)
    nn = [-x for x in d]
    nn = [x + y + z for x, y, z in zip(nn, d2, mm(nn, d2))]
    nn = [x + y + z for x, y, z in zip(nn, d4, mm(nn, d4))]
    m = base
    while m < n:
        pair = (_div_pow2(r, 2 * m) == _div_pow2(c, 2 * m)) & (_div_pow2(r, m) != _div_pow2(c, m))
        off = [jnp.where(pair, a, 0.0) for a in mats]
        y = [o + p for o, p in zip(off, mm(nn, off))]
        x = [p + q for p, q in zip(y, mm(y, nn))]
        nn = [p - q for p, q in zip(nn, x)]
        m *= 2
    return nn


def _gdn_kernel(q_ref, k_ref, v_ref, gates_ref, z_ref, s0_ref, nw_ref, o_ref, s_ref, *, chunk):
    ti = pl.program_id(1)
    grp, tb, d = z_ref.shape
    dk = d // H_G
    n_chunks = tb // chunk

    @pl.when(ti == 0)
    def _():
        s_ref[...] = s0_ref[...]

    ri = _iota((chunk, chunk), 0)
    ci = _iota((chunk, chunk), 1)
    eye_l = (_iota((LANES, LANES), 0) == _iota((LANES, LANES), 1)).astype(BF16)
    incl = ri >= ci
    strict = ri > ci
    nw = nw_ref[...]
    items = [(g, h) for g in range(grp) for h in range(H_G)]

    def chunk_body(cidx, carry):
        r0 = pl.multiple_of(cidx * chunk, chunk)
        rows = pl.ds(r0, chunk)
        gates = [gates_ref[g, rows, :] for g in range(grp)]
        gates_t = [_dot_exact_nt(eye_l, x) for x in gates]

        kbf = [k_ref[g, rows, h * dk:(h + 1) * dk] for g, h in items]
        qbf = [q_ref[g, rows, h * dk:(h + 1) * dk] for g, h in items]
        q = [x.astype(F32) for x in qbf]
        k = [x.astype(F32) for x in kbf]
        v = [v_ref[g, rows, h * dk:(h + 1) * dk].astype(F32) for g, h in items]
        beta = [gates[g][:, h:h + 1] for g, h in items]
        gcol = [gates[g][:, H_G + h:H_G + h + 1] for g, h in items]
        grow = [gates_t[g][H_G + h:H_G + h + 1, :] for g, h in items]
        dec_incl = [jnp.exp(jnp.where(incl, gc - gr, NEG_INF)) for gc, gr in zip(gcol, grow)]
        kb = [x * bt for x, bt in zip(k, beta)]
        a_mat = [_dot_nt(x.astype(BF16), y) for x, y in zip(kb, kbf)]
        qk = [_dot_nt(x, y) for x, y in zip(qbf, kbf)]
        a_mat = [jnp.where(strict, x * e, 0.0) for x, e in zip(a_mat, dec_incl)]
        qk = [x * e for x, e in zip(qk, dec_incl)]
        exp_g = [jnp.exp(gc) for gc in gcol]
        rhs = [jnp.concatenate([x * bt, y * e], axis=1) for x, bt, y, e in zip(v, beta, kb, exp_g)]
        nn = _unit_lower_inverse_minus_identity(a_mat)
        sol = [x + _dot(y.astype(BF16), x.astype(BF16)) for x, y in zip(rhs, nn)]
        s = [s_ref[g, h] for g, h in items]
        sb = [x.astype(BF16) for x in s]
        v_res = [x[:, :dk] - _dot(x[:, dk:].astype(BF16), y) for x, y in zip(sol, sb)]
        vrb = [x.astype(BF16) for x in v_res]
        o = [_dot((x * e).astype(BF16), y) for x, e, y in zip(q, exp_g, sb)]
        o = [x + _dot(y.astype(BF16), z) for x, y, z in zip(o, qk, vrb)]
        g_last = [gc[chunk - 1:chunk, :] for gc in gcol]
        k_dec = [(x * jnp.exp(gl - gc)).astype(BF16) for x, gl, gc in zip(k, g_last, gcol)]
        s_add = [_dot_tn(x, y) for x, y in zip(k_dec, vrb)]
        for i, (g, h) in enumerate(items):
            s_ref[g, h] = s[i] * jnp.exp(g_last[i]) + s_add[i]
            on = o[i] * lax.rsqrt(jnp.mean(o[i] * o[i], axis=-1, keepdims=True) + NORM_EPS) * nw
            zz = z_ref[g, rows, h * dk:(h + 1) * dk].astype(F32)
            o_ref[g, rows, h * dk:(h + 1) * dk] = (on * _silu(zz)).astype(BF16)
        return carry

    lax.fori_loop(0, n_chunks, chunk_body, 0)


def _gdn_mix(q, k, v, gates, z, s0, norm_w):
    b, t, d = q.shape
    dk = d // H_G
    chunk = min(GDN_CHUNK, t)
    tb = _row_tile(t, 4 * chunk)
    grp = 4 if b % 4 == 0 else (2 if b % 2 == 0 else 1)
    nw = norm_w.astype(F32).reshape(1, dk)

    def rows(n):
        return pl.BlockSpec((grp, tb, n), lambda bb, i: (bb, i, 0))

    state_spec = pl.BlockSpec((grp, H_G, dk, dk), lambda bb, i: (bb, 0, 0, 0))
    return pl.pallas_call(
        functools.partial(_gdn_kernel, chunk=chunk),
        grid=(b // grp, t // tb),
        in_specs=[rows(d)] * 3 + [rows(LANES), rows(d), state_spec, _const_spec((1, dk))],
        out_specs=[rows(d), state_spec],
        out_shape=[jax.ShapeDtypeStruct((b, t, d), BF16), jax.ShapeDtypeStruct((b, H_G, dk, dk), F32)],
        compiler_params=_cparams("arbitrary", "arbitrary"),
        name="gdn_mixer",
    )(q, k, v, gates, z, s0.astype(F32), nw)


def _out_proj_kernel(o_ref, x_ref, gate_ref, w_ref, g_ref, b_ref, y_ref):
    h = _dot(o_ref[...], w_ref[...])
    y = DEEPNORM_ALPHA * x_ref[...] + (1.0 + gate_ref[...]) * h
    mu = jnp.mean(y, axis=-1, keepdims=True)
    yc = y - mu
    var = jnp.mean(yc * yc, axis=-1, keepdims=True)
    y_ref[...] = yc * lax.rsqrt(var + LN_EPS) * g_ref[...] + b_ref[...]


def _out_proj(o, x, mod4, layer, boff, w_out, ln_g, ln_b):
    b, t, d = x.shape
    kdim = o.shape[-1]
    tm = _row_tile(t, 1024)
    return pl.pallas_call(
        _out_proj_kernel,
        grid=(b, t // tm),
        in_specs=[_rows_spec(tm, kdim), _rows_spec(tm, d)] + _mod_specs(layer, boff, d, (2,))
                 + [_const_spec((kdim, d)), _const_spec((1, d)), _const_spec((1, d))],
        out_specs=_rows_spec(tm, d),
        out_shape=jax.ShapeDtypeStruct((b, t, d), F32),
        compiler_params=_cparams("arbitrary", "arbitrary"),
        name="out_proj_postnorm",
    )(o, x, mod4, w_out.astype(BF16), ln_g.reshape(1, d), ln_b.reshape(1, d))


def _qkvz_outputs(u, w_ref, q_ref, k32_ref, v32_ref, kb_ref, vb_ref, z_ref, q_scale, transposed):
    d = u.shape[-1]
    if transposed:
        q_ref[...] = (_dot_nt(w_ref[:, 0:d], u) * q_scale).astype(BF16)
    else:
        q_ref[...] = (_dot(u, w_ref[:, 0:d]) * q_scale).astype(BF16)
    k = _dot(u, w_ref[:, d:2 * d])
    k32_ref[...] = k
    kb_ref[...] = k.astype(BF16)
    v = _dot(u, w_ref[:, 2 * d:3 * d])
    v32_ref[...] = v
    vb_ref[...] = (v.T if transposed else v).astype(BF16)
    z_ref[...] = _dot(u, w_ref[:, 3 * d:4 * d]).astype(BF16)


def _qkvz_weights(w_in, d, transposed):
    w = w_in[:, :4 * d]
    if transposed:
        w = jnp.concatenate([w[:, :d].T, w[:, d:]], axis=1)
    return w.astype(BF16)


def _qkvz_specs(b, t, d, tm, transposed):
    cols_spec = pl.BlockSpec((None, d, tm), lambda bb, i: (bb, 0, i))
    rows = _rows_spec(tm, d)
    f32o = jax.ShapeDtypeStruct((b, t, d), F32)
    bf16o = jax.ShapeDtypeStruct((b, t, d), BF16)
    bf16t = jax.ShapeDtypeStruct((b, d, t), BF16)
    if transposed:
        return [cols_spec, rows, rows, rows, cols_spec, rows], [bf16t, f32o, f32o, bf16o, bf16t, bf16o]
    return [rows] * 6, [bf16o, f32o, f32o, bf16o, bf16o, bf16o]


def _fox_proj_kernel(x_ref, shift_ref, scale_ref, w_ref, wf_ref, bf_ref,
                     q_ref, k32_ref, v32_ref, kb_ref, vb_ref, z_ref, logf_ref, *, q_scale, transposed):
    u = _modulated(x_ref, shift_ref, scale_ref)
    _qkvz_outputs(u, w_ref, q_ref, k32_ref, v32_ref, kb_ref, vb_ref, z_ref, q_scale, transposed)
    f = _dot(u, wf_ref[...])[:, :H_F] + bf_ref[...]
    logf_ref[...] = -_softplus(-f)


def _fox_proj(x, mod4, layer, boff, w_in, b_f, transposed):
    b, t, d = x.shape
    tm = _row_tile(t, 512)
    w = _qkvz_weights(w_in, d, transposed)
    wf = jnp.pad(w_in[:, 4 * d:], ((0, 0), (0, LANES - H_F))).astype(BF16)
    out_specs, out_shape = _qkvz_specs(b, t, d, tm, transposed)
    return pl.pallas_call(
        functools.partial(_fox_proj_kernel, q_scale=(d // H_F) ** -0.5 * (LOG2E if transposed else 1.0),
                          transposed=transposed),
        grid=(b, t // tm),
        in_specs=[_rows_spec(tm, d)] + _mod_specs(layer, boff, d, (0, 1))
                 + [_const_spec((d, 4 * d)), _const_spec((d, LANES)), _const_spec((1, H_F))],
        out_specs=out_specs + [_rows_spec(tm, H_F)],
        out_shape=out_shape + [jax.ShapeDtypeStruct((b, t, H_F), F32)],
        compiler_params=_cparams("arbitrary", "arbitrary"),
        name="fox_in_proj",
    )(x, mod4, mod4, w, wf, b_f.astype(F32).reshape(1, H_F))


def _cumsum_kernel(x_ref, c0_ref, cn_ref, ct_ref, *, blk):
    s, h = x_ref.shape
    tri = (_iota((blk, blk), 0) >= _iota((blk, blk), 1)).astype(BF16)
    eye_h = (_iota((h, h), 0) == _iota((h, h), 1)).astype(BF16)
    carry = c0_ref[...]
    for i in range(s // blk):
        c = _dot_exact_l(tri, x_ref[i * blk:(i + 1) * blk, :]) + carry
        cn_ref[i * blk:(i + 1) * blk, :] = c
        ct_ref[:, i * blk:(i + 1) * blk] = _dot_exact_nt(eye_h, c)
        carry = c[blk - 1:blk, :]


def _cumsum_time(x, c0):
    b, s, h = x.shape
    blk = 256 if s % 256 == 0 else s
    return pl.pallas_call(
        functools.partial(_cumsum_kernel, blk=blk),
        grid=(b,),
        in_specs=[pl.BlockSpec((None, s, h), lambda bb: (bb, 0, 0)),
                  pl.BlockSpec((None, 1, h), lambda bb: (bb, 0, 0))],
        out_specs=[pl.BlockSpec((None, s, h), lambda bb: (bb, 0, 0)),
                   pl.BlockSpec((None, h, s), lambda bb: (bb, 0, 0))],
        out_shape=[jax.ShapeDtypeStruct((b, s, h), F32), jax.ShapeDtypeStruct((b, h, s), F32)],
        compiler_params=_cparams("arbitrary"),
        name="logf_cumsum",
    )(x, c0)


AUG = LANES // H_F


def _fox_aug_kernel(x_ref, kaug_ref, qaugt_ref, *, blk):
    s, h = x_ref.shape
    tri = (_iota((blk, blk), 0) >= _iota((blk, blk), 1)).astype(BF16)
    lane_h = _iota((h, LANES), 1)
    row_h = _iota((h, LANES), 0)
    ek = [jnp.where(lane_h == row_h * AUG + part, -1.0, 0.0).astype(BF16) for part in range(3)]
    row_q = _iota((LANES, h), 0)
    col_q = _iota((LANES, h), 1)
    eq = [jnp.where(row_q == col_q * AUG + 3 + part, 1.0, 0.0).astype(BF16) for part in range(3)]
    k_slot = _iota((blk, LANES), 1) & (AUG - 1)
    k_ones = jnp.where(k_slot >= 3, jnp.where(k_slot < 6, 1.0, 0.0), 0.0)
    q_ones = jnp.where((_iota((LANES, blk), 0) & (AUG - 1)) < 3, 1.0, 0.0)
    carry = jnp.zeros((1, h), F32)
    for i in range(s // blk):
        c = _dot_exact_l(tri, x_ref[i * blk:(i + 1) * blk, :]) + carry
        parts = _split3(c * LOG2E)
        kaug = k_ones
        qaugt = q_ones
        for part in range(3):
            kaug = kaug + _dot(parts[part], ek[part])
            qaugt = qaugt + _dot_nt(eq[part], parts[part])
        kaug_ref[i * blk:(i + 1) * blk, :] = kaug.astype(BF16)
        qaugt_ref[:, i * blk:(i + 1) * blk] = qaugt.astype(BF16)
        carry = c[blk - 1:blk, :]


def _fox_aug(logf):
    b, s, h = logf.shape
    assert h * AUG == LANES and AUG >= 6
    blk = 256 if s % 256 == 0 else s
    return pl.pallas_call(
        functools.partial(_fox_aug_kernel, blk=blk),
        grid=(b,),
        in_specs=[pl.BlockSpec((None, s, h), lambda bb: (bb, 0, 0))],
        out_specs=[pl.BlockSpec((None, s, LANES), lambda bb: (bb, 0, 0)),
                   pl.BlockSpec((None, LANES, s), lambda bb: (bb, 0, 0))],
        out_shape=[jax.ShapeDtypeStruct((b, s, LANES), BF16), jax.ShapeDtypeStruct((b, LANES, s), BF16)],
        compiler_params=_cparams("arbitrary"),
        name="fox_bias_operands",
    )(logf)


SUM_ROWS = 16


def _softmax_t_probs(s, m_ref):
    m_prev = m_ref[...]
    m_new = jnp.maximum(m_prev, jnp.max(s, axis=0, keepdims=True))
    m_ref[...] = m_new
    return jnp.exp2(s - m_new).astype(BF16), jnp.exp2(m_prev - m_new)


def _softmax_t_accumulate(vt, p, alpha, acc_ref):
    vt_ext = jnp.concatenate([vt, jnp.ones((SUM_ROWS, vt.shape[1]), BF16)], axis=0)
    acc_ref[...] = alpha * acc_ref[...] + _dot(vt_ext, p)


def _softmax_t_result(acc_ref, dv):
    return acc_ref[0:dv, :] / acc_ref[dv:dv + 1, :]


def _fox_attn_kernel(qt_ref, k_ref, kaug_ref, vt_ref, qaugt_ref, z_ref, o_ref,
                     qcat_ref, s_ref, m_ref, acc_ref, *, tk):
    hp = pl.program_id(1)
    qi = pl.program_id(2)
    tq = qt_ref.shape[1]
    assert tq == 2 * tk
    dh = LANES // 2
    row = _iota((LANES, tq), 0)
    qa = qaugt_ref[...].astype(F32)
    for hh in range(2):
        qcat_ref[hh, hh * dh:(hh + 1) * dh, :] = qt_ref[hh * dh:(hh + 1) * dh, :]
        qcat_ref[hh, (1 - hh) * dh:(2 - hh) * dh, :] = jnp.zeros((dh, tq), BF16)
        qcat_ref[hh, LANES:2 * LANES, :] = jnp.where(_div_pow2(row, AUG) == hp * 2 + hh, qa, 0.0).astype(BF16)
    m_ref[...] = jnp.full(m_ref.shape, NEG_INF, F32)
    acc_ref[...] = jnp.zeros(acc_ref.shape, F32)
    n_full = qi * 2
    causal = _iota((tk, tk), 0) <= _iota((tk, tk), 1)

    def scores(j, slot):
        k0 = pl.multiple_of(j * tk, tk)
        kcat = jnp.concatenate([k_ref[pl.ds(k0, tk), :], kaug_ref[pl.ds(k0, tk), :]], axis=1)
        for hh in range(2):
            s_ref[slot, hh] = _dot(kcat, qcat_ref[hh])

    def consume(j, slot, diagonal=(False, False), q0=0):
        k0 = pl.multiple_of(j * tk, tk)
        for hh in range(2):
            for half in range(q0 // tk, 2):
                s = s_ref[slot, hh, :, half * tk:(half + 1) * tk]
                if diagonal[half]:
                    s = jnp.where(causal, s, NEG_INF)
                p, alpha = _softmax_t_probs(s, m_ref.at[hh, half])
                _softmax_t_accumulate(vt_ref[hh * dh:(hh + 1) * dh, pl.ds(k0, tk)], p, alpha, acc_ref.at[hh, half])

    def pair(cur, a):
        nxt = 1 - cur
        scores(a + 2, 2 * nxt)
        consume(a, 2 * cur)
        scores(a + 3, 2 * nxt + 1)
        consume(a + 1, 2 * cur + 1)

    def last_pair(cur):
        consume(n_full, 2 * cur, diagonal=(True, False))
        consume(n_full + 1, 2 * cur + 1, diagonal=(False, True), q0=tk)

    scores(0, 0)
    scores(1, 1)

    def body(i, carry):
        pair(0, 4 * i)
        pair(1, 4 * i + 2)
        return carry

    lax.fori_loop(0, lax.shift_right_logical(qi, 1), body, 0)

    @pl.when((qi & 1) == 1)
    def _():
        pair(0, n_full - 2)
        last_pair(1)

    @pl.when((qi & 1) == 0)
    def _():
        last_pair(0)
    o_t = jnp.concatenate(
        [jnp.concatenate([_softmax_t_result(acc_ref.at[hh, half], dh) for half in range(2)], axis=1)
         for hh in range(2)], axis=0)
    o_ref[...] = (o_t.T * _silu(z_ref[...].astype(F32))).astype(BF16)


def _fox_attn_prompt(qt, kb, kaug, vt, qaugt, z):
    b, d, t = qt.shape
    tq, tk = ATTN_TQ, ATTN_TK
    assert t % tq == 0
    hpairs = d // LANES
    dh = LANES // 2
    return pl.pallas_call(
        functools.partial(_fox_attn_kernel, tk=tk),
        grid=(b, hpairs, t // tq),
        in_specs=[pl.BlockSpec((None, LANES, tq), lambda bb, hp, i: (bb, hp, i)),
                  pl.BlockSpec((None, t, LANES), lambda bb, hp, i: (bb, 0, hp)),
                  pl.BlockSpec((None, t, LANES), lambda bb, hp, i: (bb, 0, 0)),
                  pl.BlockSpec((None, LANES, t), lambda bb, hp, i: (bb, hp, 0)),
                  pl.BlockSpec((None, LANES, tq), lambda bb, hp, i: (bb, 0, i)),
                  pl.BlockSpec((None, tq, LANES), lambda bb, hp, i: (bb, i, hp))],
        out_specs=pl.BlockSpec((None, tq, LANES), lambda bb, hp, i: (bb, i, hp)),
        out_shape=jax.ShapeDtypeStruct((b, t, d), BF16),
        scratch_shapes=[pltpu.VMEM((2, 2 * LANES, tq), BF16), pltpu.VMEM((4, 2, tk, tq), F32),
                        pltpu.VMEM((2, 2, 1, tk), F32), pltpu.VMEM((2, 2, dh + SUM_ROWS, tk), F32)],
        compiler_params=_cparams("arbitrary", "arbitrary", "arbitrary"),
        name="fox_attention_prompt",
    )(qt, kb, kaug, vt, qaugt, z)


def _fox_decode_kernel(q_ref, kp_ref, vp_ref, kn_ref, vn_ref, z_ref, cq_ref, ckt_ref, o_ref):
    hp = pl.program_id(1)
    t = q_ref.shape[0]
    p_len = kp_ref.shape[0]
    dh = LANES // 2
    lane = _iota((t, LANES), 1)
    q = q_ref[...]
    kp = kp_ref[...].astype(BF16)
    vp = vp_ref[...].astype(BF16)
    kn = kn_ref[...]
    vn = vn_ref[...]
    cq_all = cq_ref[...]
    hlane = _iota(cq_all.shape, 1)
    row2 = _iota((2 * t, t), 0)
    causal = _iota((2 * t, t), 1) <= jnp.where(row2 >= t, row2 - t, row2)
    qm, gate = [], []
    for hh in range(2):
        h = hp * 2 + hh
        qm.append(jnp.where(_div_pow2(lane, dh) == hh, q, jnp.zeros_like(q)))
        cq = jnp.sum(jnp.where(hlane == h, cq_all, 0.0), axis=-1, keepdims=True)
        gate.append(cq - ckt_ref[pl.ds(h, 1), :])
    qm = jnp.concatenate(qm, axis=0)
    gate = jnp.concatenate(gate, axis=0)
    s_p = _dot_nt(qm, kp) + gate[:, :p_len]
    s_n = jnp.where(causal, _dot_nt(qm, kn) + gate[:, p_len:], NEG_INF)
    m = jnp.maximum(jnp.max(s_p, axis=-1, keepdims=True), jnp.max(s_n, axis=-1, keepdims=True))
    e_p = jnp.exp(s_p - m)
    e_n = jnp.exp(s_n - m)
    den = jnp.sum(e_p, axis=-1, keepdims=True) + jnp.sum(e_n, axis=-1, keepdims=True)
    o2 = (_dot(e_p.astype(BF16), vp) + _dot(e_n.astype(BF16), vn)) / den
    o = jnp.where(lane < dh, o2[:t], o2[t:])
    o_ref[...] = (o * _silu(z_ref[...].astype(F32))).astype(BF16)


def _fox_attn_sample(q, k_past, v_past, kb, vb, z, cq_new, cum_t):
    b, t, d = q.shape
    p_len = k_past.shape[1]
    hpairs = d // LANES
    new_spec = pl.BlockSpec((None, t, LANES), lambda bb, hp: (bb, 0, hp))
    past_spec = pl.BlockSpec((None, p_len, LANES), lambda bb, hp: (bb, 0, hp))
    return pl.pallas_call(
        _fox_decode_kernel,
        grid=(b, hpairs),
        in_specs=[new_spec, past_spec, past_spec, new_spec, new_spec, new_spec,
                  pl.BlockSpec((None, t, H_F), lambda bb, hp: (bb, 0, 0)),
                  pl.BlockSpec((None, H_F, p_len + t), lambda bb, hp: (bb, 0, 0))],
        out_specs=new_spec,
        out_shape=jax.ShapeDtypeStruct((b, t, d), BF16),
        compiler_params=_cparams("arbitrary", "arbitrary"),
        name="fox_attention_sample",
    )(q, k_past, v_past, kb, vb, z, cq_new, cum_t)


def _diff_proj_kernel(x_ref, shift_ref, scale_ref, w_ref, q_ref, k32_ref, v32_ref, kb_ref, vb_ref, z_ref,
                      *, q_scale, transposed):
    u = _modulated(x_ref, shift_ref, scale_ref)
    _qkvz_outputs(u, w_ref, q_ref, k32_ref, v32_ref, kb_ref, vb_ref, z_ref, q_scale, transposed)


def _diff_proj(x, mod4, layer, boff, w_in, transposed):
    b, t, d = x.shape
    tm = _row_tile(t, 512)
    out_specs, out_shape = _qkvz_specs(b, t, d, tm, transposed)
    return pl.pallas_call(
        functools.partial(_diff_proj_kernel, q_scale=(d // (2 * H_D)) ** -0.5 * (LOG2E if transposed else 1.0),
                          transposed=transposed),
        grid=(b, t // tm),
        in_specs=[_rows_spec(tm, d)] + _mod_specs(layer, boff, d, (0, 1)) + [_const_spec((d, 4 * d))],
        out_specs=out_specs,
        out_shape=out_shape,
        compiler_params=_cparams("arbitrary", "arbitrary"),
        name="diff_in_proj",
    )(x, mod4, mod4, _qkvz_weights(w_in, d, transposed))


def _t5_thresholds():
    nb = N_BUCKETS // 2
    max_exact = nb // 2
    steps = nb - max_exact
    ratio = MAX_DISTANCE // max_exact
    out = []
    for kk in range(1, nb - max_exact):
        target = max_exact ** steps * ratio ** kk
        n = max_exact
        while n ** steps < target:
            n += 1
        out.append(n)
    return nb, max_exact, out


def _bias_kernel(tbl_ref, o_ref, *, q0, k0, keys_on_rows):
    h = pl.program_id(0)
    shape = o_ref.shape
    kdim, qdim = (0, 1) if keys_on_rows else (1, 0)
    rel = (k0 + _iota(shape, kdim)) - (q0 + _iota(shape, qdim))
    nb, max_exact, thr = _t5_thresholds()
    n = jnp.abs(rel)
    large = jnp.full(shape, max_exact, jnp.int32)
    for tval in thr:
        large = large + (n >= tval).astype(jnp.int32)
    bucket = jnp.where(rel > 0, nb, 0) + jnp.where(n < max_exact, n, large)
    acc = jnp.zeros(shape, F32)
    for bkt in range(N_BUCKETS):
        acc = jnp.where(bucket == bkt, tbl_ref[bkt * H_D + h], acc)
    if keys_on_rows:
        acc = (acc - tbl_ref[(nb - 1) * H_D + h]) * LOG2E
    o_ref[...] = acc


def _bias_tile(rel_table, q0, nq, k0, nk, keys_on_rows=False):
    shape = (nk, nq) if keys_on_rows else (nq, nk)
    return pl.pallas_call(
        functools.partial(_bias_kernel, q0=q0, k0=k0, keys_on_rows=keys_on_rows),
        grid=(H_D,),
        in_specs=[pl.BlockSpec(memory_space=pltpu.SMEM)],
        out_specs=pl.BlockSpec((None,) + shape, lambda h: (h, 0, 0)),
        out_shape=jax.ShapeDtypeStruct((H_D,) + shape, F32),
        compiler_params=_cparams("arbitrary"),
        name="t5_bias_tile",
    )(rel_table.astype(F32).reshape(N_BUCKETS * H_D))


def _diff_lambda(lam_ref):
    lam = lam_ref[...]
    s1 = jnp.sum(lam[0:1, :] * lam[1:2, :], axis=-1, keepdims=True)
    s2 = jnp.sum(lam[2:3, :] * lam[3:4, :], axis=-1, keepdims=True)
    return jnp.exp(s1) - jnp.exp(s2) + LAMBDA_INIT


def _diff_epilogue(o, z, subln_ref):
    on = o * lax.rsqrt(jnp.mean(o * o, axis=-1, keepdims=True) + NORM_EPS) * subln_ref[...]
    on = on * (1.0 - LAMBDA_INIT)
    return (on * _silu(z.astype(F32))).astype(BF16)


def _diff_attn_kernel(tbl_ref, qt_ref, k_ref, vt_ref, z_ref, biasm_ref, bias0_ref, bias1_ref, lam_ref, subln_ref,
                      o_ref, qcat_ref, s_ref, m_ref, acc_ref, *, tk):
    h = pl.program_id(1)
    qi = pl.program_id(2)
    tq = qt_ref.shape[1]
    assert tq == 2 * tk
    dh = LANES // 2
    row = _iota((LANES, tq), 0)
    nb, _, _ = _t5_thresholds()
    far = _split3(jnp.full((LANES, tq), tbl_ref[(nb - 1) * H_D + h], F32) * LOG2E)
    far_rows = jnp.zeros((LANES, tq), F32)
    for part in range(3):
        far_rows = jnp.where(row == part, far[part].astype(F32), far_rows)
    for br in range(2):
        qcat_ref[br, br * dh:(br + 1) * dh, :] = qt_ref[br * dh:(br + 1) * dh, :]
        qcat_ref[br, (1 - br) * dh:(2 - br) * dh, :] = jnp.zeros((dh, tq), BF16)
        qcat_ref[br, LANES:2 * LANES, :] = far_rows.astype(BF16)
    ones_aug = jnp.where(_iota((tk, LANES), 1) < 3, 1.0, 0.0).astype(BF16)
    m_ref[...] = jnp.full(m_ref.shape, NEG_INF, F32)
    acc_ref[...] = jnp.zeros(acc_ref.shape, F32)
    block_causal = _div_pow2(_iota((tk, tk), 0), ATTN_CHUNK) <= _div_pow2(_iota((tk, tk), 1), ATTN_CHUNK)

    def scores(j, slot):
        k0 = pl.multiple_of(j * tk, tk)
        kcat = jnp.concatenate([k_ref[pl.ds(k0, tk), :], ones_aug], axis=1)
        for br in range(2):
            s_ref[slot, br] = _dot(kcat, qcat_ref[br])

    def consume(j, slot, bias_ref=None, diagonal=(False, False), q0=0):
        k0 = pl.multiple_of(j * tk, tk)
        for br in range(2):
            for half in range(q0 // tk, 2):
                lanes = slice(half * tk, (half + 1) * tk)
                s = s_ref[slot, br, :, lanes]
                if bias_ref is not None:
                    s = s + bias_ref[:, lanes]
                if diagonal[half]:
                    s = jnp.where(block_causal, s, NEG_INF)
                p, alpha = _softmax_t_probs(s, m_ref.at[br, half])
                _softmax_t_accumulate(vt_ref[:, pl.ds(k0, tk)], p, alpha, acc_ref.at[br, half])

    def pair(cur, a, second_bias_ref=None):
        nxt = 1 - cur
        scores(a + 2, 2 * nxt)
        consume(a, 2 * cur)
        scores(a + 3, 2 * nxt + 1)
        consume(a + 1, 2 * cur + 1, bias_ref=second_bias_ref)

    def last_pair(cur):
        consume(2 * qi, 2 * cur, bias_ref=bias0_ref, diagonal=(True, False))
        consume(2 * qi + 1, 2 * cur + 1, bias_ref=bias1_ref, diagonal=(False, True), q0=tk)

    scores(0, 0)
    scores(1, 1)
    n_plain = jnp.maximum(qi - 1, 0)

    def body(i, carry):
        pair(0, 4 * i)
        pair(1, 4 * i + 2)
        return carry

    lax.fori_loop(0, lax.shift_right_logical(n_plain, 1), body, 0)

    @pl.when(qi == 0)
    def _():
        last_pair(0)

    @pl.when((qi & 1) == 1)
    def _():
        pair(0, 2 * qi - 2, second_bias_ref=biasm_ref)
        last_pair(1)

    @pl.when(jnp.logical_and(qi >= 2, (qi & 1) == 0))
    def _():
        pair(0, 2 * qi - 4)
        pair(1, 2 * qi - 2, second_bias_ref=biasm_ref)
        last_pair(0)
    branch = [jnp.concatenate([_softmax_t_result(acc_ref.at[br, half], LANES) for half in range(2)], axis=1)
              for br in range(2)]
    o_t = branch[0] - _diff_lambda(lam_ref) * branch[1]
    o_ref[...] = _diff_epilogue(o_t.T, z_ref[...], subln_ref)


def _lam_pack(lam_q1, lam_k1, lam_q2, lam_k2):
    rows = jnp.stack([lam_q1, lam_k1, lam_q2, lam_k2]).astype(F32)
    return jnp.pad(rows, ((0, SUBLANES - 4), (0, LANES - rows.shape[1])))


def _diff_attn_prompt(qt, kb, vt, z, rel_table, lam, subln_w):
    b, d, t = qt.shape
    tq, tk = ATTN_TQ, ATTN_TK
    assert t % tq == 0 and tk % ATTN_CHUNK == 0 and tk >= MAX_DISTANCE
    biasm = _bias_tile(rel_table, tk, tq, 0, tk, keys_on_rows=True)
    bias0 = _bias_tile(rel_table, 0, tq, 0, tk, keys_on_rows=True)
    bias1 = _bias_tile(rel_table, 0, tq, tk, tk, keys_on_rows=True)
    rows_spec = pl.BlockSpec((None, tq, LANES), lambda bb, h, i: (bb, i, h))
    bias_spec = pl.BlockSpec((None, tk, tq), lambda bb, h, i: (h, 0, 0))
    return pl.pallas_call(
        functools.partial(_diff_attn_kernel, tk=tk),
        grid=(b, H_D, t // tq),
        in_specs=[pl.BlockSpec(memory_space=pltpu.SMEM),
                  pl.BlockSpec((None, LANES, tq), lambda bb, h, i: (bb, h, i)),
                  pl.BlockSpec((None, t, LANES), lambda bb, h, i: (bb, 0, h)),
                  pl.BlockSpec((None, LANES, t), lambda bb, h, i: (bb, h, 0)),
                  rows_spec, bias_spec, bias_spec, bias_spec,
                  pl.BlockSpec((SUBLANES, LANES), lambda bb, h, i: (0, 0)),
                  pl.BlockSpec((1, LANES), lambda bb, h, i: (0, 0))],
        out_specs=rows_spec,
        out_shape=jax.ShapeDtypeStruct((b, t, d), BF16),
        scratch_shapes=[pltpu.VMEM((2, 2 * LANES, tq), BF16), pltpu.VMEM((4, 2, tk, tq), F32),
                        pltpu.VMEM((2, 2, 1, tk), F32), pltpu.VMEM((2, 2, LANES + SUM_ROWS, tk), F32)],
        compiler_params=_cparams("arbitrary", "arbitrary", "arbitrary"),
        name="diff_attention_prompt",
    )(rel_table.astype(F32).reshape(N_BUCKETS * H_D), qt, kb, vt, z, biasm, bias0, bias1, lam,
      subln_w.astype(F32).reshape(1, LANES))


def _diff_decode_kernel(q_ref, kp_ref, vp_ref, kn_ref, vn_ref, z_ref, bias_ref, lam_ref, subln_ref, o_ref,
                        *, p_len):
    t = q_ref.shape[0]
    dh = LANES // 2
    lane = _iota((t, LANES), 1)
    q = q_ref[...]
    kp = kp_ref[...].astype(BF16)
    vp = vp_ref[...].astype(BF16)
    kn = kn_ref[...]
    vn = vn_ref[...]
    bias = jnp.concatenate([bias_ref[...], bias_ref[...]], axis=0)
    row_p = _iota((2 * t, p_len), 0)
    row_n = _iota((2 * t, t), 0)
    qp_chunk = _div_pow2(p_len + jnp.where(row_p >= t, row_p - t, row_p), ATTN_CHUNK)
    qn_chunk = _div_pow2(p_len + jnp.where(row_n >= t, row_n - t, row_n), ATTN_CHUNK)
    kp_chunk = _div_pow2(_iota((2 * t, p_len), 1), ATTN_CHUNK)
    kn_chunk = _div_pow2(p_len + _iota((2 * t, t), 1), ATTN_CHUNK)
    qm = jnp.concatenate([jnp.where(_div_pow2(lane, dh) == br, q, jnp.zeros_like(q)) for br in range(2)], axis=0)
    s_p = jnp.where(kp_chunk <= qp_chunk, _dot_nt(qm, kp) + bias[:, :p_len], NEG_INF)
    s_n = jnp.where(kn_chunk <= qn_chunk, _dot_nt(qm, kn) + bias[:, p_len:], NEG_INF)
    m = jnp.maximum(jnp.max(s_p, axis=-1, keepdims=True), jnp.max(s_n, axis=-1, keepdims=True))
    e_p = jnp.exp(s_p - m)
    e_n = jnp.exp(s_n - m)
    den = jnp.sum(e_p, axis=-1, keepdims=True) + jnp.sum(e_n, axis=-1, keepdims=True)
    o2 = (_dot(e_p.astype(BF16), vp) + _dot(e_n.astype(BF16), vn)) / den
    o = o2[:t] - _diff_lambda(lam_ref) * o2[t:]
    o_ref[...] = _diff_epilogue(o, z_ref[...], subln_ref)


def _diff_attn_sample(q, k_past, v_past, kb, vb, z, rel_table, lam, subln_w):
    b, t, d = q.shape
    p_len = k_past.shape[1]
    bias = _bias_tile(rel_table, p_len, t, 0, p_len + t)
    new_spec = pl.BlockSpec((None, t, LANES), lambda bb, h: (bb, 0, h))
    past_spec = pl.BlockSpec((None, p_len, LANES), lambda bb, h: (bb, 0, h))
    return pl.pallas_call(
        functools.partial(_diff_decode_kernel, p_len=p_len),
        grid=(b, H_D),
        in_specs=[new_spec, past_spec, past_spec, new_spec, new_spec, new_spec,
                  pl.BlockSpec((None, t, p_len + t), lambda bb, h: (h, 0, 0)),
                  pl.BlockSpec((SUBLANES, LANES), lambda bb, h: (0, 0)),
                  pl.BlockSpec((1, LANES), lambda bb, h: (0, 0))],
        out_specs=new_spec,
        out_shape=jax.ShapeDtypeStruct((b, t, d), BF16),
        compiler_params=_cparams("arbitrary", "arbitrary"),
        name="diff_attention_sample",
    )(q, k_past, v_past, kb, vb, z, bias, lam, subln_w.astype(F32).reshape(1, LANES))


def _rope_kernel(inv_ref, cos_ref, sin_ref, *, start):
    t, w = cos_ref.shape
    pos = (start + pl.program_id(0) * t + _iota((t, w), 0)).astype(F32)
    ang = pos * inv_ref[...]
    even = (_iota((t, w), 1) & 1) == 0
    cos_ref[...] = jnp.cos(ang)
    sn = jnp.sin(ang)
    sin_ref[...] = jnp.where(even, -sn, sn)


def _rope_tables(t, start, dk):
    inv_half = np.power(np.float32(ROPE_BASE), -np.arange(0, dk, 2, dtype=np.float32) / np.float32(dk))
    inv = jnp.asarray(np.repeat(inv_half.astype(np.float32), 2).reshape(1, dk))
    tt = _row_tile(t, 512)
    return pl.pallas_call(
        functools.partial(_rope_kernel, start=start),
        grid=(t // tt,),
        in_specs=[pl.BlockSpec((1, dk), lambda i: (0, 0))],
        out_specs=[pl.BlockSpec((tt, dk), lambda i: (i, 0))] * 2,
        out_shape=[jax.ShapeDtypeStruct((t, dk), F32)] * 2,
        compiler_params=_cparams("arbitrary"),
        name="rope_tables",
    )(inv)


def _rotate_pairs(x, cos, sin_signed):
    slabs = []
    for c0 in range(0, x.shape[-1], LANES):
        xs = x[:, c0:c0 + LANES]
        even = (_iota(xs.shape, 1) & 1) == 0
        slabs.append(jnp.where(even, pltpu.roll(xs, LANES - 1, 1), pltpu.roll(xs, 1, 1)))
    return x * cos + jnp.concatenate(slabs, axis=1) * sin_signed


def _ret_proj_kernel(x_ref, shift_ref, scale_ref, w_ref, cos_ref, sin_ref, q_ref, k_ref, v_ref, z_ref,
                     *, q_scale):
    u = _modulated(x_ref, shift_ref, scale_ref)
    d = x_ref.shape[-1]
    dk = cos_ref.shape[-1]
    cos = cos_ref[...]
    sn = sin_ref[...]
    for h in range(d // dk):
        qh = _dot(u, w_ref[:, h * dk:(h + 1) * dk])
        q_ref[:, h * dk:(h + 1) * dk] = (_rotate_pairs(qh, cos, sn) * q_scale).astype(BF16)
        kh = _dot(u, w_ref[:, d + h * dk:d + (h + 1) * dk])
        k_ref[:, h * dk:(h + 1) * dk] = _rotate_pairs(kh, cos, sn).astype(BF16)
    for s in range(2):
        v_ref[:, s * d:(s + 1) * d] = _dot(u, w_ref[:, (2 + s) * d:(3 + s) * d]).astype(BF16)
        z_ref[:, s * d:(s + 1) * d] = _dot(u, w_ref[:, (4 + s) * d:(5 + s) * d]).astype(BF16)


def _ret_proj(x, mod4, layer, boff, w_in, cos, sin_signed):
    b, t, d = x.shape
    dk = d // H_R
    tm = _row_tile(t, 512)
    tab_spec = pl.BlockSpec((tm, dk), lambda bb, i: (i, 0))
    return pl.pallas_call(
        functools.partial(_ret_proj_kernel, q_scale=dk ** -0.5),
        grid=(b, t // tm),
        in_specs=[_rows_spec(tm, d)] + _mod_specs(layer, boff, d, (0, 1)) + [_const_spec((d, 6 * d)), tab_spec, tab_spec],
        out_specs=[_rows_spec(tm, d), _rows_spec(tm, d), _rows_spec(tm, 2 * d), _rows_spec(tm, 2 * d)],
        out_shape=[jax.ShapeDtypeStruct((b, t, d), BF16), jax.ShapeDtypeStruct((b, t, d), BF16),
                   jax.ShapeDtypeStruct((b, t, 2 * d), BF16), jax.ShapeDtypeStruct((b, t, 2 * d), BF16)],
        compiler_params=_cparams("arbitrary", "arbitrary"),
        name="ret_in_proj",
    )(x, mod4, mod4, w_in.astype(BF16), cos, sin_signed)


def _ret_kernel(q_ref, k_ref, v_ref, z_ref, s0_ref, gn_ref, o_ref, s_ref, intra_ref):
    ti = pl.program_id(1)
    lr = q_ref.shape[0]
    dk = q_ref.shape[-1] // H_R
    dv = v_ref.shape[-1] // H_R
    heads = range(H_R)
    log_gamma = [math.log1p(-(2.0 ** (-5.0 - h))) for h in heads]

    @pl.when(ti == 0)
    def _():
        s_ref[...] = s0_ref[...]
        rel = (_iota((lr, lr), 0) - _iota((lr, lr), 1)).astype(F32)
        for h in heads:
            intra_ref[h] = jnp.where(rel >= 0, jnp.exp(log_gamma[h] * jnp.maximum(rel, 0.0)), 0.0)

    idx = _iota((lr, 1), 0).astype(F32)
    for h in heads:
        q_dec = jnp.exp(log_gamma[h] * (idx + 1.0))
        k_dec = jnp.exp(log_gamma[h] * (lr - 1.0 - idx))
        qh = q_ref[:, h * dk:(h + 1) * dk]
        kh = k_ref[:, h * dk:(h + 1) * dk]
        vh = v_ref[:, h * dv:(h + 1) * dv]
        s = s_ref[h]
        att = _dot_nt(qh, kh) * intra_ref[h]
        o = _dot(att.astype(BF16), vh) + _dot(qh, s.astype(BF16)) * q_dec
        s_ref[h] = s * math.exp(log_gamma[h] * lr) + _dot_tn((kh.astype(F32) * k_dec).astype(BF16), vh)
        mu = jnp.mean(o, axis=-1, keepdims=True)
        oc = o - mu
        var = jnp.mean(oc * oc, axis=-1, keepdims=True)
        on = oc * lax.rsqrt(var + LN_EPS) * gn_ref[:, h * dv:(h + 1) * dv]
        zz = z_ref[:, h * dv:(h + 1) * dv].astype(F32)
        o_ref[:, h * dv:(h + 1) * dv] = (on * _silu(zz)).astype(BF16)


def _ret_mix(q, k, v, z, s0, gn_w):
    b, t, d = q.shape
    dk = d // H_R
    dv = v.shape[-1] // H_R
    lr = _row_tile(t, 256)
    state_spec = pl.BlockSpec((None, H_R, dk, dv), lambda bb, i: (bb, 0, 0, 0))
    return pl.pallas_call(
        _ret_kernel,
        grid=(b, t // lr),
        in_specs=[_rows_spec(lr, d), _rows_spec(lr, d), _rows_spec(lr, 2 * d), _rows_spec(lr, 2 * d), state_spec,
                  _const_spec((1, 2 * d))],
        out_specs=[_rows_spec(lr, 2 * d), state_spec],
        out_shape=[jax.ShapeDtypeStruct((b, t, 2 * d), BF16), jax.ShapeDtypeStruct((b, H_R, dk, dv), F32)],
        scratch_shapes=[pltpu.VMEM((H_R, lr, lr), F32)],
        compiler_params=_cparams("arbitrary", "arbitrary"),
        name="retention_mixer",
    )(q, k, v, z, s0.astype(F32), gn_w.astype(F32).reshape(1, 2 * d))


def _run_group(x, mod4, boff, state_gdn, state_gdn_conv, cache_fox_k, cache_fox_v, cache_fox_logf,
               cache_diff_k, cache_diff_v, state_ret, start, p):
    b, t, d = x.shape
    dk_g = d // H_G

    if state_gdn is None:
        state_gdn = jnp.zeros((b, H_G, dk_g, dk_g), F32)
        state_gdn_conv = jnp.zeros((b, CONV_W - 1, 3 * d), F32)
    q, k, v, gates, z, tail = _gdn_proj(x, mod4, 0, boff, p["gdn_w_in"], state_gdn_conv, p["gdn_conv_w"],
                                        p["gdn_a_log"], p["gdn_dt_bias"])
    o, gdn_state = _gdn_mix(q, k, v, gates, z, state_gdn, p["gdn_norm_w"])
    gdn_conv = tail[:, SUBLANES - (CONV_W - 1):, :]
    x = _out_proj(o, x, mod4, 0, boff, p["gdn_w_out"], p["ln_g"][0], p["ln_b"][0])

    prompt = cache_fox_k is None
    q, k32, v32, kb, vb, z, logf = _fox_proj(x, mod4, 1, boff, p["fox_w_in"], p["fox_b_f"], transposed=prompt)
    if prompt:
        kaug, qaugt = _fox_aug(logf)
        o = _fox_attn_prompt(q, kb, kaug, vb, qaugt, z)
    else:
        zero_c = jnp.zeros((b, 1, H_F), F32)
        p_len = cache_fox_k.shape[1]
        cum_pn, cum_pt = _cumsum_time(cache_fox_logf.astype(F32), zero_c)
        cum_n, cum_nt = _cumsum_time(logf, cum_pn[:, p_len - 1:, :])
        o = _fox_attn_sample(q, cache_fox_k.reshape(b, p_len, d), cache_fox_v.reshape(b, p_len, d), kb, vb, z,
                             cum_n, jnp.concatenate([cum_pt, cum_nt], axis=2))
    fox_k = k32.reshape(b, t, H_F, d // H_F)
    fox_v = v32.reshape(b, t, H_F, d // H_F)
    x = _out_proj(o, x, mod4, 1, boff, p["fox_w_out"], p["ln_g"][1], p["ln_b"][1])

    q, k32, v32, kb, vb, z = _diff_proj(x, mod4, 2, boff, p["diff_w_in"], transposed=prompt)
    lam = _lam_pack(p["diff_lam_q1"], p["diff_lam_k1"], p["diff_lam_q2"], p["diff_lam_k2"])
    if prompt:
        o = _diff_attn_prompt(q, kb, vb, z, p["rel_bias_table"], lam, p["diff_subln_w"])
    else:
        p_len = cache_diff_k.shape[1]
        o = _diff_attn_sample(q, cache_diff_k.reshape(b, p_len, d), cache_diff_v.reshape(b, p_len, d), kb, vb, z,
                              p["rel_bias_table"], lam, p["diff_subln_w"])
    diff_k = k32.reshape(b, t, H_D, 2, d // (2 * H_D))
    diff_v = v32.reshape(b, t, H_D, d // H_D)
    x = _out_proj(o, x, mod4, 2, boff, p["diff_w_out"], p["ln_g"][2], p["ln_b"][2])

    dk_r = d // H_R
    cos, sin_signed = _rope_tables(t, start, dk_r)
    q, k, v, z = _ret_proj(x, mod4, 3, boff, p["ret_w_in"], cos, sin_signed)
    if state_ret is None:
        state_ret = jnp.zeros((b, H_R, dk_r, 2 * d // H_R), F32)
    o, ret_state = _ret_mix(q, k, v, z, state_ret, p["ret_gn_w"])
    x = _out_proj(o, x, mod4, 3, boff, p["ret_w_out"], p["ln_g"][3], p["ln_b"][3])

    return x, gdn_state, gdn_conv, fox_k, fox_v, logf, diff_k, diff_v, ret_state


def kernel(x_prompt, x_sample, c_prompt, c_sample, state_gdn, state_gdn_conv, cache_fox_k, cache_fox_v, cache_fox_logf, cache_diff_k, cache_diff_v, state_ret, ada_w, ada_b, ln_g, ln_b, gdn_w_in, gdn_conv_w, gdn_a_log, gdn_dt_bias, gdn_norm_w, gdn_w_out, fox_w_in, fox_b_f, fox_w_out, rel_bias_table, diff_w_in, diff_lam_q1, diff_lam_k1, diff_lam_q2, diff_lam_k2, diff_subln_w, diff_w_out, ret_w_in, ret_gn_w, ret_w_out):
    p = dict(ln_g=ln_g, ln_b=ln_b, gdn_w_in=gdn_w_in, gdn_conv_w=gdn_conv_w, gdn_a_log=gdn_a_log,
             gdn_dt_bias=gdn_dt_bias, gdn_norm_w=gdn_norm_w, gdn_w_out=gdn_w_out, fox_w_in=fox_w_in,
             fox_b_f=fox_b_f, fox_w_out=fox_w_out, rel_bias_table=rel_bias_table, diff_w_in=diff_w_in,
             diff_lam_q1=diff_lam_q1, diff_lam_k1=diff_lam_k1, diff_lam_q2=diff_lam_q2, diff_lam_k2=diff_lam_k2,
             diff_subln_w=diff_subln_w, diff_w_out=diff_w_out, ret_w_in=ret_w_in, ret_gn_w=ret_gn_w,
             ret_w_out=ret_w_out)
    bp = x_prompt.shape[0]
    d = x_prompt.shape[-1]
    mod = _modulation(jnp.concatenate([c_prompt, c_sample], axis=0), ada_w, ada_b)
    mod4 = mod.reshape(mod.shape[0], mod.shape[1], 1, 3 * d)
    outs_p = _run_group(x_prompt, mod4, 0, None, None, None, None, None, None, None, None, 0, p)
    outs_s = _run_group(x_sample, mod4, bp, state_gdn, state_gdn_conv, cache_fox_k, cache_fox_v, cache_fox_logf,
                        cache_diff_k, cache_diff_v, state_ret, cache_fox_k.shape[1], p)
    return (outs_p[0], outs_s[0]) + tuple(outs_p[1:]) + tuple(outs_s[1:])
```
